```python
import math
import jax, jax.numpy as jnp
from jax import lax
import numpy as np

D_MODEL = 1024
BATCH = 2
SEQ = 8192
DEPTH = 1
DEC_BATCH = 4
DEC_SEQ = 4096
PAST_LEN = 128

MIX_WIDTH = D_MODEL
MLA_HEADS = 8
QK_NOPE = 64
QK_ROPE = 32
V_HEAD = 64
MLA_WIDTH = MLA_HEADS * V_HEAD
Q_LORA = 256
KV_LORA = 256
FNET_GROUPS = 4
FNET_WIDTH = MIX_WIDTH - MLA_WIDTH
FNET_GROUP_DIM = FNET_WIDTH // FNET_GROUPS
IN_COLS = Q_LORA + KV_LORA + QK_ROPE + FNET_WIDTH
D_FF = 2816
CONV_W = 3
ROPE_THETA = 10000.0
RMS_EPS = 1e-6
Q_BLOCK = 128
ATTN_SCALE = 1.0 / math.sqrt(QK_NOPE + QK_ROPE)

kernel_name = "hybrid_mla_fnet_convffn_encoder"


def rmsnorm(x, g):
    xf = x.astype(jnp.float32)
    out = xf * lax.rsqrt(jnp.mean(xf * xf, axis=-1, keepdims=True) + RMS_EPS)
    return (out * g.astype(jnp.float32)).astype(x.dtype)


def rope_tables(seq_len):
    pos = jnp.arange(seq_len, dtype=jnp.float32)
    inv_freq = ROPE_THETA ** (-jnp.arange(0, QK_ROPE, 2, dtype=jnp.float32) / QK_ROPE)
    ang = pos[:, None] * inv_freq[None, :]
    return jnp.cos(ang), jnp.sin(ang)


def apply_rope(x, cos, sin):
    xf = x.astype(jnp.float32)
    x1, x2 = jnp.split(xf, 2, axis=-1)
    out = jnp.concatenate([x1 * cos - x2 * sin, x2 * cos + x1 * sin], axis=-1)
    return out.astype(x.dtype)


def mla(c_q, c_kv, k_rope_raw, g_q, w_uq, g_kv, w_ukv):
    b, s, _ = c_q.shape
    cos, sin = rope_tables(s)
    q = (rmsnorm(c_q, g_q) @ w_uq).reshape(b, s, MLA_HEADS, QK_NOPE + QK_ROPE)
    q_nope = q[..., :QK_NOPE]
    q_rope = apply_rope(q[..., QK_NOPE:], cos[None, :, None, :], sin[None, :, None, :])
    kv = (rmsnorm(c_kv, g_kv) @ w_ukv).reshape(b, s, MLA_HEADS, QK_NOPE + V_HEAD)
    k_nope = kv[..., :QK_NOPE]
    v = kv[..., QK_NOPE:]
    k_rope = apply_rope(k_rope_raw, cos[None], sin[None])

    nb = s // Q_BLOCK
    qn_blk = q_nope.reshape(b, nb, Q_BLOCK, MLA_HEADS, QK_NOPE).transpose(1, 0, 2, 3, 4)
    qr_blk = q_rope.reshape(b, nb, Q_BLOCK, MLA_HEADS, QK_ROPE).transpose(1, 0, 2, 3, 4)

    def block(args):
        qn, qr = args
        scores = (jnp.einsum('bqhd,bkhd->bhqk', qn, k_nope, preferred_element_type=jnp.float32)
                  + jnp.einsum('bqhr,bkr->bhqk', qr, k_rope, preferred_element_type=jnp.float32)) * ATTN_SCALE
        probs = jax.nn.softmax(scores, axis=-1).astype(v.dtype)
        return jnp.einsum('bhqk,bkhd->bqhd', probs, v)

    out = lax.map(block, (qn_blk, qr_blk))
    return out.transpose(1, 0, 2, 3, 4).reshape(b, s, MLA_WIDTH)


def fourier_mix(u, w_fnet):
    b, s, _ = u.shape
    ug = u.reshape(b, s, FNET_GROUPS, FNET_GROUP_DIM).astype(jnp.float32)
    fr = jnp.fft.fftn(ug, axes=(1, 3), norm="ortho").real.astype(u.dtype)
    out = jnp.einsum('bsgc,gcd->bsgd', fr, w_fnet)
    return out.reshape(b, s, FNET_WIDTH)


def dwconv_centred(h, w, bias):
    s = h.shape[1]
    half = CONV_W // 2
    hp = jnp.pad(h, ((0, 0), (half, half), (0, 0)))
    out = bias
    for j in range(CONV_W):
        out = out + hp[:, j:j + s, :] * w[j]
    return out


def conv_ffn(h, w_gate, w_up, conv_w, conv_b, w_down):
    gate = dwconv_centred(h @ w_gate, conv_w, conv_b)
    act = jax.nn.gelu(gate, approximate=True) * (h @ w_up)
    return act @ w_down


def trunk(x, g_pre_mix, w_in, g_q, w_uq, g_kv, w_ukv, w_fnet, w_out, g_post_mix,
          g_pre_ffn, w_gate, w_up, conv_w, conv_b, w_down, g_post_ffn):
    for l in range(DEPTH):
        h = rmsnorm(x, g_pre_mix[l])
        p = h @ w_in[l]
        c_q = p[..., :Q_LORA]
        c_kv = p[..., Q_LORA:Q_LORA + KV_LORA]
        k_rope = p[..., Q_LORA + KV_LORA:Q_LORA + KV_LORA + QK_ROPE]
        f_in = p[..., Q_LORA + KV_LORA + QK_ROPE:]
        a = mla(c_q, c_kv, k_rope, g_q[l], w_uq[l], g_kv[l], w_ukv[l])
        f = fourier_mix(f_in, w_fnet[l])
        mix = jnp.concatenate([a, f], axis=-1) @ w_out[l]
        x = x + rmsnorm(mix, g_post_mix[l])
        h = rmsnorm(x, g_pre_ffn[l])
        x = x + rmsnorm(conv_ffn(h, w_gate[l], w_up[l], conv_w[l], conv_b[l], w_down[l]), g_post_ffn[l])
    return x


def setup_inputs(seed: int = 0) -> dict:
    key = jax.random.key(seed)
    ks = jax.random.split(key, 20)
    f32 = jnp.float32

    def nrm(k, shape, fan_in):
        return jax.random.normal(k, shape, f32) * (fan_in ** -0.5)

    def gain(k, dim):
        return 1.0 + 0.02 * jax.random.normal(k, (DEPTH, dim), f32)

    return {
        "x_prompt": jax.random.normal(ks[0], (BATCH, SEQ, D_MODEL), f32),
        "x_sample": jax.random.normal(ks[1], (DEC_BATCH, DEC_SEQ, D_MODEL), f32),
        "g_pre_mix": gain(ks[2], D_MODEL),
        "w_in": nrm(ks[3], (DEPTH, D_MODEL, IN_COLS), D_MODEL),
        "g_q": gain(ks[4], Q_LORA),
        "w_uq": nrm(ks[5], (DEPTH, Q_LORA, MLA_HEADS * (QK_NOPE + QK_ROPE)), Q_LORA),
        "g_kv": gain(ks[6], KV_LORA),
        "w_ukv": nrm(ks[7], (DEPTH, KV_LORA, MLA_HEADS * (QK_NOPE + V_HEAD)), KV_LORA),
        "w_fnet": nrm(ks[8], (DEPTH, FNET_GROUPS, FNET_GROUP_DIM, FNET_GROUP_DIM), FNET_GROUP_DIM),
        "w_out": nrm(ks[9], (DEPTH, MIX_WIDTH, D_MODEL), MIX_WIDTH),
        "g_post_mix": gain(ks[10], D_MODEL),
        "g_pre_ffn": gain(ks[11], D_MODEL),
        "w_gate": nrm(ks[12], (DEPTH, D_MODEL, D_FF), D_MODEL),
        "w_up": nrm(ks[13], (DEPTH, D_MODEL, D_FF), D_MODEL),
        "conv_w": nrm(ks[14], (DEPTH, CONV_W, D_FF), CONV_W),
        "conv_b": 0.01 * jax.random.normal(ks[15], (DEPTH, D_FF), f32),
        "w_down": nrm(ks[16], (DEPTH, D_FF, D_MODEL), D_FF),
        "g_post_ffn": gain(ks[17], D_MODEL),
    }


def reference(x_prompt, x_sample, g_pre_mix, w_in, g_q, w_uq, g_kv, w_ukv, w_fnet, w_out,
              g_post_mix, g_pre_ffn, w_gate, w_up, conv_w, conv_b, w_down, g_post_ffn):
    y_prompt = trunk(x_prompt, g_pre_mix, w_in, g_q, w_uq, g_kv, w_ukv, w_fnet, w_out, g_post_mix,
                     g_pre_ffn, w_gate, w_up, conv_w, conv_b, w_down, g_post_ffn)
    y_sample = trunk(x_sample, g_pre_mix, w_in, g_q, w_uq, g_kv, w_ukv, w_fnet, w_out, g_post_mix,
                     g_pre_ffn, w_gate, w_up, conv_w, conv_b, w_down, g_post_ffn)
    return (y_prompt, y_sample)
```

```python
import functools
import math

import jax
import jax.numpy as jnp
import numpy as np
from jax.experimental import pallas as pl
from jax.experimental.pallas import tpu as pltpu

D_MODEL = 1024
MLA_HEADS = 8
QK_NOPE = 64
QK_ROPE = 32
V_HEAD = 64
MLA_WIDTH = MLA_HEADS * V_HEAD
Q_LORA = 256
KV_LORA = 256
FNET_GROUPS = 4
FNET_WIDTH = 512
FNET_GROUP_DIM = 128
D_FF = 2816
ROPE_THETA = 10000.0
RMS_EPS = 1e-6
ATTN_SCALE = 1.0 / math.sqrt(QK_NOPE + QK_ROPE)

LANES = 128
HEAD_PAD = 128
HALF_ROPE = QK_ROPE // 2
DFT_N2 = 128
VMEM_LIMIT = 56 * 1024 * 1024

BF16 = jnp.bfloat16
F32 = jnp.float32


def _rms(x, g):
    return x * jax.lax.rsqrt(jnp.mean(x * x, axis=-1, keepdims=True) + RMS_EPS) * g


def _dot(a, b):
    return jnp.dot(a, b, preferred_element_type=F32)


def _params(sem):
    return pltpu.CompilerParams(dimension_semantics=sem, vmem_limit_bytes=VMEM_LIMIT)


def _const_spec(shape):
    zeros = (0,) * len(shape)
    return pl.BlockSpec(shape, lambda *_: zeros)


def _proj_kernel(x_ref, gpre_ref, win_ref, gq_ref, wuq_ref, gkv_ref, wukv_ref, cs_ref,
                 cos_ref, sina_ref, sinb_ref, qt_ref, k_ref, vt_ref, z_ref):
    h = _rms(x_ref[0], gpre_ref[...]).astype(BF16)
    p = _dot(h, win_ref[...])
    cq = _rms(p[:, :Q_LORA], gq_ref[...]).astype(BF16)
    q = _dot(cq, wuq_ref[...])
    ckv = _rms(p[:, Q_LORA:Q_LORA + KV_LORA], gkv_ref[...]).astype(BF16)
    kv = _dot(ckv, wukv_ref[...])
    cos, sina, sinb = cos_ref[...], sina_ref[...], sinb_ref[...]

    def rope(t):
        return (t * cos + pltpu.roll(t, HALF_ROPE, 1) * sina
                + pltpu.roll(t, HEAD_PAD - HALF_ROPE, 1) * sinb)

    kr = rope(p[:, 2 * FNET_WIDTH:])
    for hd in range(MLA_HEADS):
        sl = slice(hd * HEAD_PAD, (hd + 1) * HEAD_PAD)
        qh = rope(q[:, sl]) * ATTN_SCALE
        qt_ref[0, sl, :] = qh.T.astype(BF16)
        k_ref[0, :, sl] = (kv[:, sl] + kr).astype(BF16)
    vt_ref[0] = kv[:, MLA_HEADS * HEAD_PAD:].T.astype(BF16)
    f = p[:, Q_LORA + KV_LORA:2 * FNET_WIDTH].astype(BF16)
    for g in range(FNET_GROUPS):
        sl = slice(g * FNET_GROUP_DIM, (g + 1) * FNET_GROUP_DIM)
        ab = _dot(f[:, sl], cs_ref[...])
        z_ref[0, 0, :, sl] = ab[:, :FNET_GROUP_DIM].astype(BF16)
        z_ref[0, 1, :, sl] = ab[:, FNET_GROUP_DIM:].astype(BF16)


def _proj(x, gpre, win, gq, wuq, gkv, wukv, cs, cos, sina, sinb, tm):
    b, s, _ = x.shape
    wide = MLA_HEADS * HEAD_PAD
    tok = lambda w: pl.BlockSpec((1, tm, w), lambda bi, i: (bi, i, 0))
    tab = pl.BlockSpec((tm, HEAD_PAD), lambda bi, i: (i, 0))
    return pl.pallas_call(
        _proj_kernel,
        grid=(b, s // tm),
        in_specs=[tok(D_MODEL), _const_spec(gpre.shape), _const_spec(win.shape),
                  _const_spec(gq.shape), _const_spec(wuq.shape), _const_spec(gkv.shape),
                  _const_spec(wukv.shape), _const_spec(cs.shape), tab, tab, tab],
        out_specs=[pl.BlockSpec((1, wide, tm), lambda bi, i: (bi, 0, i)),
                   tok(wide),
                   pl.BlockSpec((1, MLA_WIDTH, tm), lambda bi, i: (bi, 0, i)),
                   pl.BlockSpec((1, 2, tm, FNET_WIDTH), lambda bi, i: (bi, 0, i, 0))],
        out_shape=[jax.ShapeDtypeStruct((b, wide, s), BF16),
                   jax.ShapeDtypeStruct((b, s, wide), BF16),
                   jax.ShapeDtypeStruct((b, MLA_WIDTH, s), BF16),
                   jax.ShapeDtypeStruct((b, 2, s, FNET_WIDTH), BF16)],
        compiler_params=_params(("parallel", "parallel")),
        name="proj",
    )(x, gpre, win, gq, wuq, gkv, wukv, cs, cos, sina, sinb)


def _attn_kernel(qt_ref, k_ref, vt_ref, o_ref, *, tk, heads):
    s = k_ref.shape[1]
    tq = qt_ref.shape[2]
    outs = []
    for hd in range(heads):
        qt = qt_ref[0, hd * HEAD_PAD:(hd + 1) * HEAD_PAD, :]

        def body(c, carry):
            m, l, acc = carry
            off = pl.multiple_of(c * tk, tk)
            kc = k_ref[0, pl.ds(off, tk), hd * HEAD_PAD:(hd + 1) * HEAD_PAD]
            st = _dot(kc, qt)
            m_new = jnp.maximum(m, jnp.max(st, axis=0, keepdims=True))
            alpha = jnp.exp(m - m_new)
            p = jnp.exp(st - m_new)
            l = alpha * l + jnp.sum(p, axis=0, keepdims=True)
            vc = vt_ref[0, hd * V_HEAD:(hd + 1) * V_HEAD, pl.ds(off, tk)]
            acc = alpha * acc + _dot(vc, p.astype(BF16))
            return m_new, l, acc

        init = (jnp.full((1, tq), -jnp.inf, F32), jnp.zeros((1, tq), F32),
                jnp.zeros((V_HEAD, tq), F32))
        m, l, acc = jax.lax.fori_loop(0, s // tk, body, init)
        outs.append(acc / l)
    o_ref[0] = jnp.concatenate(outs, axis=0).T.astype(BF16)


def _attn(qt, k, vt, tq, tk):
    b, _, s = qt.shape
    heads = LANES // V_HEAD
    groups = MLA_HEADS // heads
    return pl.pallas_call(
        functools.partial(_attn_kernel, tk=tk, heads=heads),
        grid=(b, groups, s // tq),
        in_specs=[pl.BlockSpec((1, heads * HEAD_PAD, tq), lambda bi, j, i: (bi, j, i)),
                  pl.BlockSpec((1, s, heads * HEAD_PAD), lambda bi, j, i: (bi, 0, j)),
                  pl.BlockSpec((1, heads * V_HEAD, s), lambda bi, j, i: (bi, j, 0))],
        out_specs=pl.BlockSpec((1, tq, heads * V_HEAD), lambda bi, j, i: (bi, i, j)),
        out_shape=jax.ShapeDtypeStruct((b, s, MLA_WIDTH), BF16),
        compiler_params=_params(("parallel", "parallel", "arbitrary")),
        name="attn",
    )(qt, k, vt)


def _dft1_kernel(z_ref, f1_ref, y_ref):
    n1 = z_ref.shape[2]
    y = _dot(f1_ref[:, :n1], z_ref[0, 0]) + _dot(f1_ref[:, n1:], z_ref[0, 1])
    y_ref[0, 0] = y[:n1].astype(BF16)
    y_ref[0, 1] = y[n1:].astype(BF16)


def _dft1(z, f1, cb):
    b, _, n1, cols = z.shape
    spec = pl.BlockSpec((1, 2, n1, cb), lambda bi, j: (bi, 0, 0, j))
    return pl.pallas_call(
        _dft1_kernel,
        grid=(b, cols // cb),
        in_specs=[spec, _const_spec(f1.shape)],
        out_specs=spec,
        out_shape=jax.ShapeDtypeStruct(z.shape, BF16),
        compiler_params=_params(("parallel", "parallel")),
        name="dft1",
    )(z, f1)


def _dft2_kernel(y_ref, w_ref, wf_ref, o_ref):
    kb, n2 = y_ref.shape[2], y_ref.shape[3]
    for t in range(kb):
        xr = _dot(w_ref[t, :, :n2], y_ref[0, 0, t]) + _dot(w_ref[t, :, n2:], y_ref[0, 1, t])
        xr = xr.astype(BF16)
        for g in range(FNET_GROUPS):
            sl = slice(g * FNET_GROUP_DIM, (g + 1) * FNET_GROUP_DIM)
            o_ref[0, :, t * FNET_WIDTH + g * FNET_GROUP_DIM:
                  t * FNET_WIDTH + (g + 1) * FNET_GROUP_DIM] = _dot(xr[:, sl], wf_ref[g]).astype(BF16)


def _dft2(y, w, wf, kb):
    b, _, n1, n2, c = y.shape
    return pl.pallas_call(
        _dft2_kernel,
        grid=(b, n1 // kb),
        in_specs=[pl.BlockSpec((1, 2, kb, n2, c), lambda bi, j: (bi, 0, j, 0, 0)),
                  pl.BlockSpec((kb, n2, 2 * n2), lambda bi, j: (j, 0, 0)),
                  _const_spec(wf.shape)],
        out_specs=pl.BlockSpec((1, n2, kb * c), lambda bi, j: (bi, 0, j)),
        out_shape=jax.ShapeDtypeStruct((b, n2, n1 * c), BF16),
        compiler_params=_params(("parallel", "parallel")),
        name="dft2",
    )(y, w, wf)


def _mix_kernel(a_ref, f_ref, x_ref, woa_ref, wof_ref, gpm_ref, gpf_ref, x1_ref, h2_ref):
    mix = _dot(a_ref[0], woa_ref[...]) + _dot(f_ref[0], wof_ref[...])
    x1 = x_ref[0] + _rms(mix, gpm_ref[...])
    x1_ref[0] = x1
    h2_ref[0] = _rms(x1, gpf_ref[...]).astype(BF16)


def _mix(a, f, x, woa, wof, gpm, gpf, tm):
    b, s, _ = x.shape
    tok = lambda w: pl.BlockSpec((1, tm, w), lambda bi, i: (bi, i, 0))
    return pl.pallas_call(
        _mix_kernel,
        grid=(b, s // tm),
        in_specs=[tok(MLA_WIDTH), tok(FNET_WIDTH), tok(D_MODEL), _const_spec(woa.shape),
                  _const_spec(wof.shape), _const_spec(gpm.shape), _const_spec(gpf.shape)],
        out_specs=[tok(D_MODEL), tok(D_MODEL)],
        out_shape=[jax.ShapeDtypeStruct(x.shape, F32), jax.ShapeDtypeStruct(x.shape, BF16)],
        compiler_params=_params(("parallel", "parallel")),
        name="mix",
    )(a, f, x, woa, wof, gpm, gpf)


HALO = 16


def _ffn_kernel(h_ref, hp_ref, hn_ref, x1_ref, wg_ref, wu_ref, cw_ref, cb_ref, wd_ref, gpo_ref,
                y_ref, act_ref, *, ffc):
    i = pl.program_id(1)
    tm = h_ref.shape[1]
    h = h_ref[0]
    prev = jnp.where(i == 0, jnp.zeros_like(hp_ref[0]), hp_ref[0])
    nxt = jnp.where(i == pl.num_programs(1) - 1, jnp.zeros_like(hn_ref[0]), hn_ref[0])
    hext = jnp.concatenate([prev, h, nxt], axis=0)
    rows = tm + 2 * HALO
    for c in range(D_FF // ffc):
        sl = slice(c * ffc, (c + 1) * ffc)
        g = _dot(hext, wg_ref[:, sl])
        u = _dot(h, wu_ref[:, sl])
        g_prev = pltpu.roll(g, 1, 0)[HALO:HALO + tm]
        g_next = pltpu.roll(g, rows - 1, 0)[HALO:HALO + tm]
        gate = (cb_ref[:, sl] + g_prev * cw_ref[0:1, sl] + g[HALO:HALO + tm] * cw_ref[1:2, sl]
                + g_next * cw_ref[2:3, sl])
        inner = math.sqrt(2.0 / math.pi) * (gate + 0.044715 * (gate * gate * gate))
        act = 0.5 * gate * (1.0 + jnp.tanh(inner)) * u
        act_ref[:, sl] = act.astype(BF16)
    out = _dot(act_ref[...], wd_ref[...])
    y_ref[0] = x1_ref[0] + _rms(out, gpo_ref[...])


def _ffn(h2, x1, wg, wu, cw, cb, wd, gpo, tm, ffc):
    b, s, _ = x1.shape
    per = tm // HALO
    last = s // HALO - 1
    tok = pl.BlockSpec((1, tm, D_MODEL), lambda bi, i: (bi, i, 0))
    single = lambda shape: pl.BlockSpec(shape, lambda *_: (0,) * len(shape),
                                        pipeline_mode=pl.Buffered(1))
    return pl.pallas_call(
        functools.partial(_ffn_kernel, ffc=ffc),
        grid=(b, s // tm),
        in_specs=[tok,
                  pl.BlockSpec((1, HALO, D_MODEL), lambda bi, i: (bi, jnp.maximum(i * per - 1, 0), 0)),
                  pl.BlockSpec((1, HALO, D_MODEL), lambda bi, i: (bi, jnp.minimum((i + 1) * per, last), 0)),
                  tok, single(wg.shape), single(wu.shape), _const_spec(cw.shape),
                  _const_spec(cb.shape), single(wd.shape), _const_spec(gpo.shape)],
        out_specs=tok,
        out_shape=jax.ShapeDtypeStruct(x1.shape, F32),
        scratch_shapes=[pltpu.VMEM((tm, D_FF), BF16)],
        compiler_params=_params(("parallel", "arbitrary")),
        name="ffn",
    )(h2, h2, h2, x1, wg, wu, cw, cb, wd, gpo)


def _rope_tables(s):
    pos = jnp.arange(s, dtype=F32)
    inv_freq = ROPE_THETA ** (-jnp.arange(0, QK_ROPE, 2, dtype=F32) / QK_ROPE)
    ang = pos[:, None] * inv_freq[None, :]
    cos, sin = jnp.cos(ang), jnp.sin(ang)
    ones = jnp.ones((s, QK_NOPE), F32)
    zeros = jnp.zeros((s, QK_NOPE), F32)
    zh = jnp.zeros((s, HALF_ROPE), F32)
    tail = jnp.zeros((s, HEAD_PAD - QK_NOPE - QK_ROPE), F32)
    cos_t = jnp.concatenate([ones, cos, cos, tail], axis=1)
    sina = jnp.concatenate([zeros, zh, sin, tail], axis=1)
    sinb = jnp.concatenate([zeros, -sin, zh, tail], axis=1)
    return cos_t, sina, sinb


def _angles(num, den):
    return (2.0 * math.pi / den) * (num % den).astype(F32)


def _dft_tables(s):
    n2 = DFT_N2
    n1 = s // n2
    c = jnp.arange(FNET_GROUP_DIM, dtype=jnp.int32)
    ang = _angles(c[:, None] * c[None, :], FNET_GROUP_DIM)
    cs = jnp.concatenate([jnp.cos(ang), jnp.sin(ang)], axis=1) / math.sqrt(FNET_GROUP_DIM)
    k1 = jnp.arange(n1, dtype=jnp.int32)
    a1 = _angles(k1[:, None] * k1[None, :], n1)
    c1, s1 = jnp.cos(a1), jnp.sin(a1)
    f1 = jnp.concatenate([jnp.concatenate([c1, -s1], axis=1),
                          jnp.concatenate([s1, c1], axis=1)], axis=0) / math.sqrt(n1)
    k2 = jnp.arange(n2, dtype=jnp.int32)
    kk = k1[:, None, None] + n1 * k2[None, :, None]
    a2 = _angles(kk * k2[None, None, :], s)
    w = jnp.concatenate([jnp.cos(a2), -jnp.sin(a2)], axis=2) / math.sqrt(n2)
    return cs.astype(BF16), f1.astype(BF16), w.astype(BF16)


def _prep_weights(g_pre_mix, w_in, g_q, w_uq, g_kv, w_ukv, w_fnet, w_out, g_post_mix,
                  g_pre_ffn, w_gate, w_up, conv_w, conv_b, w_down, g_post_ffn):
    win = w_in[0]
    f_lo = Q_LORA + KV_LORA + QK_ROPE
    kr_cols = jnp.concatenate([jnp.zeros((D_MODEL, QK_NOPE), F32),
                               win[:, Q_LORA + KV_LORA:f_lo],
                               jnp.zeros((D_MODEL, HEAD_PAD - QK_NOPE - QK_ROPE), F32)], axis=1)
    win_p = jnp.concatenate([win[:, :Q_LORA + KV_LORA], win[:, f_lo:], kr_cols], axis=1)
    wuq = w_uq[0].reshape(Q_LORA, MLA_HEADS, QK_NOPE + QK_ROPE)
    wuq = jnp.pad(wuq, ((0, 0), (0, 0), (0, HEAD_PAD - QK_NOPE - QK_ROPE)))
    wuq = wuq.reshape(Q_LORA, MLA_HEADS * HEAD_PAD)
    wukv = w_ukv[0].reshape(KV_LORA, MLA_HEADS, QK_NOPE + V_HEAD)
    wuk = jnp.pad(wukv[..., :QK_NOPE], ((0, 0), (0, 0), (0, HEAD_PAD - QK_NOPE)))
    wukv = jnp.concatenate([wuk.reshape(KV_LORA, MLA_HEADS * HEAD_PAD),
                            wukv[..., QK_NOPE:].reshape(KV_LORA, MLA_WIDTH)], axis=1)
    return dict(
        gpre=g_pre_mix, win=win_p.astype(BF16), gq=g_q, wuq=wuq.astype(BF16), gkv=g_kv,
        wukv=wukv.astype(BF16), wf=w_fnet[0].astype(BF16),
        woa=w_out[0, :MLA_WIDTH].astype(BF16), wof=w_out[0, MLA_WIDTH:].astype(BF16),
        gpm=g_post_mix, gpf=g_pre_ffn, wg=w_gate[0].astype(BF16), wu=w_up[0].astype(BF16),
        cw=conv_w[0], cb=conv_b, wd=w_down[0].astype(BF16), gpo=g_post_ffn)


def _trunk(x, w):
    b, s, _ = x.shape
    n2 = DFT_N2
    n1 = s // n2
    cos, sina, sinb = _rope_tables(s)
    cs, f1, wtab = _dft_tables(s)
    qt, k, vt, z = _proj(x, w["gpre"], w["win"], w["gq"], w["wuq"], w["gkv"], w["wukv"], cs,
                         cos, sina, sinb, tm=512)
    a = _attn(qt, k, vt, tq=256, tk=512)
    y = _dft1(z.reshape(b, 2, n1, n2 * FNET_WIDTH), f1, cb=8192)
    f = _dft2(y.reshape(b, 2, n1, n2, FNET_WIDTH), wtab, w["wf"], kb=8)
    f = f.reshape(b, s, FNET_WIDTH)
    x1, h2 = _mix(a, f, x, w["woa"], w["wof"], w["gpm"], w["gpf"], tm=512)
    return _ffn(h2, x1, w["wg"], w["wu"], w["cw"], w["cb"], w["wd"], w["gpo"], tm=512, ffc=256)


def kernel(x_prompt, x_sample, g_pre_mix, w_in, g_q, w_uq, g_kv, w_ukv, w_fnet, w_out,
           g_post_mix, g_pre_ffn, w_gate, w_up, conv_w, conv_b, w_down, g_post_ffn):
    w = _prep_weights(g_pre_mix, w_in, g_q, w_uq, g_kv, w_ukv, w_fnet, w_out, g_post_mix,
                      g_pre_ffn, w_gate, w_up, conv_w, conv_b, w_down, g_post_ffn)
    return _trunk(x_prompt, w), _trunk(x_sample, w)
```

```python
import functools
import math

import jax
import jax.numpy as jnp
import numpy as np
from jax.experimental import pallas as pl
from jax.experimental.pallas import tpu as pltpu

D_MODEL = 1024
MLA_HEADS = 8
QK_NOPE = 64
QK_ROPE = 32
V_HEAD = 64
MLA_WIDTH = MLA_HEADS * V_HEAD
Q_LORA = 256
KV_LORA = 256
FNET_GROUPS = 4
FNET_WIDTH = 512
FNET_GROUP_DIM = 128
D_FF = 2816
ROPE_THETA = 10000.0
RMS_EPS = 1e-6
ATTN_SCALE = 1.0 / math.sqrt(QK_NOPE + QK_ROPE)
Q_SCALE = ATTN_SCALE * math.log2(math.e)

LANES = 128
HEAD_PAD = 128
HALF_ROPE = QK_ROPE // 2
DENOM_ROWS = 16
DFT_N2 = 128
VMEM_LIMIT = 56 * 1024 * 1024

BF16 = jnp.bfloat16
F32 = jnp.float32


def _rms(x, g):
    return x * jax.lax.rsqrt(jnp.mean(x * x, axis=-1, keepdims=True) + RMS_EPS) * g


def _dot(a, b):
    return jnp.dot(a, b, preferred_element_type=F32)


def _params(sem):
    return pltpu.CompilerParams(dimension_semantics=sem, vmem_limit_bytes=VMEM_LIMIT)


def _const_spec(shape):
    zeros = (0,) * len(shape)
    return pl.BlockSpec(shape, lambda *_: zeros)


def _proj_kernel(x_ref, gpre_ref, win_ref, gq_ref, wuq_ref, gkv_ref, wukv_ref, cs_ref,
                 cos_ref, sina_ref, sinb_ref, qt_ref, k_ref, vt_ref, z_ref):
    h = _rms(x_ref[0], gpre_ref[...]).astype(BF16)
    p = _dot(h, win_ref[...])
    cq = _rms(p[:, :Q_LORA], gq_ref[...]).astype(BF16)
    q = _dot(cq, wuq_ref[...])
    ckv = _rms(p[:, Q_LORA:Q_LORA + KV_LORA], gkv_ref[...]).astype(BF16)
    kv = _dot(ckv, wukv_ref[...])
    cos, sina, sinb = cos_ref[...], sina_ref[...], sinb_ref[...]

    def rope(t):
        return (t * cos + pltpu.roll(t, HALF_ROPE, 1) * sina
                + pltpu.roll(t, HEAD_PAD - HALF_ROPE, 1) * sinb)

    kr = rope(p[:, 2 * FNET_WIDTH:])
    for hd in range(MLA_HEADS):
        sl = slice(hd * HEAD_PAD, (hd + 1) * HEAD_PAD)
        qh = rope(q[:, sl]) * Q_SCALE
        qt_ref[0, sl, :] = qh.T.astype(BF16)
        k_ref[0, :, sl] = (kv[:, sl] + kr).astype(BF16)
    vt_ref[0] = kv[:, MLA_HEADS * HEAD_PAD:].T.astype(BF16)
    f = p[:, Q_LORA + KV_LORA:2 * FNET_WIDTH].astype(BF16)
    for g in range(FNET_GROUPS):
        sl = slice(g * FNET_GROUP_DIM, (g + 1) * FNET_GROUP_DIM)
        ab = _dot(f[:, sl], cs_ref[...])
        z_ref[0, 0, :, sl] = ab[:, :FNET_GROUP_DIM].astype(BF16)
        z_ref[0, 1, :, sl] = ab[:, FNET_GROUP_DIM:].astype(BF16)


def _proj(x, gpre, win, gq, wuq, gkv, wukv, cs, cos, sina, sinb, tm):
    b, s, _ = x.shape
    wide = MLA_HEADS * HEAD_PAD
    tok = lambda w: pl.BlockSpec((1, tm, w), lambda bi, i: (bi, i, 0))
    tab = pl.BlockSpec((tm, HEAD_PAD), lambda bi, i: (i, 0))
    return pl.pallas_call(
        _proj_kernel,
        grid=(b, s // tm),
        in_specs=[tok(D_MODEL), _const_spec(gpre.shape), _const_spec(win.shape),
                  _const_spec(gq.shape), _const_spec(wuq.shape), _const_spec(gkv.shape),
                  _const_spec(wukv.shape), _const_spec(cs.shape), tab, tab, tab],
        out_specs=[pl.BlockSpec((1, wide, tm), lambda bi, i: (bi, 0, i)),
                   tok(wide),
                   pl.BlockSpec((1, MLA_WIDTH, tm), lambda bi, i: (bi, 0, i)),
                   pl.BlockSpec((1, 2, tm, FNET_WIDTH), lambda bi, i: (bi, 0, i, 0))],
        out_shape=[jax.ShapeDtypeStruct((b, wide, s), BF16),
                   jax.ShapeDtypeStruct((b, s, wide), BF16),
                   jax.ShapeDtypeStruct((b, MLA_WIDTH, s), BF16),
                   jax.ShapeDtypeStruct((b, 2, s, FNET_WIDTH), BF16)],
        compiler_params=_params(("parallel", "parallel")),
        name="proj",
    )(x, gpre, win, gq, wuq, gkv, wukv, cs, cos, sina, sinb)


def _attn_kernel(qt_ref, k_ref, vt_ref, o_ref, st_ref, mc_ref, *, tk, heads):
    s = k_ref.shape[1]
    tq = qt_ref.shape[2]
    n = s // tk
    ones = jnp.ones((DENOM_ROWS, tk), BF16)

    def scores(c, slot):
        off = pl.multiple_of(c * tk, tk)
        for hd in range(heads):
            qt = qt_ref[0, hd * HEAD_PAD:(hd + 1) * HEAD_PAD, :]
            kc = k_ref[0, pl.ds(off, tk), hd * HEAD_PAD:(hd + 1) * HEAD_PAD]
            st = _dot(kc, qt)
            st_ref[slot, hd] = st
            mc_ref[slot, hd] = jnp.max(st, axis=0, keepdims=True)

    def update(c, slot, carry):
        off = pl.multiple_of(c * tk, tk)
        new = []
        for hd in range(heads):
            m, acc = carry[hd]
            m_new = jnp.maximum(m, mc_ref[slot, hd])
            alpha = jnp.exp2(m - m_new)
            p = jnp.exp2(st_ref[slot, hd] - m_new).astype(BF16)
            vc = vt_ref[0, hd * V_HEAD:(hd + 1) * V_HEAD, pl.ds(off, tk)]
            acc = alpha * acc + _dot(jnp.concatenate([vc, ones], axis=0), p)
            new.append((m_new, acc))
        return tuple(new)

    def body(c2, carry):
        c = 2 * c2
        scores(c + 1, 1)
        carry = update(c, 0, carry)
        scores(c + 2, 0)
        return update(c + 1, 1, carry)

    init = tuple((jnp.full((1, tq), -jnp.inf, F32), jnp.zeros((V_HEAD + DENOM_ROWS, tq), F32))
                 for _ in range(heads))
    scores(0, 0)
    carry = jax.lax.fori_loop(0, n // 2 - 1, body, init)
    scores(n - 1, 1)
    carry = update(n - 2, 0, carry)
    carry = update(n - 1, 1, carry)
    outs = [acc[:V_HEAD] / acc[V_HEAD:V_HEAD + 1] for _, acc in carry]
    o_ref[0] = jnp.concatenate(outs, axis=0).T.astype(BF16)


def _attn(qt, k, vt, tq, tk):
    b, _, s = qt.shape
    heads = LANES // V_HEAD
    groups = MLA_HEADS // heads
    assert (s // tk) % 2 == 0 and s % tk == 0
    return pl.pallas_call(
        functools.partial(_attn_kernel, tk=tk, heads=heads),
        grid=(b, groups, s // tq),
        in_specs=[pl.BlockSpec((1, heads * HEAD_PAD, tq), lambda bi, j, i: (bi, j, i)),
                  pl.BlockSpec((1, s, heads * HEAD_PAD), lambda bi, j, i: (bi, 0, j)),
                  pl.BlockSpec((1, heads * V_HEAD, s), lambda bi, j, i: (bi, j, 0))],
        out_specs=pl.BlockSpec((1, tq, heads * V_HEAD), lambda bi, j, i: (bi, i, j)),
        out_shape=jax.ShapeDtypeStruct((b, s, MLA_WIDTH), BF16),
        scratch_shapes=[pltpu.VMEM((2, heads, tk, tq), F32), pltpu.VMEM((2, heads, 1, tq), F32)],
        compiler_params=_params(("parallel", "parallel", "arbitrary")),
        name="attn",
    )(qt, k, vt)


def _dft1_kernel(z_ref, f1_ref, y_ref):
    n1 = z_ref.shape[2]
    y = _dot(f1_ref[:, :n1], z_ref[0, 0]) + _dot(f1_ref[:, n1:], z_ref[0, 1])
    y_ref[0, 0] = y[:n1].astype(BF16)
    y_ref[0, 1] = y[n1:].astype(BF16)


def _dft1(z, f1, cb):
    b, _, n1, cols = z.shape
    spec = pl.BlockSpec((1, 2, n1, cb), lambda bi, j: (bi, 0, 0, j))
    return pl.pallas_call(
        _dft1_kernel,
        grid=(b, cols // cb),
        in_specs=[spec, _const_spec(f1.shape)],
        out_specs=spec,
        out_shape=jax.ShapeDtypeStruct(z.shape, BF16),
        compiler_params=_params(("parallel", "parallel")),
        name="dft1",
    )(z, f1)


def _dft2_kernel(y_ref, w_ref, wf_ref, o_ref):
    kb, n2 = y_ref.shape[2], y_ref.shape[3]
    for t in range(kb):
        xr = _dot(w_ref[t, :, :n2], y_ref[0, 0, t]) + _dot(w_ref[t, :, n2:], y_ref[0, 1, t])
        xr = xr.astype(BF16)
        for g in range(FNET_GROUPS):
            sl = slice(g * FNET_GROUP_DIM, (g + 1) * FNET_GROUP_DIM)
            o_ref[0, :, t * FNET_WIDTH + g * FNET_GROUP_DIM:
                  t * FNET_WIDTH + (g + 1) * FNET_GROUP_DIM] = _dot(xr[:, sl], wf_ref[g]).astype(BF16)


def _dft2(y, w, wf, kb):
    b, _, n1, n2, c = y.shape
    return pl.pallas_call(
        _dft2_kernel,
        grid=(b, n1 // kb),
        in_specs=[pl.BlockSpec((1, 2, kb, n2, c), lambda bi, j: (bi, 0, j, 0, 0)),
                  pl.BlockSpec((kb, n2, 2 * n2), lambda bi, j: (j, 0, 0)),
                  _const_spec(wf.shape)],
        out_specs=pl.BlockSpec((1, n2, kb * c), lambda bi, j: (bi, 0, j)),
        out_shape=jax.ShapeDtypeStruct((b, n2, n1 * c), BF16),
        compiler_params=_params(("parallel", "parallel")),
        name="dft2",
    )(y, w, wf)


def _mix_kernel(a_ref, f_ref, x_ref, woa_ref, wof_ref, gpm_ref, gpf_ref, x1_ref, h2_ref):
    mix = _dot(a_ref[0], woa_ref[...]) + _dot(f_ref[0], wof_ref[...])
    x1 = x_ref[0] + _rms(mix, gpm_ref[...])
    x1_ref[0] = x1
    h2_ref[0] = _rms(x1, gpf_ref[...]).astype(BF16)


def _mix(a, f, x, woa, wof, gpm, gpf, tm):
    b, s, _ = x.shape
    tok = lambda w: pl.BlockSpec((1, tm, w), lambda bi, i: (bi, i, 0))
    return pl.pallas_call(
        _mix_kernel,
        grid=(b, s // tm),
        in_specs=[tok(MLA_WIDTH), tok(FNET_WIDTH), tok(D_MODEL), _const_spec(woa.shape),
                  _const_spec(wof.shape), _const_spec(gpm.shape), _const_spec(gpf.shape)],
        out_specs=[tok(D_MODEL), tok(D_MODEL)],
        out_shape=[jax.ShapeDtypeStruct(x.shape, F32), jax.ShapeDtypeStruct(x.shape, BF16)],
        compiler_params=_params(("parallel", "parallel")),
        name="mix",
    )(a, f, x, woa, wof, gpm, gpf)


HALO = 16


def _ffn_kernel(h_ref, hp_ref, hn_ref, x1_ref, wg_ref, wu_ref, cw_ref, cb_ref, wd_ref, gpo_ref,
                y_ref, act_ref, *, ffc):
    i = pl.program_id(1)
    tm = h_ref.shape[1]
    h = h_ref[0]
    prev = jnp.where(i == 0, jnp.zeros_like(hp_ref[0]), hp_ref[0])
    nxt = jnp.where(i == pl.num_programs(1) - 1, jnp.zeros_like(hn_ref[0]), hn_ref[0])
    hext = jnp.concatenate([prev, h, nxt], axis=0)
    rows = tm + 2 * HALO
    for c in range(D_FF // ffc):
        sl = slice(c * ffc, (c + 1) * ffc)
        g = _dot(hext, wg_ref[:, sl])
        u = _dot(h, wu_ref[:, sl])
        g_prev = pltpu.roll(g, 1, 0)[HALO:HALO + tm]
        g_next = pltpu.roll(g, rows - 1, 0)[HALO:HALO + tm]
        gate = (cb_ref[:, sl] + g_prev * cw_ref[0:1, sl] + g[HALO:HALO + tm] * cw_ref[1:2, sl]
                + g_next * cw_ref[2:3, sl])
        inner = math.sqrt(2.0 / math.pi) * (gate + 0.044715 * (gate * gate * gate))
        act = 0.5 * gate * (1.0 + jnp.tanh(inner)) * u
        act_ref[:, sl] = act.astype(BF16)
    out = _dot(act_ref[...], wd_ref[...])
    y_ref[0] = x1_ref[0] + _rms(out, gpo_ref[...])


def _ffn(h2, x1, wg, wu, cw, cb, wd, gpo, tm, ffc):
    b, s, _ = x1.shape
    per = tm // HALO
    last = s // HALO - 1
    tok = pl.BlockSpec((1, tm, D_MODEL), lambda bi, i: (bi, i, 0))
    single = lambda shape: pl.BlockSpec(shape, lambda *_: (0,) * len(shape),
                                        pipeline_mode=pl.Buffered(1))
    return pl.pallas_call(
        functools.partial(_ffn_kernel, ffc=ffc),
        grid=(b, s // tm),
        in_specs=[tok,
                  pl.BlockSpec((1, HALO, D_MODEL), lambda bi, i: (bi, jnp.maximum(i * per - 1, 0), 0)),
                  pl.BlockSpec((1, HALO, D_MODEL), lambda bi, i: (bi, jnp.minimum((i + 1) * per, last), 0)),
                  tok, single(wg.shape), single(wu.shape), _const_spec(cw.shape),
                  _const_spec(cb.shape), single(wd.shape), _const_spec(gpo.shape)],
        out_specs=tok,
        out_shape=jax.ShapeDtypeStruct(x1.shape, F32),
        scratch_shapes=[pltpu.VMEM((tm, D_FF), BF16)],
        compiler_params=_params(("parallel", "arbitrary")),
        name="ffn",
    )(h2, h2, h2, x1, wg, wu, cw, cb, wd, gpo)


def _rope_tables(s):
    pos = jnp.arange(s, dtype=F32)
    inv_freq = ROPE_THETA ** (-jnp.arange(0, QK_ROPE, 2, dtype=F32) / QK_ROPE)
    ang = pos[:, None] * inv_freq[None, :]
    cos, sin = jnp.cos(ang), jnp.sin(ang)
    ones = jnp.ones((s, QK_NOPE), F32)
    zeros = jnp.zeros((s, QK_NOPE), F32)
    zh = jnp.zeros((s, HALF_ROPE), F32)
    tail = jnp.zeros((s, HEAD_PAD - QK_NOPE - QK_ROPE), F32)
    cos_t = jnp.concatenate([ones, cos, cos, tail], axis=1)
    sina = jnp.concatenate([zeros, zh, sin, tail], axis=1)
    sinb = jnp.concatenate([zeros, -sin, zh, tail], axis=1)
    return cos_t, sina, sinb


def _angles(num, den):
    return (2.0 * math.pi / den) * (num % den).astype(F32)


def _dft_tables(s):
    n2 = DFT_N2
    n1 = s // n2
    c = jnp.arange(FNET_GROUP_DIM, dtype=jnp.int32)
    ang = _angles(c[:, None] * c[None, :], FNET_GROUP_DIM)
    cs = jnp.concatenate([jnp.cos(ang), jnp.sin(ang)], axis=1) / math.sqrt(FNET_GROUP_DIM)
    k1 = jnp.arange(n1, dtype=jnp.int32)
    a1 = _angles(k1[:, None] * k1[None, :], n1)
    c1, s1 = jnp.cos(a1), jnp.sin(a1)
    f1 = jnp.concatenate([jnp.concatenate([c1, -s1], axis=1),
                          jnp.concatenate([s1, c1], axis=1)], axis=0) / math.sqrt(n1)
    k2 = jnp.arange(n2, dtype=jnp.int32)
    kk = k1[:, None, None] + n1 * k2[None, :, None]
    a2 = _angles(kk * k2[None, None, :], s)
    w = jnp.concatenate([jnp.cos(a2), -jnp.sin(a2)], axis=2) / math.sqrt(n2)
    return cs.astype(BF16), f1.astype(BF16), w.astype(BF16)


def _prep_weights(g_pre_mix, w_in, g_q, w_uq, g_kv, w_ukv, w_fnet, w_out, g_post_mix,
                  g_pre_ffn, w_gate, w_up, conv_w, conv_b, w_down, g_post_ffn):
    win = w_in[0]
    f_lo = Q_LORA + KV_LORA + QK_ROPE
    kr_cols = jnp.concatenate([jnp.zeros((D_MODEL, QK_NOPE), F32),
                               win[:, Q_LORA + KV_LORA:f_lo],
                               jnp.zeros((D_MODEL, HEAD_PAD - QK_NOPE - QK_ROPE), F32)], axis=1)
    win_p = jnp.concatenate([win[:, :Q_LORA + KV_LORA], win[:, f_lo:], kr_cols], axis=1)
    wuq = w_uq[0].reshape(Q_LORA, MLA_HEADS, QK_NOPE + QK_ROPE)
    wuq = jnp.pad(wuq, ((0, 0), (0, 0), (0, HEAD_PAD - QK_NOPE - QK_ROPE)))
    wuq = wuq.reshape(Q_LORA, MLA_HEADS * HEAD_PAD)
    wukv = w_ukv[0].reshape(KV_LORA, MLA_HEADS, QK_NOPE + V_HEAD)
    wuk = jnp.pad(wukv[..., :QK_NOPE], ((0, 0), (0, 0), (0, HEAD_PAD - QK_NOPE)))
    wukv = jnp.concatenate([wuk.reshape(KV_LORA, MLA_HEADS * HEAD_PAD),
                            wukv[..., QK_NOPE:].reshape(KV_LORA, MLA_WIDTH)], axis=1)
    return dict(
        gpre=g_pre_mix, win=win_p.astype(BF16), gq=g_q, wuq=wuq.astype(BF16), gkv=g_kv,
        wukv=wukv.astype(BF16), wf=w_fnet[0].astype(BF16),
        woa=w_out[0, :MLA_WIDTH].astype(BF16), wof=w_out[0, MLA_WIDTH:].astype(BF16),
        gpm=g_post_mix, gpf=g_pre_ffn, wg=w_gate[0].astype(BF16), wu=w_up[0].astype(BF16),
        cw=conv_w[0], cb=conv_b, wd=w_down[0].astype(BF16), gpo=g_post_ffn)


def _trunk(x, w):
    b, s, _ = x.shape
    n2 = DFT_N2
    n1 = s // n2
    cos, sina, sinb = _rope_tables(s)
    cs, f1, wtab = _dft_tables(s)
    qt, k, vt, z = _proj(x, w["gpre"], w["win"], w["gq"], w["wuq"], w["gkv"], w["wukv"], cs,
                         cos, sina, sinb, tm=512)
    a = _attn(qt, k, vt, tq=256, tk=1024)
    y = _dft1(z.reshape(b, 2, n1, n2 * FNET_WIDTH), f1, cb=8192)
    f = _dft2(y.reshape(b, 2, n1, n2, FNET_WIDTH), wtab, w["wf"], kb=8)
    f = f.reshape(b, s, FNET_WIDTH)
    x1, h2 = _mix(a, f, x, w["woa"], w["wof"], w["gpm"], w["gpf"], tm=512)
    return _ffn(h2, x1, w["wg"], w["wu"], w["cw"], w["cb"], w["wd"], w["gpo"], tm=512, ffc=256)


def kernel(x_prompt, x_sample, g_pre_mix, w_in, g_q, w_uq, g_kv, w_ukv, w_fnet, w_out,
           g_post_mix, g_pre_ffn, w_gate, w_up, conv_w, conv_b, w_down, g_post_ffn):
    w = _prep_weights(g_pre_mix, w_in, g_q, w_uq, g_kv, w_ukv, w_fnet, w_out, g_post_mix,
                      g_pre_ffn, w_gate, w_up, conv_w, conv_b, w_down, g_post_ffn)
    return _trunk(x_prompt, w), _trunk(x_sample, w)
```

```python
import functools
import math

import jax
import jax.numpy as jnp
import numpy as np
from jax.experimental import pallas as pl
from jax.experimental.pallas import tpu as pltpu

D_MODEL = 1024
MLA_HEADS = 8
QK_NOPE = 64
QK_ROPE = 32
V_HEAD = 64
MLA_WIDTH = MLA_HEADS * V_HEAD
Q_LORA = 256
KV_LORA = 256
FNET_GROUPS = 4
FNET_WIDTH = 512
FNET_GROUP_DIM = 128
D_FF = 2816
ROPE_THETA = 10000.0
RMS_EPS = 1e-6
ATTN_SCALE = 1.0 / math.sqrt(QK_NOPE + QK_ROPE)
Q_SCALE = ATTN_SCALE * math.log2(math.e)

LANES = 128
HEAD_PAD = 128
HALF_ROPE = QK_ROPE // 2
DENOM_ROWS = 16
DFT_N2 = 128
VMEM_LIMIT = 56 * 1024 * 1024

BF16 = jnp.bfloat16
F32 = jnp.float32


def _rms(x, g):
    return x * jax.lax.rsqrt(jnp.mean(x * x, axis=-1, keepdims=True) + RMS_EPS) * g


def _dot(a, b):
    return jnp.dot(a, b, preferred_element_type=F32)


def _params(sem):
    return pltpu.CompilerParams(dimension_semantics=sem, vmem_limit_bytes=VMEM_LIMIT)


def _const_spec(shape):
    zeros = (0,) * len(shape)
    return pl.BlockSpec(shape, lambda *_: zeros)


def _proj_kernel(x_ref, gpre_ref, win_ref, gq_ref, wuq_ref, gkv_ref, wukv_ref, cs_ref,
                 cos_ref, sina_ref, sinb_ref, qt_ref, k_ref, vt_ref, z_ref):
    h = _rms(x_ref[0], gpre_ref[...]).astype(BF16)
    p = _dot(h, win_ref[...])
    cq = _rms(p[:, :Q_LORA], gq_ref[...]).astype(BF16)
    q = _dot(cq, wuq_ref[...])
    ckv = _rms(p[:, Q_LORA:Q_LORA + KV_LORA], gkv_ref[...]).astype(BF16)
    kv = _dot(ckv, wukv_ref[...])
    cos, sina, sinb = cos_ref[...], sina_ref[...], sinb_ref[...]

    def rope(t):
        return (t * cos + pltpu.roll(t, HALF_ROPE, 1) * sina
                + pltpu.roll(t, HEAD_PAD - HALF_ROPE, 1) * sinb)

    kr = rope(p[:, 2 * FNET_WIDTH:])
    for hd in range(MLA_HEADS):
        sl = slice(hd * HEAD_PAD, (hd + 1) * HEAD_PAD)
        qh = rope(q[:, sl]) * Q_SCALE
        qt_ref[0, sl, :] = qh.T.astype(BF16)
        k_ref[0, :, sl] = (kv[:, sl] + kr).astype(BF16)
    vt_ref[0] = kv[:, MLA_HEADS * HEAD_PAD:].T.astype(BF16)
    f = p[:, Q_LORA + KV_LORA:2 * FNET_WIDTH].astype(BF16)
    cs = cs_ref[...].astype(BF16)
    for g in range(FNET_GROUPS):
        sl = slice(g * FNET_GROUP_DIM, (g + 1) * FNET_GROUP_DIM)
        ab = _dot(f[:, sl], cs)
        z_ref[0, 0, :, sl] = ab[:, :FNET_GROUP_DIM].astype(BF16)
        z_ref[0, 1, :, sl] = ab[:, FNET_GROUP_DIM:].astype(BF16)


def _proj(x, gpre, win, gq, wuq, gkv, wukv, cs, cos, sina, sinb, tm):
    b, s, _ = x.shape
    wide = MLA_HEADS * HEAD_PAD
    tok = lambda w: pl.BlockSpec((1, tm, w), lambda bi, i: (bi, i, 0))
    tab = pl.BlockSpec((tm, HEAD_PAD), lambda bi, i: (i, 0))
    return pl.pallas_call(
        _proj_kernel,
        grid=(b, s // tm),
        in_specs=[tok(D_MODEL), _const_spec(gpre.shape), _const_spec(win.shape),
                  _const_spec(gq.shape), _const_spec(wuq.shape), _const_spec(gkv.shape),
                  _const_spec(wukv.shape), _const_spec(cs.shape), tab, tab, tab],
        out_specs=[pl.BlockSpec((1, wide, tm), lambda bi, i: (bi, 0, i)),
                   tok(wide),
                   pl.BlockSpec((1, MLA_WIDTH, tm), lambda bi, i: (bi, 0, i)),
                   pl.BlockSpec((1, 2, tm, FNET_WIDTH), lambda bi, i: (bi, 0, i, 0))],
        out_shape=[jax.ShapeDtypeStruct((b, wide, s), BF16),
                   jax.ShapeDtypeStruct((b, s, wide), BF16),
                   jax.ShapeDtypeStruct((b, MLA_WIDTH, s), BF16),
                   jax.ShapeDtypeStruct((b, 2, s, FNET_WIDTH), BF16)],
        compiler_params=_params(("parallel", "parallel")),
        name="proj",
    )(x, gpre, win, gq, wuq, gkv, wukv, cs, cos, sina, sinb)


def _attn_kernel(qt_ref, k_ref, vt_ref, o_ref, st_all, mc_all, *, tq, tk, heads):
    st_ref, mc_ref = (st_all.at[0], st_all.at[1]), (mc_all.at[0], mc_all.at[1])
    s = k_ref.shape[1]
    n = s // tk
    i = pl.program_id(2)
    i_next = jnp.minimum(i + 1, s // tq - 1)
    ones = jnp.ones((DENOM_ROWS, tk), BF16)

    def scores(qi, c, slot):
        q_off = pl.multiple_of(qi * tq, tq)
        k_off = pl.multiple_of(c * tk, tk)
        for hd in range(heads):
            qt = qt_ref[0, hd * HEAD_PAD:(hd + 1) * HEAD_PAD, pl.ds(q_off, tq)]
            kc = k_ref[0, pl.ds(k_off, tk), hd * HEAD_PAD:(hd + 1) * HEAD_PAD]
            st = _dot(kc, qt)
            st_ref[slot][hd] = st
            mc_ref[slot][hd] = jnp.max(st, axis=0, keepdims=True)

    def update(c, slot, carry):
        k_off = pl.multiple_of(c * tk, tk)
        new = []
        for hd in range(heads):
            m, acc = carry[hd]
            m_new = jnp.maximum(m, mc_ref[slot][hd])
            alpha = jnp.exp2(m - m_new)
            p = jnp.exp2(st_ref[slot][hd] - m_new).astype(BF16)
            vc = vt_ref[0, hd * V_HEAD:(hd + 1) * V_HEAD, pl.ds(k_off, tk)]
            acc = alpha * acc + _dot(jnp.concatenate([vc, ones], axis=0), p)
            new.append((m_new, acc))
        return tuple(new)

    @pl.when(i == 0)
    def _():
        scores(0, 0, 0)

    def step(c, slot, carry):
        wraps = c + 1 == n
        scores(jnp.where(wraps, i_next, i), jnp.where(wraps, 0, c + 1), 1 - slot)
        return update(c, slot, carry)

    def body(c2, carry):
        return step(2 * c2 + 1, 1, step(2 * c2, 0, carry))

    init = tuple((jnp.full((1, tq), -jnp.inf, F32), jnp.zeros((V_HEAD + DENOM_ROWS, tq), F32))
                 for _ in range(heads))
    carry = jax.lax.fori_loop(0, n // 2, body, init)
    outs = [acc[:V_HEAD] / acc[V_HEAD:V_HEAD + 1] for _, acc in carry]
    o_ref[0] = jnp.concatenate(outs, axis=0).T.astype(BF16)


def _attn(qt, k, vt, tq, tk):
    b, _, s = qt.shape
    heads = LANES // V_HEAD
    groups = MLA_HEADS // heads
    assert s % tq == 0 and s % (2 * tk) == 0
    return pl.pallas_call(
        functools.partial(_attn_kernel, tq=tq, tk=tk, heads=heads),
        grid=(b, groups, s // tq),
        in_specs=[pl.BlockSpec((1, heads * HEAD_PAD, s), lambda bi, j, i: (bi, j, 0)),
                  pl.BlockSpec((1, s, heads * HEAD_PAD), lambda bi, j, i: (bi, 0, j)),
                  pl.BlockSpec((1, heads * V_HEAD, s), lambda bi, j, i: (bi, j, 0))],
        out_specs=pl.BlockSpec((1, tq, heads * V_HEAD), lambda bi, j, i: (bi, i, j)),
        out_shape=jax.ShapeDtypeStruct((b, s, MLA_WIDTH), BF16),
        scratch_shapes=[pltpu.VMEM((2, heads, tk, tq), F32), pltpu.VMEM((2, heads, 1, tq), F32)],
        compiler_params=_params(("parallel", "parallel", "arbitrary")),
        name="attn",
    )(qt, k, vt)


def _dft1_kernel(z_ref, f1_ref, y_ref):
    n1 = z_ref.shape[2]
    f1 = f1_ref[...].astype(BF16)
    y = _dot(f1[:, :n1], z_ref[0, 0]) + _dot(f1[:, n1:], z_ref[0, 1])
    y_ref[0, 0] = y[:n1].astype(BF16)
    y_ref[0, 1] = y[n1:].astype(BF16)


def _dft1(z, f1, cb):
    b, _, n1, cols = z.shape
    spec = pl.BlockSpec((1, 2, n1, cb), lambda bi, j: (bi, 0, 0, j))
    return pl.pallas_call(
        _dft1_kernel,
        grid=(b, cols // cb),
        in_specs=[spec, _const_spec(f1.shape)],
        out_specs=spec,
        out_shape=jax.ShapeDtypeStruct(z.shape, BF16),
        compiler_params=_params(("parallel", "parallel")),
        name="dft1",
    )(z, f1)


def _dft2_kernel(y_ref, w_ref, wf_ref, o_ref):
    kb, n2 = y_ref.shape[2], y_ref.shape[3]
    for t in range(kb):
        w = w_ref[t].astype(BF16)
        xr = _dot(w[:, :n2], y_ref[0, 0, t]) + _dot(w[:, n2:], y_ref[0, 1, t])
        xr = xr.astype(BF16)
        for g in range(FNET_GROUPS):
            sl = slice(g * FNET_GROUP_DIM, (g + 1) * FNET_GROUP_DIM)
            o_ref[0, :, t * FNET_WIDTH + g * FNET_GROUP_DIM:
                  t * FNET_WIDTH + (g + 1) * FNET_GROUP_DIM] = _dot(xr[:, sl], wf_ref[g]).astype(BF16)


def _dft2(y, w, wf, kb):
    b, _, n1, n2, c = y.shape
    return pl.pallas_call(
        _dft2_kernel,
        grid=(b, n1 // kb),
        in_specs=[pl.BlockSpec((1, 2, kb, n2, c), lambda bi, j: (bi, 0, j, 0, 0)),
                  pl.BlockSpec((kb, n2, 2 * n2), lambda bi, j: (j, 0, 0)),
                  _const_spec(wf.shape)],
        out_specs=pl.BlockSpec((1, n2, kb * c), lambda bi, j: (bi, 0, j)),
        out_shape=jax.ShapeDtypeStruct((b, n2, n1 * c), BF16),
        compiler_params=_params(("parallel", "parallel")),
        name="dft2",
    )(y, w, wf)


HALO = 16


def _tail_kernel(a_ref, ap_ref, an_ref, f_ref, fp_ref, fn_ref, x_ref, xp_ref, xn_ref,
                 woa_ref, wof_ref, gpm_ref, gpf_ref, wg_ref, wu_ref, cw_ref, cb_ref, wd_ref,
                 gpo_ref, y_ref, act_ref, *, ffc):
    i = pl.program_id(1)
    tm = x_ref.shape[1]
    rows = tm + 2 * HALO
    ext = lambda p, c, n: jnp.concatenate([p[0], c[0], n[0]], axis=0)
    mix = _dot(ext(ap_ref, a_ref, an_ref), woa_ref[...]) + _dot(ext(fp_ref, f_ref, fn_ref), wof_ref[...])
    x1 = ext(xp_ref, x_ref, xn_ref) + _rms(mix, gpm_ref[...])
    row = jax.lax.broadcasted_iota(jnp.int32, (rows, 1), 0)
    inside = ((row >= HALO) | (i > 0)) & ((row < HALO + tm) | (i < pl.num_programs(1) - 1))
    hext = jnp.where(inside, _rms(x1, gpf_ref[...]), 0.0).astype(BF16)
    h = hext[HALO:HALO + tm]
    for c in range(D_FF // ffc):
        sl = slice(c * ffc, (c + 1) * ffc)
        g = _dot(hext, wg_ref[:, sl])
        u = _dot(h, wu_ref[:, sl])
        g_prev = pltpu.roll(g, 1, 0)[HALO:HALO + tm]
        g_next = pltpu.roll(g, rows - 1, 0)[HALO:HALO + tm]
        gate = (cb_ref[:, sl] + g_prev * cw_ref[0:1, sl] + g[HALO:HALO + tm] * cw_ref[1:2, sl]
                + g_next * cw_ref[2:3, sl])
        inner = math.sqrt(2.0 / math.pi) * (gate + 0.044715 * (gate * gate * gate))
        act = 0.5 * gate * (1.0 + jnp.tanh(inner)) * u
        act_ref[:, sl] = act.astype(BF16)
    out = _dot(act_ref[...], wd_ref[...])
    y_ref[0] = x1[HALO:HALO + tm] + _rms(out, gpo_ref[...])


def _tail(a, f, x, woa, wof, gpm, gpf, wg, wu, cw, cb, wd, gpo, tm, ffc):
    b, s, _ = x.shape
    per = tm // HALO
    last = s // HALO - 1
    single = lambda shape: pl.BlockSpec(shape, lambda *_: (0,) * len(shape),
                                        pipeline_mode=pl.Buffered(1))

    def with_halo(width):
        return [pl.BlockSpec((1, tm, width), lambda bi, i: (bi, i, 0)),
                pl.BlockSpec((1, HALO, width), lambda bi, i: (bi, jnp.maximum(i * per - 1, 0), 0)),
                pl.BlockSpec((1, HALO, width), lambda bi, i: (bi, jnp.minimum((i + 1) * per, last), 0))]

    return pl.pallas_call(
        functools.partial(_tail_kernel, ffc=ffc),
        grid=(b, s // tm),
        in_specs=with_halo(MLA_WIDTH) + with_halo(FNET_WIDTH) + with_halo(D_MODEL) + [
            single(woa.shape), single(wof.shape), _const_spec(gpm.shape), _const_spec(gpf.shape),
            single(wg.shape), single(wu.shape), _const_spec(cw.shape), _const_spec(cb.shape),
            single(wd.shape), _const_spec(gpo.shape)],
        out_specs=pl.BlockSpec((1, tm, D_MODEL), lambda bi, i: (bi, i, 0)),
        out_shape=jax.ShapeDtypeStruct(x.shape, F32),
        scratch_shapes=[pltpu.VMEM((tm, D_FF), BF16)],
        compiler_params=_params(("parallel", "arbitrary")),
        name="tail",
    )(a, a, a, f, f, f, x, x, x, woa, wof, gpm, gpf, wg, wu, cw, cb, wd, gpo)


def _rope_tables(s):
    ang = np.arange(s, dtype=np.float64)[:, None] * (
        ROPE_THETA ** (-np.arange(0, QK_ROPE, 2, dtype=np.float64) / QK_ROPE))[None, :]
    cos, sin = np.cos(ang), np.sin(ang)
    zeros = np.zeros((s, QK_NOPE))
    zh = np.zeros((s, HALF_ROPE))
    tail = np.zeros((s, HEAD_PAD - QK_NOPE - QK_ROPE))
    cos_t = np.concatenate([np.ones((s, QK_NOPE)), cos, cos, tail], axis=1)
    sina = np.concatenate([zeros, zh, sin, tail], axis=1)
    sinb = np.concatenate([zeros, -sin, zh, tail], axis=1)
    return tuple(jnp.asarray(t, F32) for t in (cos_t, sina, sinb))


def _angles(num, den):
    return (2.0 * np.pi / den) * (num % den).astype(np.float64)


def _dft_tables(s):
    n2 = DFT_N2
    n1 = s // n2
    c = np.arange(FNET_GROUP_DIM, dtype=np.int64)
    ang = _angles(c[:, None] * c[None, :], FNET_GROUP_DIM)
    cs = np.concatenate([np.cos(ang), np.sin(ang)], axis=1) / math.sqrt(FNET_GROUP_DIM)
    k1 = np.arange(n1, dtype=np.int64)
    a1 = _angles(k1[:, None] * k1[None, :], n1)
    c1, s1 = np.cos(a1), np.sin(a1)
    f1 = np.concatenate([np.concatenate([c1, -s1], axis=1),
                         np.concatenate([s1, c1], axis=1)], axis=0) / math.sqrt(n1)
    k2 = np.arange(n2, dtype=np.int64)
    kk = k1[:, None, None] + n1 * k2[None, :, None]
    a2 = _angles(kk * k2[None, None, :], s)
    w = np.concatenate([np.cos(a2), -np.sin(a2)], axis=2) / math.sqrt(n2)
    return tuple(jnp.asarray(t, F32) for t in (cs, f1, w))


def _prep_weights(g_pre_mix, w_in, g_q, w_uq, g_kv, w_ukv, w_fnet, w_out, g_post_mix,
                  g_pre_ffn, w_gate, w_up, conv_w, conv_b, w_down, g_post_ffn):
    win = w_in[0]
    f_lo = Q_LORA + KV_LORA + QK_ROPE
    kr_cols = jnp.concatenate([jnp.zeros((D_MODEL, QK_NOPE), F32),
                               win[:, Q_LORA + KV_LORA:f_lo],
                               jnp.zeros((D_MODEL, HEAD_PAD - QK_NOPE - QK_ROPE), F32)], axis=1)
    win_p = jnp.concatenate([win[:, :Q_LORA + KV_LORA], win[:, f_lo:], kr_cols], axis=1)
    wuq = w_uq[0].reshape(Q_LORA, MLA_HEADS, QK_NOPE + QK_ROPE)
    wuq = jnp.pad(wuq, ((0, 0), (0, 0), (0, HEAD_PAD - QK_NOPE - QK_ROPE)))
    wuq = wuq.reshape(Q_LORA, MLA_HEADS * HEAD_PAD)
    wukv = w_ukv[0].reshape(KV_LORA, MLA_HEADS, QK_NOPE + V_HEAD)
    wuk = jnp.pad(wukv[..., :QK_NOPE], ((0, 0), (0, 0), (0, HEAD_PAD - QK_NOPE)))
    wukv = jnp.concatenate([wuk.reshape(KV_LORA, MLA_HEADS * HEAD_PAD),
                            wukv[..., QK_NOPE:].reshape(KV_LORA, MLA_WIDTH)], axis=1)
    return dict(
        gpre=g_pre_mix, win=win_p.astype(BF16), gq=g_q, wuq=wuq.astype(BF16), gkv=g_kv,
        wukv=wukv.astype(BF16), wf=w_fnet[0].astype(BF16),
        woa=w_out[0, :MLA_WIDTH].astype(BF16), wof=w_out[0, MLA_WIDTH:].astype(BF16),
        gpm=g_post_mix, gpf=g_pre_ffn, wg=w_gate[0].astype(BF16), wu=w_up[0].astype(BF16),
        cw=conv_w[0], cb=conv_b, wd=w_down[0].astype(BF16), gpo=g_post_ffn)


def _trunk(x, w):
    b, s, _ = x.shape
    n2 = DFT_N2
    n1 = s // n2
    cos, sina, sinb = _rope_tables(s)
    cs, f1, wtab = _dft_tables(s)
    qt, k, vt, z = _proj(x, w["gpre"], w["win"], w["gq"], w["wuq"], w["gkv"], w["wukv"], cs,
                         cos, sina, sinb, tm=512)
    a = _attn(qt, k, vt, tq=256, tk=2048)
    y = _dft1(z.reshape(b, 2, n1, n2 * FNET_WIDTH), f1, cb=8192)
    f = _dft2(y.reshape(b, 2, n1, n2, FNET_WIDTH), wtab, w["wf"], kb=8)
    f = f.reshape(b, s, FNET_WIDTH)
    return _tail(a, f, x, w["woa"], w["wof"], w["gpm"], w["gpf"], w["wg"], w["wu"], w["cw"],
                 w["cb"], w["wd"], w["gpo"], tm=512, ffc=256)


def kernel(x_prompt, x_sample, g_pre_mix, w_in, g_q, w_uq, g_kv, w_ukv, w_fnet, w_out,
           g_post_mix, g_pre_ffn, w_gate, w_up, conv_w, conv_b, w_down, g_post_ffn):
    w = _prep_weights(g_pre_mix, w_in, g_q, w_uq, g_kv, w_ukv, w_fnet, w_out, g_post_mix,
                      g_pre_ffn, w_gate, w_up, conv_w, conv_b, w_down, g_post_ffn)
    return _trunk(x_prompt, w), _trunk(x_sample, w)
```

```python
import functools
import math

import jax
import jax.numpy as jnp
import numpy as np
from jax.experimental import pallas as pl
from jax.experimental.pallas import tpu as pltpu

D_MODEL = 1024
MLA_HEADS = 8
QK_NOPE = 64
QK_ROPE = 32
V_HEAD = 64
MLA_WIDTH = MLA_HEADS * V_HEAD
Q_LORA = 256
KV_LORA = 256
FNET_GROUPS = 4
FNET_WIDTH = 512
FNET_GROUP_DIM = 128
D_FF = 2816
ROPE_THETA = 10000.0
RMS_EPS = 1e-6
ATTN_SCALE = 1.0 / math.sqrt(QK_NOPE + QK_ROPE)
Q_SCALE = ATTN_SCALE * math.log2(math.e)

LANES = 128
HEAD_PAD = 128
HALF_ROPE = QK_ROPE // 2
DENOM_ROWS = 16
SCORE_PAD = 8
DFT_N2 = 128
VMEM_LIMIT = 56 * 1024 * 1024

BF16 = jnp.bfloat16
F32 = jnp.float32


def _rms(x, g):
    return x * jax.lax.rsqrt(jnp.mean(x * x, axis=-1, keepdims=True) + RMS_EPS) * g


def _dot(a, b):
    return jnp.dot(a, b, preferred_element_type=F32)


def _params(sem):
    return pltpu.CompilerParams(dimension_semantics=sem, vmem_limit_bytes=VMEM_LIMIT)


def _const_spec(shape):
    zeros = (0,) * len(shape)
    return pl.BlockSpec(shape, lambda *_: zeros)


def _proj_kernel(x_ref, gpre_ref, win_ref, gq_ref, wuq_ref, gkv_ref, wukv_ref, cs_ref,
                 cos_ref, sina_ref, sinb_ref, qt_ref, k_ref, vt_ref, z_ref):
    h = _rms(x_ref[0], gpre_ref[...]).astype(BF16)
    p = _dot(h, win_ref[...])
    cq = _rms(p[:, :Q_LORA], gq_ref[...]).astype(BF16)
    q = _dot(cq, wuq_ref[...])
    ckv = _rms(p[:, Q_LORA:Q_LORA + KV_LORA], gkv_ref[...]).astype(BF16)
    kv = _dot(ckv, wukv_ref[...])
    cos, sina, sinb = cos_ref[...], sina_ref[...], sinb_ref[...]

    def rope(t):
        return (t * cos + pltpu.roll(t, HALF_ROPE, 1) * sina
                + pltpu.roll(t, HEAD_PAD - HALF_ROPE, 1) * sinb)

    kr = rope(p[:, 2 * FNET_WIDTH:])
    for hd in range(MLA_HEADS):
        sl = slice(hd * HEAD_PAD, (hd + 1) * HEAD_PAD)
        qh = rope(q[:, sl]) * Q_SCALE
        qt_ref[0, sl, :] = qh.T.astype(BF16)
        k_ref[0, :, sl] = (kv[:, sl] + kr).astype(BF16)
    vt_ref[0] = kv[:, MLA_HEADS * HEAD_PAD:].T.astype(BF16)
    f = p[:, Q_LORA + KV_LORA:2 * FNET_WIDTH].astype(BF16)
    cs = cs_ref[...].astype(BF16)
    for g in range(FNET_GROUPS):
        sl = slice(g * FNET_GROUP_DIM, (g + 1) * FNET_GROUP_DIM)
        ab = _dot(f[:, sl], cs)
        z_ref[0, 0, g] = ab[:, :FNET_GROUP_DIM]
        z_ref[0, 1, g] = ab[:, FNET_GROUP_DIM:]


def _proj(x, gpre, win, gq, wuq, gkv, wukv, cs, cos, sina, sinb, tm):
    b, s, _ = x.shape
    wide = MLA_HEADS * HEAD_PAD
    tok = lambda w: pl.BlockSpec((1, tm, w), lambda bi, i: (bi, i, 0))
    tab = pl.BlockSpec((tm, HEAD_PAD), lambda bi, i: (i, 0))
    return pl.pallas_call(
        _proj_kernel,
        grid=(b, s // tm),
        in_specs=[tok(D_MODEL), _const_spec(gpre.shape), _const_spec(win.shape),
                  _const_spec(gq.shape), _const_spec(wuq.shape), _const_spec(gkv.shape),
                  _const_spec(wukv.shape), _const_spec(cs.shape), tab, tab, tab],
        out_specs=[pl.BlockSpec((1, wide, tm), lambda bi, i: (bi, 0, i)),
                   tok(wide),
                   pl.BlockSpec((1, MLA_WIDTH, tm), lambda bi, i: (bi, 0, i)),
                   pl.BlockSpec((1, 2, FNET_GROUPS, tm, FNET_GROUP_DIM),
                                lambda bi, i: (bi, 0, 0, i, 0))],
        out_shape=[jax.ShapeDtypeStruct((b, wide, s), BF16),
                   jax.ShapeDtypeStruct((b, s, wide), BF16),
                   jax.ShapeDtypeStruct((b, MLA_WIDTH, s), BF16),
                   jax.ShapeDtypeStruct((b, 2, FNET_GROUPS, s, FNET_GROUP_DIM), F32)],
        compiler_params=_params(("parallel", "parallel")),
        name="proj",
    )(x, gpre, win, gq, wuq, gkv, wukv, cs, cos, sina, sinb)


def _attn_kernel(qt_ref, k_ref, vt_ref, o_ref, st_all, mc_all, *, tq, tk, heads):
    st_ref, mc_ref = (st_all.at[0], st_all.at[1]), (mc_all.at[0], mc_all.at[1])
    s = k_ref.shape[1]
    n = s // tk
    i = pl.program_id(2)
    i_next = jnp.minimum(i + 1, s // tq - 1)
    ones = jnp.ones((DENOM_ROWS, tk), BF16)

    def scores(qi, c, slot):
        q_off = pl.multiple_of(qi * tq, tq)
        k_off = pl.multiple_of(c * tk, tk)
        for hd in range(heads):
            qt = qt_ref[0, hd * HEAD_PAD:(hd + 1) * HEAD_PAD, pl.ds(q_off, tq)]
            kc = k_ref[0, pl.ds(k_off, tk), hd * HEAD_PAD:(hd + 1) * HEAD_PAD]
            st = _dot(kc, qt)
            st_ref[slot][hd, :tk, :] = st
            mc_ref[slot][hd] = jnp.max(st, axis=0, keepdims=True)

    def update(c, slot, carry):
        k_off = pl.multiple_of(c * tk, tk)
        new = []
        for hd in range(heads):
            m, acc = carry[hd]
            m_new = jnp.maximum(m, mc_ref[slot][hd])
            alpha = jnp.exp2(m - m_new)
            p = jnp.exp2(st_ref[slot][hd, :tk, :] - m_new).astype(BF16)
            vc = vt_ref[0, hd * V_HEAD:(hd + 1) * V_HEAD, pl.ds(k_off, tk)]
            acc = alpha * acc + _dot(jnp.concatenate([vc, ones], axis=0), p)
            new.append((m_new, acc))
        return tuple(new)

    @pl.when(i == 0)
    def _():
        scores(0, 0, 0)

    def step(c, slot, carry):
        wraps = c + 1 == n
        scores(jnp.where(wraps, i_next, i), jnp.where(wraps, 0, c + 1), 1 - slot)
        return update(c, slot, carry)

    def body(c2, carry):
        return step(2 * c2 + 1, 1, step(2 * c2, 0, carry))

    init = tuple((jnp.full((1, tq), -jnp.inf, F32), jnp.zeros((V_HEAD + DENOM_ROWS, tq), F32))
                 for _ in range(heads))
    carry = jax.lax.fori_loop(0, n // 2, body, init)
    outs = [acc[:V_HEAD] / acc[V_HEAD:V_HEAD + 1] for _, acc in carry]
    o_ref[0] = jnp.concatenate(outs, axis=0).T.astype(BF16)


def _attn(qt, k, vt, tq, tk):
    b, _, s = qt.shape
    heads = LANES // V_HEAD
    groups = MLA_HEADS // heads
    assert s % tq == 0 and s % (2 * tk) == 0
    return pl.pallas_call(
        functools.partial(_attn_kernel, tq=tq, tk=tk, heads=heads),
        grid=(b, groups, s // tq),
        in_specs=[pl.BlockSpec((1, heads * HEAD_PAD, s), lambda bi, j, i: (bi, j, 0)),
                  pl.BlockSpec((1, s, heads * HEAD_PAD), lambda bi, j, i: (bi, 0, j)),
                  pl.BlockSpec((1, heads * V_HEAD, s), lambda bi, j, i: (bi, j, 0))],
        out_specs=pl.BlockSpec((1, tq, heads * V_HEAD), lambda bi, j, i: (bi, i, j)),
        out_shape=jax.ShapeDtypeStruct((b, s, MLA_WIDTH), BF16),
        scratch_shapes=[pltpu.VMEM((2, heads, tk + SCORE_PAD, tq), F32),
                        pltpu.VMEM((2, heads, 1, tq), F32)],
        compiler_params=_params(("parallel", "parallel", "arbitrary")),
        name="attn",
    )(qt, k, vt)


def _dft1_kernel(z_ref, f1_ref, y_ref):
    groups, n1, r, c = z_ref.shape[2:]
    f1 = f1_ref[...].astype(BF16)
    flat = lambda ref, part, g: ref.at[0, part, g].reshape(n1 * r, c)
    for j in range(r):
        rows = pl.ds(j, n1, stride=r)
        gather = lambda part: jnp.concatenate(
            [flat(z_ref, part, g)[rows, :] for g in range(groups)], axis=1).astype(BF16)
        y = _dot(f1[:, :n1], gather(0)) + _dot(f1[:, n1:], gather(1))
        for g in range(groups):
            flat(y_ref, 0, g)[rows, :] = y[:n1, g * c:(g + 1) * c]
            flat(y_ref, 1, g)[rows, :] = y[n1:, g * c:(g + 1) * c]


def _dft1(z, f1, r):
    b, _, groups, n1, n2, c = z.shape
    spec = pl.BlockSpec((1, 2, groups, n1, r, c), lambda bi, j: (bi, 0, 0, 0, j, 0))
    return pl.pallas_call(
        _dft1_kernel,
        grid=(b, n2 // r),
        in_specs=[spec, _const_spec(f1.shape)],
        out_specs=spec,
        out_shape=jax.ShapeDtypeStruct(z.shape, F32),
        compiler_params=_params(("parallel", "parallel")),
        name="dft1",
    )(z, f1)


def _dft2_kernel(y_ref, w_ref, wf_ref, o_ref):
    groups, kb, n2, c = y_ref.shape[2:]
    for t in range(kb):
        w = w_ref[t].astype(BF16)
        part = lambda p: jnp.concatenate(
            [y_ref[0, p, g, t] for g in range(groups)], axis=1).astype(BF16)
        xr = (_dot(w[:, :n2], part(0)) + _dot(w[:, n2:], part(1))).astype(BF16)
        for g in range(groups):
            lo = t * groups * c + g * c
            o_ref[0, :, lo:lo + c] = _dot(xr[:, g * c:(g + 1) * c], wf_ref[g]).astype(BF16)


def _dft2(y, w, wf, kb):
    b, _, groups, n1, n2, c = y.shape
    return pl.pallas_call(
        _dft2_kernel,
        grid=(b, n1 // kb),
        in_specs=[pl.BlockSpec((1, 2, groups, kb, n2, c), lambda bi, j: (bi, 0, 0, j, 0, 0)),
                  pl.BlockSpec((kb, n2, 2 * n2), lambda bi, j: (j, 0, 0)),
                  _const_spec(wf.shape)],
        out_specs=pl.BlockSpec((1, n2, kb * groups * c), lambda bi, j: (bi, 0, j)),
        out_shape=jax.ShapeDtypeStruct((b, n2, n1 * groups * c), BF16),
        compiler_params=_params(("parallel", "parallel")),
        name="dft2",
    )(y, w, wf)


HALO = 16


def _tail_kernel(a_ref, ap_ref, an_ref, f_ref, fp_ref, fn_ref, x_ref, xp_ref, xn_ref,
                 woa_ref, wof_ref, gpm_ref, gpf_ref, wg_ref, wu_ref, cw_ref, cb_ref, wd_ref,
                 gpo_ref, y_ref, act_ref, *, ffc):
    i = pl.program_id(1)
    tm = x_ref.shape[1]
    rows = tm + 2 * HALO
    ext = lambda p, c, n: jnp.concatenate([p[0], c[0], n[0]], axis=0)
    mix = _dot(ext(ap_ref, a_ref, an_ref), woa_ref[...]) + _dot(ext(fp_ref, f_ref, fn_ref), wof_ref[...])
    x1 = ext(xp_ref, x_ref, xn_ref) + _rms(mix, gpm_ref[...])
    row = jax.lax.broadcasted_iota(jnp.int32, (rows, 1), 0)
    inside = ((row >= HALO) | (i > 0)) & ((row < HALO + tm) | (i < pl.num_programs(1) - 1))
    hext = jnp.where(inside, _rms(x1, gpf_ref[...]), 0.0).astype(BF16)
    h = hext[HALO:HALO + tm]
    for c in range(D_FF // ffc):
        sl = slice(c * ffc, (c + 1) * ffc)
        g = _dot(hext, wg_ref[:, sl])
        u = _dot(h, wu_ref[:, sl])
        g_prev = pltpu.roll(g, 1, 0)[HALO:HALO + tm]
        g_next = pltpu.roll(g, rows - 1, 0)[HALO:HALO + tm]
        gate = (cb_ref[:, sl] + g_prev * cw_ref[0:1, sl] + g[HALO:HALO + tm] * cw_ref[1:2, sl]
                + g_next * cw_ref[2:3, sl])
        inner = math.sqrt(2.0 / math.pi) * (gate + 0.044715 * (gate * gate * gate))
        act = 0.5 * gate * (1.0 + jnp.tanh(inner)) * u
        act_ref[:, sl] = act.astype(BF16)
    out = _dot(act_ref[...], wd_ref[...])
    y_ref[0] = x1[HALO:HALO + tm] + _rms(out, gpo_ref[...])


def _tail(a, f, x, woa, wof, gpm, gpf, wg, wu, cw, cb, wd, gpo, tm, ffc):
    b, s, _ = x.shape
    per = tm // HALO
    last = s // HALO - 1
    single = lambda shape: pl.BlockSpec(shape, lambda *_: (0,) * len(shape),
                                        pipeline_mode=pl.Buffered(1))

    def with_halo(width):
        return [pl.BlockSpec((1, tm, width), lambda bi, i: (bi, i, 0)),
                pl.BlockSpec((1, HALO, width), lambda bi, i: (bi, jnp.maximum(i * per - 1, 0), 0)),
                pl.BlockSpec((1, HALO, width), lambda bi, i: (bi, jnp.minimum((i + 1) * per, last), 0))]

    return pl.pallas_call(
        functools.partial(_tail_kernel, ffc=ffc),
        grid=(b, s // tm),
        in_specs=with_halo(MLA_WIDTH) + with_halo(FNET_WIDTH) + with_halo(D_MODEL) + [
            single(woa.shape), single(wof.shape), _const_spec(gpm.shape), _const_spec(gpf.shape),
            single(wg.shape), single(wu.shape), _const_spec(cw.shape), _const_spec(cb.shape),
            single(wd.shape), _const_spec(gpo.shape)],
        out_specs=pl.BlockSpec((1, tm, D_MODEL), lambda bi, i: (bi, i, 0)),
        out_shape=jax.ShapeDtypeStruct(x.shape, F32),
        scratch_shapes=[pltpu.VMEM((tm, D_FF), BF16)],
        compiler_params=_params(("parallel", "arbitrary")),
        name="tail",
    )(a, a, a, f, f, f, x, x, x, woa, wof, gpm, gpf, wg, wu, cw, cb, wd, gpo)


def _rope_tables(s):
    ang = np.arange(s, dtype=np.float64)[:, None] * (
        ROPE_THETA ** (-np.arange(0, QK_ROPE, 2, dtype=np.float64) / QK_ROPE))[None, :]
    cos, sin = np.cos(ang), np.sin(ang)
    zeros = np.zeros((s, QK_NOPE))
    zh = np.zeros((s, HALF_ROPE))
    tail = np.zeros((s, HEAD_PAD - QK_NOPE - QK_ROPE))
    cos_t = np.concatenate([np.ones((s, QK_NOPE)), cos, cos, tail], axis=1)
    sina = np.concatenate([zeros, zh, sin, tail], axis=1)
    sinb = np.concatenate([zeros, -sin, zh, tail], axis=1)
    return tuple(jnp.asarray(t, F32) for t in (cos_t, sina, sinb))


def _angles(num, den):
    return (2.0 * np.pi / den) * (num % den).astype(np.float64)


def _dft_tables(s):
    n2 = DFT_N2
    n1 = s // n2
    c = np.arange(FNET_GROUP_DIM, dtype=np.int64)
    ang = _angles(c[:, None] * c[None, :], FNET_GROUP_DIM)
    cs = np.concatenate([np.cos(ang), np.sin(ang)], axis=1) / math.sqrt(FNET_GROUP_DIM)
    k1 = np.arange(n1, dtype=np.int64)
    a1 = _angles(k1[:, None] * k1[None, :], n1)
    c1, s1 = np.cos(a1), np.sin(a1)
    f1 = np.concatenate([np.concatenate([c1, -s1], axis=1),
                         np.concatenate([s1, c1], axis=1)], axis=0) / math.sqrt(n1)
    k2 = np.arange(n2, dtype=np.int64)
    kk = k1[:, None, None] + n1 * k2[None, :, None]
    a2 = _angles(kk * k2[None, None, :], s)
    w = np.concatenate([np.cos(a2), -np.sin(a2)], axis=2) / math.sqrt(n2)
    return tuple(jnp.asarray(t, F32) for t in (cs, f1, w))


def _prep_weights(g_pre_mix, w_in, g_q, w_uq, g_kv, w_ukv, w_fnet, w_out, g_post_mix,
                  g_pre_ffn, w_gate, w_up, conv_w, conv_b, w_down, g_post_ffn):
    win = w_in[0]
    f_lo = Q_LORA + KV_LORA + QK_ROPE
    kr_cols = jnp.concatenate([jnp.zeros((D_MODEL, QK_NOPE), F32),
                               win[:, Q_LORA + KV_LORA:f_lo],
                               jnp.zeros((D_MODEL, HEAD_PAD - QK_NOPE - QK_ROPE), F32)], axis=1)
    win_p = jnp.concatenate([win[:, :Q_LORA + KV_LORA], win[:, f_lo:], kr_cols], axis=1)
    wuq = w_uq[0].reshape(Q_LORA, MLA_HEADS, QK_NOPE + QK_ROPE)
    wuq = jnp.pad(wuq, ((0, 0), (0, 0), (0, HEAD_PAD - QK_NOPE - QK_ROPE)))
    wuq = wuq.reshape(Q_LORA, MLA_HEADS * HEAD_PAD)
    wukv = w_ukv[0].reshape(KV_LORA, MLA_HEADS, QK_NOPE + V_HEAD)
    wuk = jnp.pad(wukv[..., :QK_NOPE], ((0, 0), (0, 0), (0, HEAD_PAD - QK_NOPE)))
    wukv = jnp.concatenate([wuk.reshape(KV_LORA, MLA_HEADS * HEAD_PAD),
                            wukv[..., QK_NOPE:].reshape(KV_LORA, MLA_WIDTH)], axis=1)
    return dict(
        gpre=g_pre_mix, win=win_p.astype(BF16), gq=g_q, wuq=wuq.astype(BF16), gkv=g_kv,
        wukv=wukv.astype(BF16), wf=w_fnet[0].astype(BF16),
        woa=w_out[0, :MLA_WIDTH].astype(BF16), wof=w_out[0, MLA_WIDTH:].astype(BF16),
        gpm=g_post_mix, gpf=g_pre_ffn, wg=w_gate[0].astype(BF16), wu=w_up[0].astype(BF16),
        cw=conv_w[0], cb=conv_b, wd=w_down[0].astype(BF16), gpo=g_post_ffn)


def _trunk(x, w):
    b, s, _ = x.shape
    n2 = DFT_N2
    n1 = s // n2
    cos, sina, sinb = _rope_tables(s)
    cs, f1, wtab = _dft_tables(s)
    qt, k, vt, z = _proj(x, w["gpre"], w["win"], w["gq"], w["wuq"], w["gkv"], w["wukv"], cs,
                         cos, sina, sinb, tm=512)
    a = _attn(qt, k, vt, tq=256, tk=2048)
    y = _dft1(z.reshape(b, 2, FNET_GROUPS, n1, n2, FNET_GROUP_DIM), f1, r=16)
    f = _dft2(y, wtab, w["wf"], kb=8)
    f = f.reshape(b, s, FNET_WIDTH)
    return _tail(a, f, x, w["woa"], w["wof"], w["gpm"], w["gpf"], w["wg"], w["wu"], w["cw"],
                 w["cb"], w["wd"], w["gpo"], tm=512, ffc=256)


def kernel(x_prompt, x_sample, g_pre_mix, w_in, g_q, w_uq, g_kv, w_ukv, w_fnet, w_out,
           g_post_mix, g_pre_ffn, w_gate, w_up, conv_w, conv_b, w_down, g_post_ffn):
    w = _prep_weights(g_pre_mix, w_in, g_q, w_uq, g_kv, w_ukv, w_fnet, w_out, g_post_mix,
                      g_pre_ffn, w_gate, w_up, conv_w, conv_b, w_down, g_post_ffn)
    return _trunk(x_prompt, w), _trunk(x_sample, w)
```

```python
import functools
import math

import jax
import jax.numpy as jnp
import numpy as np
from jax.experimental import pallas as pl
from jax.experimental.pallas import tpu as pltpu

D_MODEL = 1024
MLA_HEADS = 8
QK_NOPE = 64
QK_ROPE = 32
V_HEAD = 64
MLA_WIDTH = MLA_HEADS * V_HEAD
Q_LORA = 256
KV_LORA = 256
FNET_GROUPS = 4
FNET_WIDTH = 512
FNET_GROUP_DIM = 128
D_FF = 2816
ROPE_THETA = 10000.0
RMS_EPS = 1e-6
ATTN_SCALE = 1.0 / math.sqrt(QK_NOPE + QK_ROPE)
Q_SCALE = ATTN_SCALE * math.log2(math.e)

LANES = 128
HEAD_PAD = 128
HALF_ROPE = QK_ROPE // 2
DENOM_ROWS = 16
SCORE_PAD = 8
DFT_N2 = 128
VMEM_LIMIT = 56 * 1024 * 1024

BF16 = jnp.bfloat16
F32 = jnp.float32


def _rms(x, g):
    return x * jax.lax.rsqrt(jnp.mean(x * x, axis=-1, keepdims=True) + RMS_EPS) * g


def _dot(a, b):
    return jnp.dot(a, b, preferred_element_type=F32)


def _params(sem):
    return pltpu.CompilerParams(dimension_semantics=sem, vmem_limit_bytes=VMEM_LIMIT)


def _const_spec(shape):
    zeros = (0,) * len(shape)
    return pl.BlockSpec(shape, lambda *_: zeros)


def _proj_kernel(x_ref, gpre_ref, win_ref, gq_ref, wuq_ref, gkv_ref, wukv_ref, cs_ref,
                 cos_ref, sina_ref, sinb_ref, qt_ref, k_ref, vt_ref, z_ref):
    h = _rms(x_ref[0], gpre_ref[...]).astype(BF16)
    p = _dot(h, win_ref[...])
    cq = _rms(p[:, :Q_LORA], gq_ref[...]).astype(BF16)
    q = _dot(cq, wuq_ref[...])
    ckv = _rms(p[:, Q_LORA:Q_LORA + KV_LORA], gkv_ref[...]).astype(BF16)
    kv = _dot(ckv, wukv_ref[...])
    cos, sina, sinb = cos_ref[...], sina_ref[...], sinb_ref[...]

    def rope(t):
        return (t * cos + pltpu.roll(t, HALF_ROPE, 1) * sina
                + pltpu.roll(t, HEAD_PAD - HALF_ROPE, 1) * sinb)

    kr = rope(p[:, 2 * FNET_WIDTH:])
    for hd in range(MLA_HEADS):
        sl = slice(hd * HEAD_PAD, (hd + 1) * HEAD_PAD)
        qh = rope(q[:, sl]) * Q_SCALE
        qt_ref[0, sl, :] = qh.T.astype(BF16)
        k_ref[0, :, sl] = (kv[:, sl] + kr).astype(BF16)
    vt_ref[0] = kv[:, MLA_HEADS * HEAD_PAD:].T.astype(BF16)
    f = p[:, Q_LORA + KV_LORA:2 * FNET_WIDTH].astype(BF16)
    cs = cs_ref[...].astype(BF16)
    for g in range(FNET_GROUPS):
        sl = slice(g * FNET_GROUP_DIM, (g + 1) * FNET_GROUP_DIM)
        ab = _dot(f[:, sl], cs)
        z_ref[0, 0, g] = ab[:, :FNET_GROUP_DIM]
        z_ref[0, 1, g] = ab[:, FNET_GROUP_DIM:]


def _proj(x, gpre, win, gq, wuq, gkv, wukv, cs, cos, sina, sinb, tm):
    b, s, _ = x.shape
    wide = MLA_HEADS * HEAD_PAD
    tok = lambda w: pl.BlockSpec((1, tm, w), lambda bi, i: (bi, i, 0))
    tab = pl.BlockSpec((tm, HEAD_PAD), lambda bi, i: (i, 0))
    return pl.pallas_call(
        _proj_kernel,
        grid=(b, s // tm),
        in_specs=[tok(D_MODEL), _const_spec(gpre.shape), _const_spec(win.shape),
                  _const_spec(gq.shape), _const_spec(wuq.shape), _const_spec(gkv.shape),
                  _const_spec(wukv.shape), _const_spec(cs.shape), tab, tab, tab],
        out_specs=[pl.BlockSpec((1, wide, tm), lambda bi, i: (bi, 0, i)),
                   tok(wide),
                   pl.BlockSpec((1, MLA_WIDTH, tm), lambda bi, i: (bi, 0, i)),
                   pl.BlockSpec((1, 2, FNET_GROUPS, tm, FNET_GROUP_DIM),
                                lambda bi, i: (bi, 0, 0, i, 0))],
        out_shape=[jax.ShapeDtypeStruct((b, wide, s), BF16),
                   jax.ShapeDtypeStruct((b, s, wide), BF16),
                   jax.ShapeDtypeStruct((b, MLA_WIDTH, s), BF16),
                   jax.ShapeDtypeStruct((b, 2, FNET_GROUPS, s, FNET_GROUP_DIM), F32)],
        compiler_params=_params(("parallel", "parallel")),
        name="proj",
    )(x, gpre, win, gq, wuq, gkv, wukv, cs, cos, sina, sinb)


def _attn_kernel(qt_ref, k_ref, vt_ref, o_ref, st_all, mc_all, *, tq, tk, heads):
    st_ref, mc_ref = (st_all.at[0], st_all.at[1]), (mc_all.at[0], mc_all.at[1])
    s = k_ref.shape[1]
    n = s // tk
    i = pl.program_id(2)
    i_next = jnp.minimum(i + 1, s // tq - 1)
    ones = jnp.ones((DENOM_ROWS, tk), BF16)

    def scores(hd, qi, c, slot):
        q_off = pl.multiple_of(qi * tq, tq)
        k_off = pl.multiple_of(c * tk, tk)
        qt = qt_ref[0, hd * HEAD_PAD:(hd + 1) * HEAD_PAD, pl.ds(q_off, tq)]
        kc = k_ref[0, pl.ds(k_off, tk), hd * HEAD_PAD:(hd + 1) * HEAD_PAD]
        st = _dot(kc, qt)
        st_ref[slot][hd, :tk, :] = st
        mc_ref[slot][hd] = jnp.max(st, axis=0, keepdims=True)

    def update(hd, c, slot, m, acc):
        k_off = pl.multiple_of(c * tk, tk)
        m_new = jnp.maximum(m, mc_ref[slot][hd])
        alpha = jnp.exp2(m - m_new)
        p = jnp.exp2(st_ref[slot][hd, :tk, :] - m_new).astype(BF16)
        vc = vt_ref[0, hd * V_HEAD:(hd + 1) * V_HEAD, pl.ds(k_off, tk)]
        return m_new, alpha * acc + _dot(jnp.concatenate([vc, ones], axis=0), p)

    @pl.when(i == 0)
    def _():
        for hd in range(heads):
            scores(hd, 0, 0, 0)

    def step(c, slot, carry):
        wraps = c + 1 == n
        qi, c_next = jnp.where(wraps, i_next, i), jnp.where(wraps, 0, c + 1)
        scores(0, qi, c_next, 1 - slot)
        new = tuple(update(hd, c, slot, *carry[hd]) for hd in range(heads))
        for hd in range(1, heads):
            scores(hd, qi, c_next, 1 - slot)
        return new

    def body(c2, carry):
        return step(2 * c2 + 1, 1, step(2 * c2, 0, carry))

    init = tuple((jnp.full((1, tq), -jnp.inf, F32), jnp.zeros((V_HEAD + DENOM_ROWS, tq), F32))
                 for _ in range(heads))
    carry = jax.lax.fori_loop(0, n // 2, body, init)
    outs = [acc[:V_HEAD] / acc[V_HEAD:V_HEAD + 1] for _, acc in carry]
    o_ref[0] = jnp.concatenate(outs, axis=0).T.astype(BF16)


def _attn(qt, k, vt, tq, tk):
    b, _, s = qt.shape
    heads = LANES // V_HEAD
    groups = MLA_HEADS // heads
    assert s % tq == 0 and s % (2 * tk) == 0
    return pl.pallas_call(
        functools.partial(_attn_kernel, tq=tq, tk=tk, heads=heads),
        grid=(b, groups, s // tq),
        in_specs=[pl.BlockSpec((1, heads * HEAD_PAD, s), lambda bi, j, i: (bi, j, 0)),
                  pl.BlockSpec((1, s, heads * HEAD_PAD), lambda bi, j, i: (bi, 0, j)),
                  pl.BlockSpec((1, heads * V_HEAD, s), lambda bi, j, i: (bi, j, 0))],
        out_specs=pl.BlockSpec((1, tq, heads * V_HEAD), lambda bi, j, i: (bi, i, j)),
        out_shape=jax.ShapeDtypeStruct((b, s, MLA_WIDTH), BF16),
        scratch_shapes=[pltpu.VMEM((2, heads, tk + SCORE_PAD, tq), F32),
                        pltpu.VMEM((2, heads, 1, tq), F32)],
        compiler_params=_params(("parallel", "parallel", "arbitrary")),
        name="attn",
    )(qt, k, vt)


def _dft1_kernel(z_ref, f1_ref, y_ref):
    groups, n1, r, c = z_ref.shape[2:]
    f1 = f1_ref[...].astype(BF16)
    flat = lambda ref, part, g: ref.at[0, part, g].reshape(n1 * r, c)
    for j in range(r):
        rows = pl.ds(j, n1, stride=r)
        gather = lambda part: jnp.concatenate(
            [flat(z_ref, part, g)[rows, :] for g in range(groups)], axis=1).astype(BF16)
        y = _dot(f1[:, :n1], gather(0)) + _dot(f1[:, n1:], gather(1))
        for g in range(groups):
            flat(y_ref, 0, g)[rows, :] = y[:n1, g * c:(g + 1) * c]
            flat(y_ref, 1, g)[rows, :] = y[n1:, g * c:(g + 1) * c]


def _dft1(z, f1, r):
    b, _, groups, n1, n2, c = z.shape
    spec = pl.BlockSpec((1, 2, groups, n1, r, c), lambda bi, j: (bi, 0, 0, 0, j, 0))
    return pl.pallas_call(
        _dft1_kernel,
        grid=(b, n2 // r),
        in_specs=[spec, _const_spec(f1.shape)],
        out_specs=spec,
        out_shape=jax.ShapeDtypeStruct(z.shape, F32),
        compiler_params=_params(("parallel", "parallel")),
        name="dft1",
    )(z, f1)


def _dft2_kernel(y_ref, w_ref, wf_ref, o_ref):
    groups, kb, n2, c = y_ref.shape[2:]
    for t in range(kb):
        w = w_ref[t].astype(BF16)
        part = lambda p: jnp.concatenate(
            [y_ref[0, p, g, t] for g in range(groups)], axis=1).astype(BF16)
        xr = (_dot(w[:, :n2], part(0)) + _dot(w[:, n2:], part(1))).astype(BF16)
        for g in range(groups):
            lo = t * groups * c + g * c
            o_ref[0, :, lo:lo + c] = _dot(xr[:, g * c:(g + 1) * c], wf_ref[g]).astype(BF16)


def _dft2(y, w, wf, kb):
    b, _, groups, n1, n2, c = y.shape
    return pl.pallas_call(
        _dft2_kernel,
        grid=(b, n1 // kb),
        in_specs=[pl.BlockSpec((1, 2, groups, kb, n2, c), lambda bi, j: (bi, 0, 0, j, 0, 0)),
                  pl.BlockSpec((kb, n2, 2 * n2), lambda bi, j: (j, 0, 0)),
                  _const_spec(wf.shape)],
        out_specs=pl.BlockSpec((1, n2, kb * groups * c), lambda bi, j: (bi, 0, j)),
        out_shape=jax.ShapeDtypeStruct((b, n2, n1 * groups * c), BF16),
        compiler_params=_params(("parallel", "parallel")),
        name="dft2",
    )(y, w, wf)


HALO = 16


def _tail_kernel(a_ref, ap_ref, an_ref, f_ref, fp_ref, fn_ref, x_ref, xp_ref, xn_ref,
                 woa_ref, wof_ref, gpm_ref, gpf_ref, wg_ref, wu_ref, cw_ref, cb_ref, wd_ref,
                 gpo_ref, y_ref, act_ref, *, ffc):
    i = pl.program_id(1)
    tm = x_ref.shape[1]
    rows = tm + 2 * HALO
    ext = lambda p, c, n: jnp.concatenate([p[0], c[0], n[0]], axis=0)
    mix = _dot(ext(ap_ref, a_ref, an_ref), woa_ref[...]) + _dot(ext(fp_ref, f_ref, fn_ref), wof_ref[...])
    x1 = ext(xp_ref, x_ref, xn_ref) + _rms(mix, gpm_ref[...])
    row = jax.lax.broadcasted_iota(jnp.int32, (rows, 1), 0)
    inside = ((row >= HALO) | (i > 0)) & ((row < HALO + tm) | (i < pl.num_programs(1) - 1))
    hext = jnp.where(inside, _rms(x1, gpf_ref[...]), 0.0).astype(BF16)
    h = hext[HALO:HALO + tm]
    for c in range(D_FF // ffc):
        sl = slice(c * ffc, (c + 1) * ffc)
        g = _dot(hext, wg_ref[:, sl])
        u = _dot(h, wu_ref[:, sl])
        g_prev = pltpu.roll(g, 1, 0)[HALO:HALO + tm]
        g_next = pltpu.roll(g, rows - 1, 0)[HALO:HALO + tm]
        gate = (cb_ref[:, sl] + g_prev * cw_ref[0:1, sl] + g[HALO:HALO + tm] * cw_ref[1:2, sl]
                + g_next * cw_ref[2:3, sl])
        inner = math.sqrt(2.0 / math.pi) * (gate + 0.044715 * (gate * gate * gate))
        act = 0.5 * gate * (1.0 + jnp.tanh(inner)) * u
        act_ref[:, sl] = act.astype(BF16)
    out = _dot(act_ref[...], wd_ref[...])
    y_ref[0] = x1[HALO:HALO + tm] + _rms(out, gpo_ref[...])


def _tail(a, f, x, woa, wof, gpm, gpf, wg, wu, cw, cb, wd, gpo, tm, ffc):
    b, s, _ = x.shape
    per = tm // HALO
    last = s // HALO - 1
    single = lambda shape: pl.BlockSpec(shape, lambda *_: (0,) * len(shape),
                                        pipeline_mode=pl.Buffered(1))

    def with_halo(width):
        return [pl.BlockSpec((1, tm, width), lambda bi, i: (bi, i, 0)),
                pl.BlockSpec((1, HALO, width), lambda bi, i: (bi, jnp.maximum(i * per - 1, 0), 0)),
                pl.BlockSpec((1, HALO, width), lambda bi, i: (bi, jnp.minimum((i + 1) * per, last), 0))]

    return pl.pallas_call(
        functools.partial(_tail_kernel, ffc=ffc),
        grid=(b, s // tm),
        in_specs=with_halo(MLA_WIDTH) + with_halo(FNET_WIDTH) + with_halo(D_MODEL) + [
            single(woa.shape), single(wof.shape), _const_spec(gpm.shape), _const_spec(gpf.shape),
            single(wg.shape), single(wu.shape), _const_spec(cw.shape), _const_spec(cb.shape),
            single(wd.shape), _const_spec(gpo.shape)],
        out_specs=pl.BlockSpec((1, tm, D_MODEL), lambda bi, i: (bi, i, 0)),
        out_shape=jax.ShapeDtypeStruct(x.shape, F32),
        scratch_shapes=[pltpu.VMEM((tm, D_FF), BF16)],
        compiler_params=_params(("parallel", "arbitrary")),
        name="tail",
    )(a, a, a, f, f, f, x, x, x, woa, wof, gpm, gpf, wg, wu, cw, cb, wd, gpo)


def _rope_tables(s):
    ang = np.arange(s, dtype=np.float64)[:, None] * (
        ROPE_THETA ** (-np.arange(0, QK_ROPE, 2, dtype=np.float64) / QK_ROPE))[None, :]
    cos, sin = np.cos(ang), np.sin(ang)
    zeros = np.zeros((s, QK_NOPE))
    zh = np.zeros((s, HALF_ROPE))
    tail = np.zeros((s, HEAD_PAD - QK_NOPE - QK_ROPE))
    cos_t = np.concatenate([np.ones((s, QK_NOPE)), cos, cos, tail], axis=1)
    sina = np.concatenate([zeros, zh, sin, tail], axis=1)
    sinb = np.concatenate([zeros, -sin, zh, tail], axis=1)
    return tuple(jnp.asarray(t, F32) for t in (cos_t, sina, sinb))


def _angles(num, den):
    return (2.0 * np.pi / den) * (num % den).astype(np.float64)


def _dft_tables(s):
    n2 = DFT_N2
    n1 = s // n2
    c = np.arange(FNET_GROUP_DIM, dtype=np.int64)
    ang = _angles(c[:, None] * c[None, :], FNET_GROUP_DIM)
    cs = np.concatenate([np.cos(ang), np.sin(ang)], axis=1) / math.sqrt(FNET_GROUP_DIM)
    k1 = np.arange(n1, dtype=np.int64)
    a1 = _angles(k1[:, None] * k1[None, :], n1)
    c1, s1 = np.cos(a1), np.sin(a1)
    f1 = np.concatenate([np.concatenate([c1, -s1], axis=1),
                         np.concatenate([s1, c1], axis=1)], axis=0) / math.sqrt(n1)
    k2 = np.arange(n2, dtype=np.int64)
    kk = k1[:, None, None] + n1 * k2[None, :, None]
    a2 = _angles(kk * k2[None, None, :], s)
    w = np.concatenate([np.cos(a2), -np.sin(a2)], axis=2) / math.sqrt(n2)
    return tuple(jnp.asarray(t, F32) for t in (cs, f1, w))


def _prep_weights(g_pre_mix, w_in, g_q, w_uq, g_kv, w_ukv, w_fnet, w_out, g_post_mix,
                  g_pre_ffn, w_gate, w_up, conv_w, conv_b, w_down, g_post_ffn):
    win = w_in[0]
    f_lo = Q_LORA + KV_LORA + QK_ROPE
    kr_cols = jnp.concatenate([jnp.zeros((D_MODEL, QK_NOPE), F32),
                               win[:, Q_LORA + KV_LORA:f_lo],
                               jnp.zeros((D_MODEL, HEAD_PAD - QK_NOPE - QK_ROPE), F32)], axis=1)
    win_p = jnp.concatenate([win[:, :Q_LORA + KV_LORA], win[:, f_lo:], kr_cols], axis=1)
    wuq = w_uq[0].reshape(Q_LORA, MLA_HEADS, QK_NOPE + QK_ROPE)
    wuq = jnp.pad(wuq, ((0, 0), (0, 0), (0, HEAD_PAD - QK_NOPE - QK_ROPE)))
    wuq = wuq.reshape(Q_LORA, MLA_HEADS * HEAD_PAD)
    wukv = w_ukv[0].reshape(KV_LORA, MLA_HEADS, QK_NOPE + V_HEAD)
    wuk = jnp.pad(wukv[..., :QK_NOPE], ((0, 0), (0, 0), (0, HEAD_PAD - QK_NOPE)))
    wukv = jnp.concatenate([wuk.reshape(KV_LORA, MLA_HEADS * HEAD_PAD),
                            wukv[..., QK_NOPE:].reshape(KV_LORA, MLA_WIDTH)], axis=1)
    return dict(
        gpre=g_pre_mix, win=win_p.astype(BF16), gq=g_q, wuq=wuq.astype(BF16), gkv=g_kv,
        wukv=wukv.astype(BF16), wf=w_fnet[0].astype(BF16),
        woa=w_out[0, :MLA_WIDTH].astype(BF16), wof=w_out[0, MLA_WIDTH:].astype(BF16),
        gpm=g_post_mix, gpf=g_pre_ffn, wg=w_gate[0].astype(BF16), wu=w_up[0].astype(BF16),
        cw=conv_w[0], cb=conv_b, wd=w_down[0].astype(BF16), gpo=g_post_ffn)


def _trunk(x, w):
    b, s, _ = x.shape
    n2 = DFT_N2
    n1 = s // n2
    cos, sina, sinb = _rope_tables(s)
    cs, f1, wtab = _dft_tables(s)
    qt, k, vt, z = _proj(x, w["gpre"], w["win"], w["gq"], w["wuq"], w["gkv"], w["wukv"], cs,
                         cos, sina, sinb, tm=512)
    a = _attn(qt, k, vt, tq=256, tk=2048)
    y = _dft1(z.reshape(b, 2, FNET_GROUPS, n1, n2, FNET_GROUP_DIM), f1, r=16)
    f = _dft2(y, wtab, w["wf"], kb=8)
    f = f.reshape(b, s, FNET_WIDTH)
    return _tail(a, f, x, w["woa"], w["wof"], w["gpm"], w["gpf"], w["wg"], w["wu"], w["cw"],
                 w["cb"], w["wd"], w["gpo"], tm=512, ffc=256)


def kernel(x_prompt, x_sample, g_pre_mix, w_in, g_q, w_uq, g_kv, w_ukv, w_fnet, w_out,
           g_post_mix, g_pre_ffn, w_gate, w_up, conv_w, conv_b, w_down, g_post_ffn):
    w = _prep_weights(g_pre_mix, w_in, g_q, w_uq, g_kv, w_ukv, w_fnet, w_out, g_post_mix,
                      g_pre_ffn, w_gate, w_up, conv_w, conv_b, w_down, g_post_ffn)
    return _trunk(x_prompt, w), _trunk(x_sample, w)
```

```python
import functools
import math

import jax
import jax.numpy as jnp
import numpy as np
from jax.experimental import pallas as pl
from jax.experimental.pallas import tpu as pltpu

D_MODEL = 1024
MLA_HEADS = 8
QK_NOPE = 64
QK_ROPE = 32
V_HEAD = 64
MLA_WIDTH = MLA_HEADS * V_HEAD
Q_LORA = 256
KV_LORA = 256
FNET_GROUPS = 4
FNET_WIDTH = 512
FNET_GROUP_DIM = 128
D_FF = 2816
ROPE_THETA = 10000.0
RMS_EPS = 1e-6
ATTN_SCALE = 1.0 / math.sqrt(QK_NOPE + QK_ROPE)
Q_SCALE = ATTN_SCALE * math.log2(math.e)

LANES = 128
HEAD_PAD = 128
HALF_ROPE = QK_ROPE // 2
DENOM_ROWS = 16
SCORE_PAD = 8
MXU_TILE = 256
SCORE_ACC = (0, 64, 128)
VALUE_ACC = (192, 224)
SCORE_AHEAD = 2
SCORE_SLOTS = 4
DFT_N2 = 128
VMEM_LIMIT = 56 * 1024 * 1024

BF16 = jnp.bfloat16
F32 = jnp.float32


def _rms(x, g):
    return x * jax.lax.rsqrt(jnp.mean(x * x, axis=-1, keepdims=True) + RMS_EPS) * g


def _dot(a, b):
    return jnp.dot(a, b, preferred_element_type=F32)


def _params(sem):
    return pltpu.CompilerParams(dimension_semantics=sem, vmem_limit_bytes=VMEM_LIMIT)


def _const_spec(shape):
    zeros = (0,) * len(shape)
    return pl.BlockSpec(shape, lambda *_: zeros)


def _proj_kernel(x_ref, gpre_ref, win_ref, gq_ref, wuq_ref, gkv_ref, wukv_ref, cs_ref,
                 cos_ref, sina_ref, sinb_ref, qt_ref, k_ref, vt_ref, z_ref):
    h = _rms(x_ref[0], gpre_ref[...]).astype(BF16)
    p = _dot(h, win_ref[...])
    cq = _rms(p[:, :Q_LORA], gq_ref[...]).astype(BF16)
    q = _dot(cq, wuq_ref[...])
    ckv = _rms(p[:, Q_LORA:Q_LORA + KV_LORA], gkv_ref[...]).astype(BF16)
    kv = _dot(ckv, wukv_ref[...])
    cos, sina, sinb = cos_ref[...], sina_ref[...], sinb_ref[...]

    def rope(t):
        return (t * cos + pltpu.roll(t, HALF_ROPE, 1) * sina
                + pltpu.roll(t, HEAD_PAD - HALF_ROPE, 1) * sinb)

    kr = rope(p[:, 2 * FNET_WIDTH:])
    for hd in range(MLA_HEADS):
        sl = slice(hd * HEAD_PAD, (hd + 1) * HEAD_PAD)
        qh = rope(q[:, sl]) * Q_SCALE
        qt_ref[0, sl, :] = qh.T.astype(BF16)
        k_ref[0, :, sl] = (kv[:, sl] + kr).astype(BF16)
    vt_ref[0] = kv[:, MLA_HEADS * HEAD_PAD:].T.astype(BF16)
    f = p[:, Q_LORA + KV_LORA:2 * FNET_WIDTH].astype(BF16)
    cs = cs_ref[...].astype(BF16)
    for g in range(FNET_GROUPS):
        sl = slice(g * FNET_GROUP_DIM, (g + 1) * FNET_GROUP_DIM)
        ab = _dot(f[:, sl], cs)
        z_ref[0, 0, g] = ab[:, :FNET_GROUP_DIM]
        z_ref[0, 1, g] = ab[:, FNET_GROUP_DIM:]


def _proj(x, gpre, win, gq, wuq, gkv, wukv, cs, cos, sina, sinb, tm):
    b, s, _ = x.shape
    wide = MLA_HEADS * HEAD_PAD
    tok = lambda w: pl.BlockSpec((1, tm, w), lambda bi, i: (bi, i, 0))
    tab = pl.BlockSpec((tm, HEAD_PAD), lambda bi, i: (i, 0))
    return pl.pallas_call(
        _proj_kernel,
        grid=(b, s // tm),
        in_specs=[tok(D_MODEL), _const_spec(gpre.shape), _const_spec(win.shape),
                  _const_spec(gq.shape), _const_spec(wuq.shape), _const_spec(gkv.shape),
                  _const_spec(wukv.shape), _const_spec(cs.shape), tab, tab, tab],
        out_specs=[pl.BlockSpec((1, wide, tm), lambda bi, i: (bi, 0, i)),
                   tok(wide),
                   pl.BlockSpec((1, MLA_WIDTH, tm), lambda bi, i: (bi, 0, i)),
                   pl.BlockSpec((1, 2, FNET_GROUPS, tm, FNET_GROUP_DIM),
                                lambda bi, i: (bi, 0, 0, i, 0))],
        out_shape=[jax.ShapeDtypeStruct((b, wide, s), BF16),
                   jax.ShapeDtypeStruct((b, s, wide), BF16),
                   jax.ShapeDtypeStruct((b, MLA_WIDTH, s), BF16),
                   jax.ShapeDtypeStruct((b, 2, FNET_GROUPS, s, FNET_GROUP_DIM), F32)],
        compiler_params=_params(("parallel", "parallel")),
        name="proj",
    )(x, gpre, win, gq, wuq, gkv, wukv, cs, cos, sina, sinb)


def _attn_kernel(qt_ref, k_ref, vt_ref, o_ref, st_all, mc_all, *, tq, tk, heads):
    st_ref = tuple(st_all.at[k] for k in range(SCORE_SLOTS))
    mc_ref = tuple(mc_all.at[k] for k in range(SCORE_SLOTS))
    s = k_ref.shape[1]
    n = s // tk
    nt = tk // MXU_TILE
    i = pl.program_id(2)
    i_next = jnp.minimum(i + 1, s // tq - 1)
    ones = jnp.ones((DENOM_ROWS, MXU_TILE), BF16)
    pad_k = jnp.zeros((MXU_TILE, MXU_TILE - HEAD_PAD), BF16)
    pad_q = jnp.zeros((MXU_TILE - HEAD_PAD, tq), BF16)

    def q_rhs(hd, qi):
        q_off = pl.multiple_of(qi * tq, tq)
        qt = qt_ref[0, hd * HEAD_PAD:(hd + 1) * HEAD_PAD, pl.ds(q_off, tq)]
        return jnp.concatenate([qt, pad_q], axis=0)

    def k_lhs(hd, c, j):
        k_off = pl.multiple_of(c * tk + j * MXU_TILE, MXU_TILE)
        kc = k_ref[0, pl.ds(k_off, MXU_TILE), hd * HEAD_PAD:(hd + 1) * HEAD_PAD]
        return jnp.concatenate([kc, pad_k], axis=1)

    def pop_scores(hd, j, slot, mx):
        st = pltpu.matmul_pop(SCORE_ACC[j % len(SCORE_ACC)], (MXU_TILE, tq), F32, hd)
        st_ref[slot][hd, j * MXU_TILE:(j + 1) * MXU_TILE, :] = st
        return jnp.maximum(mx, jnp.max(st, axis=0, keepdims=True))

    lag = len(SCORE_ACC) - 1

    @pl.when(i == 0)
    def _():
        for c in range(SCORE_AHEAD):
            mx = [jnp.full((1, tq), -jnp.inf, F32) for _ in range(heads)]
            for hd in range(heads):
                pltpu.matmul_push_rhs(q_rhs(hd, 0), 0, hd)
            for j in range(nt + lag):
                for hd in range(heads):
                    if j < nt:
                        pltpu.matmul_acc_lhs(SCORE_ACC[j % len(SCORE_ACC)], k_lhs(hd, c, j), hd,
                                             0 if j == 0 else None)
                    if j >= lag:
                        mx[hd] = pop_scores(hd, j - lag, c, mx[hd])
            for hd in range(heads):
                mc_ref[c][hd] = mx[hd]

    def body(c4, carry):
        ms = [carry[hd][0] for hd in range(heads)]
        accs = [carry[hd][1] for hd in range(heads)]
        pending = []
        tile = 0

        def drain(keep):
            while len(pending) > keep:
                pending.pop(0)()

        for slot in range(SCORE_SLOTS):
            c = SCORE_SLOTS * c4 + slot
            ahead = c + SCORE_AHEAD
            wraps = ahead >= n
            qi, c_next = jnp.where(wraps, i_next, i), jnp.where(wraps, ahead - n, ahead)
            fill = (slot + SCORE_AHEAD) % SCORE_SLOTS
            value_acc = VALUE_ACC[slot % len(VALUE_ACC)]
            m_old = list(ms)
            ms = [jnp.maximum(ms[hd], mc_ref[slot][hd]) for hd in range(heads)]
            mx = [jnp.full((1, tq), -jnp.inf, F32) for _ in range(heads)]
            for j in range(nt):
                score_acc = SCORE_ACC[tile % len(SCORE_ACC)]
                tile += 1
                for hd in range(heads):
                    rows = slice(j * MXU_TILE, (j + 1) * MXU_TILE)
                    p = jnp.exp2(st_ref[slot][hd, rows, :] - ms[hd]).astype(BF16)
                    pltpu.matmul_push_rhs(p, 1, hd)
                    pltpu.matmul_push_rhs(q_rhs(hd, qi), 0, hd)
                    v_off = pl.multiple_of(c * tk + j * MXU_TILE, MXU_TILE)
                    vc = vt_ref[0, hd * V_HEAD:(hd + 1) * V_HEAD, pl.ds(v_off, MXU_TILE)]
                    pltpu.matmul_acc_lhs(value_acc, jnp.concatenate([vc, ones], axis=0), hd, 1)
                    pltpu.matmul_acc_lhs(score_acc, k_lhs(hd, c_next, j), hd, 0)

                def pop_tile(j=j, fill=fill, mx=mx, score_acc=score_acc, value_acc=value_acc,
                             m_old=m_old, m_new=list(ms)):
                    for hd in range(heads):
                        st = pltpu.matmul_pop(score_acc, (MXU_TILE, tq), F32, hd)
                        st_ref[fill][hd, j * MXU_TILE:(j + 1) * MXU_TILE, :] = st
                        mx[hd] = jnp.maximum(mx[hd], jnp.max(st, axis=0, keepdims=True))
                        if j == nt - 1:
                            mc_ref[fill][hd] = mx[hd]
                            pv = pltpu.matmul_pop(value_acc, (V_HEAD + DENOM_ROWS, tq), F32, hd)
                            accs[hd] = jnp.exp2(m_old[hd] - m_new[hd]) * accs[hd] + pv

                drain(lag - 1)
                pending.append(pop_tile)
        drain(0)
        return tuple((ms[hd], accs[hd]) for hd in range(heads))

    init = tuple((jnp.full((1, tq), -jnp.inf, F32), jnp.zeros((V_HEAD + DENOM_ROWS, tq), F32))
                 for _ in range(heads))
    carry = jax.lax.fori_loop(0, n // SCORE_SLOTS, body, init)
    outs = [acc[:V_HEAD] / acc[V_HEAD:V_HEAD + 1] for _, acc in carry]
    o_ref[0] = jnp.concatenate(outs, axis=0).T.astype(BF16)


def _attn(qt, k, vt, tq, tk):
    b, _, s = qt.shape
    heads = LANES // V_HEAD
    groups = MLA_HEADS // heads
    assert s % tq == 0 and s % (SCORE_SLOTS * tk) == 0 and tq == MXU_TILE and tk % MXU_TILE == 0
    return pl.pallas_call(
        functools.partial(_attn_kernel, tq=tq, tk=tk, heads=heads),
        grid=(b, groups, s // tq),
        in_specs=[pl.BlockSpec((1, heads * HEAD_PAD, s), lambda bi, j, i: (bi, j, 0)),
                  pl.BlockSpec((1, s, heads * HEAD_PAD), lambda bi, j, i: (bi, 0, j)),
                  pl.BlockSpec((1, heads * V_HEAD, s), lambda bi, j, i: (bi, j, 0))],
        out_specs=pl.BlockSpec((1, tq, heads * V_HEAD), lambda bi, j, i: (bi, i, j)),
        out_shape=jax.ShapeDtypeStruct((b, s, MLA_WIDTH), BF16),
        scratch_shapes=[pltpu.VMEM((SCORE_SLOTS, heads, tk + SCORE_PAD, tq), F32),
                        pltpu.VMEM((SCORE_SLOTS, heads, 1, tq), F32)],
        compiler_params=_params(("parallel", "parallel", "arbitrary")),
        name="attn",
    )(qt, k, vt)


def _dft1_kernel(z_ref, f1_ref, y_ref):
    groups, n1, r, c = z_ref.shape[2:]
    f1 = f1_ref[...].astype(BF16)
    flat = lambda ref, part, g: ref.at[0, part, g].reshape(n1 * r, c)
    for j in range(r):
        rows = pl.ds(j, n1, stride=r)
        gather = lambda part: jnp.concatenate(
            [flat(z_ref, part, g)[rows, :] for g in range(groups)], axis=1).astype(BF16)
        y = _dot(f1[:, :n1], gather(0)) + _dot(f1[:, n1:], gather(1))
        for g in range(groups):
            flat(y_ref, 0, g)[rows, :] = y[:n1, g * c:(g + 1) * c]
            flat(y_ref, 1, g)[rows, :] = y[n1:, g * c:(g + 1) * c]


def _dft1(z, f1, r):
    b, _, groups, n1, n2, c = z.shape
    spec = pl.BlockSpec((1, 2, groups, n1, r, c), lambda bi, j: (bi, 0, 0, 0, j, 0))
    return pl.pallas_call(
        _dft1_kernel,
        grid=(b, n2 // r),
        in_specs=[spec, _const_spec(f1.shape)],
        out_specs=spec,
        out_shape=jax.ShapeDtypeStruct(z.shape, F32),
        compiler_params=_params(("parallel", "parallel")),
        name="dft1",
    )(z, f1)


def _dft2_kernel(y_ref, w_ref, wf_ref, o_ref):
    groups, kb, n2, c = y_ref.shape[2:]
    for t in range(kb):
        w = w_ref[t].astype(BF16)
        part = lambda p: jnp.concatenate(
            [y_ref[0, p, g, t] for g in range(groups)], axis=1).astype(BF16)
        xr = (_dot(w[:, :n2], part(0)) + _dot(w[:, n2:], part(1))).astype(BF16)
        for g in range(groups):
            lo = t * groups * c + g * c
            o_ref[0, :, lo:lo + c] = _dot(xr[:, g * c:(g + 1) * c], wf_ref[g]).astype(BF16)


def _dft2(y, w, wf, kb):
    b, _, groups, n1, n2, c = y.shape
    return pl.pallas_call(
        _dft2_kernel,
        grid=(b, n1 // kb),
        in_specs=[pl.BlockSpec((1, 2, groups, kb, n2, c), lambda bi, j: (bi, 0, 0, j, 0, 0)),
                  pl.BlockSpec((kb, n2, 2 * n2), lambda bi, j: (j, 0, 0)),
                  _const_spec(wf.shape)],
        out_specs=pl.BlockSpec((1, n2, kb * groups * c), lambda bi, j: (bi, 0, j)),
        out_shape=jax.ShapeDtypeStruct((b, n2, n1 * groups * c), BF16),
        compiler_params=_params(("parallel", "parallel")),
        name="dft2",
    )(y, w, wf)


HALO = 16


def _tail_kernel(a_ref, ap_ref, an_ref, f_ref, fp_ref, fn_ref, x_ref, xp_ref, xn_ref,
                 woa_ref, wof_ref, gpm_ref, gpf_ref, wg_ref, wu_ref, cw_ref, cb_ref, wd_ref,
                 gpo_ref, y_ref, act_ref, *, ffc):
    i = pl.program_id(1)
    tm = x_ref.shape[1]
    rows = tm + 2 * HALO
    ext = lambda p, c, n: jnp.concatenate([p[0], c[0], n[0]], axis=0)
    mix = _dot(ext(ap_ref, a_ref, an_ref), woa_ref[...]) + _dot(ext(fp_ref, f_ref, fn_ref), wof_ref[...])
    x1 = ext(xp_ref, x_ref, xn_ref) + _rms(mix, gpm_ref[...])
    row = jax.lax.broadcasted_iota(jnp.int32, (rows, 1), 0)
    inside = ((row >= HALO) | (i > 0)) & ((row < HALO + tm) | (i < pl.num_programs(1) - 1))
    hext = jnp.where(inside, _rms(x1, gpf_ref[...]), 0.0).astype(BF16)
    h = hext[HALO:HALO + tm]
    for c in range(D_FF // ffc):
        sl = slice(c * ffc, (c + 1) * ffc)
        g = _dot(hext, wg_ref[:, sl])
        u = _dot(h, wu_ref[:, sl])
        g_prev = pltpu.roll(g, 1, 0)[HALO:HALO + tm]
        g_next = pltpu.roll(g, rows - 1, 0)[HALO:HALO + tm]
        gate = (cb_ref[:, sl] + g_prev * cw_ref[0:1, sl] + g[HALO:HALO + tm] * cw_ref[1:2, sl]
                + g_next * cw_ref[2:3, sl])
        inner = math.sqrt(2.0 / math.pi) * (gate + 0.044715 * (gate * gate * gate))
        act = 0.5 * gate * (1.0 + jnp.tanh(inner)) * u
        act_ref[:, sl] = act.astype(BF16)
    out = _dot(act_ref[...], wd_ref[...])
    y_ref[0] = x1[HALO:HALO + tm] + _rms(out, gpo_ref[...])


def _tail(a, f, x, woa, wof, gpm, gpf, wg, wu, cw, cb, wd, gpo, tm, ffc):
    b, s, _ = x.shape
    per = tm // HALO
    last = s // HALO - 1
    single = lambda shape: pl.BlockSpec(shape, lambda *_: (0,) * len(shape),
                                        pipeline_mode=pl.Buffered(1))

    def with_halo(width):
        return [pl.BlockSpec((1, tm, width), lambda bi, i: (bi, i, 0)),
                pl.BlockSpec((1, HALO, width), lambda bi, i: (bi, jnp.maximum(i * per - 1, 0), 0)),
                pl.BlockSpec((1, HALO, width), lambda bi, i: (bi, jnp.minimum((i + 1) * per, last), 0))]

    return pl.pallas_call(
        functools.partial(_tail_kernel, ffc=ffc),
        grid=(b, s // tm),
        in_specs=with_halo(MLA_WIDTH) + with_halo(FNET_WIDTH) + with_halo(D_MODEL) + [
            single(woa.shape), single(wof.shape), _const_spec(gpm.shape), _const_spec(gpf.shape),
            single(wg.shape), single(wu.shape), _const_spec(cw.shape), _const_spec(cb.shape),
            single(wd.shape), _const_spec(gpo.shape)],
        out_specs=pl.BlockSpec((1, tm, D_MODEL), lambda bi, i: (bi, i, 0)),
        out_shape=jax.ShapeDtypeStruct(x.shape, F32),
        scratch_shapes=[pltpu.VMEM((tm, D_FF), BF16)],
        compiler_params=_params(("parallel", "arbitrary")),
        name="tail",
    )(a, a, a, f, f, f, x, x, x, woa, wof, gpm, gpf, wg, wu, cw, cb, wd, gpo)


def _rope_tables(s):
    ang = np.arange(s, dtype=np.float64)[:, None] * (
        ROPE_THETA ** (-np.arange(0, QK_ROPE, 2, dtype=np.float64) / QK_ROPE))[None, :]
    cos, sin = np.cos(ang), np.sin(ang)
    zeros = np.zeros((s, QK_NOPE))
    zh = np.zeros((s, HALF_ROPE))
    tail = np.zeros((s, HEAD_PAD - QK_NOPE - QK_ROPE))
    cos_t = np.concatenate([np.ones((s, QK_NOPE)), cos, cos, tail], axis=1)
    sina = np.concatenate([zeros, zh, sin, tail], axis=1)
    sinb = np.concatenate([zeros, -sin, zh, tail], axis=1)
    return tuple(jnp.asarray(t, F32) for t in (cos_t, sina, sinb))


def _angles(num, den):
    return (2.0 * np.pi / den) * (num % den).astype(np.float64)


def _dft_tables(s):
    n2 = DFT_N2
    n1 = s // n2
    c = np.arange(FNET_GROUP_DIM, dtype=np.int64)
    ang = _angles(c[:, None] * c[None, :], FNET_GROUP_DIM)
    cs = np.concatenate([np.cos(ang), np.sin(ang)], axis=1) / math.sqrt(FNET_GROUP_DIM)
    k1 = np.arange(n1, dtype=np.int64)
    a1 = _angles(k1[:, None] * k1[None, :], n1)
    c1, s1 = np.cos(a1), np.sin(a1)
    f1 = np.concatenate([np.concatenate([c1, -s1], axis=1),
                         np.concatenate([s1, c1], axis=1)], axis=0) / math.sqrt(n1)
    k2 = np.arange(n2, dtype=np.int64)
    kk = k1[:, None, None] + n1 * k2[None, :, None]
    a2 = _angles(kk * k2[None, None, :], s)
    w = np.concatenate([np.cos(a2), -np.sin(a2)], axis=2) / math.sqrt(n2)
    return tuple(jnp.asarray(t, F32) for t in (cs, f1, w))


def _prep_weights(g_pre_mix, w_in, g_q, w_uq, g_kv, w_ukv, w_fnet, w_out, g_post_mix,
                  g_pre_ffn, w_gate, w_up, conv_w, conv_b, w_down, g_post_ffn):
    win = w_in[0]
    f_lo = Q_LORA + KV_LORA + QK_ROPE
    kr_cols = jnp.concatenate([jnp.zeros((D_MODEL, QK_NOPE), F32),
                               win[:, Q_LORA + KV_LORA:f_lo],
                               jnp.zeros((D_MODEL, HEAD_PAD - QK_NOPE - QK_ROPE), F32)], axis=1)
    win_p = jnp.concatenate([win[:, :Q_LORA + KV_LORA], win[:, f_lo:], kr_cols], axis=1)
    wuq = w_uq[0].reshape(Q_LORA, MLA_HEADS, QK_NOPE + QK_ROPE)
    wuq = jnp.pad(wuq, ((0, 0), (0, 0), (0, HEAD_PAD - QK_NOPE - QK_ROPE)))
    wuq = wuq.reshape(Q_LORA, MLA_HEADS * HEAD_PAD)
    wukv = w_ukv[0].reshape(KV_LORA, MLA_HEADS, QK_NOPE + V_HEAD)
    wuk = jnp.pad(wukv[..., :QK_NOPE], ((0, 0), (0, 0), (0, HEAD_PAD - QK_NOPE)))
    wukv = jnp.concatenate([wuk.reshape(KV_LORA, MLA_HEADS * HEAD_PAD),
                            wukv[..., QK_NOPE:].reshape(KV_LORA, MLA_WIDTH)], axis=1)
    return dict(
        gpre=g_pre_mix, win=win_p.astype(BF16), gq=g_q, wuq=wuq.astype(BF16), gkv=g_kv,
        wukv=wukv.astype(BF16), wf=w_fnet[0].astype(BF16),
        woa=w_out[0, :MLA_WIDTH].astype(BF16), wof=w_out[0, MLA_WIDTH:].astype(BF16),
        gpm=g_post_mix, gpf=g_pre_ffn, wg=w_gate[0].astype(BF16), wu=w_up[0].astype(BF16),
        cw=conv_w[0], cb=conv_b, wd=w_down[0].astype(BF16), gpo=g_post_ffn)


def _trunk(x, w):
    b, s, _ = x.shape
    n2 = DFT_N2
    n1 = s // n2
    cos, sina, sinb = _rope_tables(s)
    cs, f1, wtab = _dft_tables(s)
    qt, k, vt, z = _proj(x, w["gpre"], w["win"], w["gq"], w["wuq"], w["gkv"], w["wukv"], cs,
                         cos, sina, sinb, tm=512)
    a = _attn(qt, k, vt, tq=MXU_TILE, tk=s // SCORE_SLOTS)
    y = _dft1(z.reshape(b, 2, FNET_GROUPS, n1, n2, FNET_GROUP_DIM), f1, r=16)
    f = _dft2(y, wtab, w["wf"], kb=8)
    f = f.reshape(b, s, FNET_WIDTH)
    return _tail(a, f, x, w["woa"], w["wof"], w["gpm"], w["gpf"], w["wg"], w["wu"], w["cw"],
                 w["cb"], w["wd"], w["gpo"], tm=512, ffc=256)


def kernel(x_prompt, x_sample, g_pre_mix, w_in, g_q, w_uq, g_kv, w_ukv, w_fnet, w_out,
           g_post_mix, g_pre_ffn, w_gate, w_up, conv_w, conv_b, w_down, g_post_ffn):
    w = _prep_weights(g_pre_mix, w_in, g_q, w_uq, g_kv, w_ukv, w_fnet, w_out, g_post_mix,
                      g_pre_ffn, w_gate, w_up, conv_w, conv_b, w_down, g_post_ffn)
    return _trunk(x_prompt, w), _trunk(x_sample, w)
```

```python
import functools
import math

import jax
import jax.numpy as jnp
import numpy as np
from jax.experimental import pallas as pl
from jax.experimental.pallas import tpu as pltpu

D_MODEL = 1024
MLA_HEADS = 8
QK_NOPE = 64
QK_ROPE = 32
V_HEAD = 64
MLA_WIDTH = MLA_HEADS * V_HEAD
Q_LORA = 256
KV_LORA = 256
FNET_GROUPS = 4
FNET_WIDTH = 512
FNET_GROUP_DIM = 128
D_FF = 2816
ROPE_THETA = 10000.0
RMS_EPS = 1e-6
ATTN_SCALE = 1.0 / math.sqrt(QK_NOPE + QK_ROPE)
Q_SCALE = ATTN_SCALE * math.log2(math.e)

LANES = 128
HEAD_PAD = 128
HALF_ROPE = QK_ROPE // 2
DENOM_ROWS = 16
SCORE_PAD = 8
MXU_TILE = 256
SCORE_ACC = (0, 64, 128)
VALUE_ACC = (192, 224)
SCORE_AHEAD = 2
SCORE_SLOTS = 4
ATTN_STEP_TILES = 64
DFT_N2 = 128
VMEM_LIMIT = 56 * 1024 * 1024

BF16 = jnp.bfloat16
F32 = jnp.float32


def _rms(x, g):
    return x * jax.lax.rsqrt(jnp.mean(x * x, axis=-1, keepdims=True) + RMS_EPS) * g


def _dot(a, b):
    return jnp.dot(a, b, preferred_element_type=F32)


def _params(sem):
    return pltpu.CompilerParams(dimension_semantics=sem, vmem_limit_bytes=VMEM_LIMIT)


def _const_spec(shape):
    zeros = (0,) * len(shape)
    return pl.BlockSpec(shape, lambda *_: zeros)


def _proj_kernel(x_ref, gpre_ref, win_ref, gq_ref, wuq_ref, gkv_ref, wukv_ref, cs_ref,
                 cos_ref, sina_ref, sinb_ref, qt_ref, k_ref, vt_ref, z_ref):
    h = _rms(x_ref[0], gpre_ref[...]).astype(BF16)
    p = _dot(h, win_ref[...])
    cq = _rms(p[:, :Q_LORA], gq_ref[...]).astype(BF16)
    q = _dot(cq, wuq_ref[...])
    ckv = _rms(p[:, Q_LORA:Q_LORA + KV_LORA], gkv_ref[...]).astype(BF16)
    kv = _dot(ckv, wukv_ref[...])
    cos, sina, sinb = cos_ref[...], sina_ref[...], sinb_ref[...]

    def rope(t):
        return (t * cos + pltpu.roll(t, HALF_ROPE, 1) * sina
                + pltpu.roll(t, HEAD_PAD - HALF_ROPE, 1) * sinb)

    kr = rope(p[:, 2 * FNET_WIDTH:])
    for hd in range(MLA_HEADS):
        sl = slice(hd * HEAD_PAD, (hd + 1) * HEAD_PAD)
        qh = rope(q[:, sl]) * Q_SCALE
        qt_ref[0, sl, :] = qh.T.astype(BF16)
        k_ref[0, :, sl] = (kv[:, sl] + kr).astype(BF16)
    vt_ref[0] = kv[:, MLA_HEADS * HEAD_PAD:].T.astype(BF16)
    f = p[:, Q_LORA + KV_LORA:2 * FNET_WIDTH].astype(BF16)
    cs = cs_ref[...].astype(BF16)
    for g in range(FNET_GROUPS):
        sl = slice(g * FNET_GROUP_DIM, (g + 1) * FNET_GROUP_DIM)
        ab = _dot(f[:, sl], cs)
        z_ref[0, 0, g] = ab[:, :FNET_GROUP_DIM]
        z_ref[0, 1, g] = ab[:, FNET_GROUP_DIM:]


def _proj(x, gpre, win, gq, wuq, gkv, wukv, cs, cos, sina, sinb, tm):
    b, s, _ = x.shape
    wide = MLA_HEADS * HEAD_PAD
    tok = lambda w: pl.BlockSpec((1, tm, w), lambda bi, i: (bi, i, 0))
    tab = pl.BlockSpec((tm, HEAD_PAD), lambda bi, i: (i, 0))
    return pl.pallas_call(
        _proj_kernel,
        grid=(b, s // tm),
        in_specs=[tok(D_MODEL), _const_spec(gpre.shape), _const_spec(win.shape),
                  _const_spec(gq.shape), _const_spec(wuq.shape), _const_spec(gkv.shape),
                  _const_spec(wukv.shape), _const_spec(cs.shape), tab, tab, tab],
        out_specs=[pl.BlockSpec((1, wide, tm), lambda bi, i: (bi, 0, i)),
                   tok(wide),
                   pl.BlockSpec((1, MLA_WIDTH, tm), lambda bi, i: (bi, 0, i)),
                   pl.BlockSpec((1, 2, FNET_GROUPS, tm, FNET_GROUP_DIM),
                                lambda bi, i: (bi, 0, 0, i, 0))],
        out_shape=[jax.ShapeDtypeStruct((b, wide, s), BF16),
                   jax.ShapeDtypeStruct((b, s, wide), BF16),
                   jax.ShapeDtypeStruct((b, MLA_WIDTH, s), BF16),
                   jax.ShapeDtypeStruct((b, 2, FNET_GROUPS, s, FNET_GROUP_DIM), F32)],
        compiler_params=_params(("parallel", "parallel")),
        name="proj",
    )(x, gpre, win, gq, wuq, gkv, wukv, cs, cos, sina, sinb)


def _attn_kernel(qt_ref, k_ref, vt_ref, o_ref, st_all, mc_all, *, tq, tk, heads, qsubs):
    st_ref = tuple(st_all.at[k] for k in range(SCORE_SLOTS))
    mc_ref = tuple(mc_all.at[k] for k in range(SCORE_SLOTS))
    s = k_ref.shape[1]
    n = s // tk
    nt = tk // MXU_TILE
    i = pl.program_id(2)
    ones = jnp.ones((DENOM_ROWS, MXU_TILE), BF16)
    pad_k = jnp.zeros((MXU_TILE, MXU_TILE - HEAD_PAD), BF16)
    pad_q = jnp.zeros((MXU_TILE - HEAD_PAD, tq), BF16)

    def q_rhs(hd, qi):
        q_off = qi * tq if isinstance(qi, int) else pl.multiple_of(qi * tq, tq)
        qt = qt_ref[0, hd * HEAD_PAD:(hd + 1) * HEAD_PAD, pl.ds(q_off, tq)]
        return jnp.concatenate([qt, pad_q], axis=0)

    def k_lhs(hd, c, j):
        k_lo = c * tk + j * MXU_TILE
        kc = k_ref[0, k_lo:k_lo + MXU_TILE, hd * HEAD_PAD:(hd + 1) * HEAD_PAD]
        return jnp.concatenate([kc, pad_k], axis=1)

    def pop_scores(hd, j, slot, mx):
        st = pltpu.matmul_pop(SCORE_ACC[j % len(SCORE_ACC)], (MXU_TILE, tq), F32, hd)
        st_ref[slot][hd, j * MXU_TILE:(j + 1) * MXU_TILE, :] = st
        return jnp.maximum(mx, jnp.max(st, axis=0, keepdims=True))

    lag = len(SCORE_ACC) - 1

    @pl.when(i == 0)
    def _():
        for c in range(SCORE_AHEAD):
            mx = [jnp.full((1, tq), -jnp.inf, F32) for _ in range(heads)]
            for hd in range(heads):
                pltpu.matmul_push_rhs(q_rhs(hd, 0), 0, hd)
            for j in range(nt + lag):
                for hd in range(heads):
                    if j < nt:
                        pltpu.matmul_acc_lhs(SCORE_ACC[j % len(SCORE_ACC)], k_lhs(hd, c, j), hd,
                                             0 if j == 0 else None)
                    if j >= lag:
                        mx[hd] = pop_scores(hd, j - lag, c, mx[hd])
            for hd in range(heads):
                mc_ref[c][hd] = mx[hd]

    pending = []
    tile = 0
    block_accs = []

    def drain(keep):
        while len(pending) > keep:
            pending.pop(0)()

    for qsub in range(qsubs):
        qb = i * qsubs + qsub
        ms = [jnp.full((1, tq), -jnp.inf, F32) for _ in range(heads)]
        accs = [jnp.zeros((V_HEAD + DENOM_ROWS, tq), F32) for _ in range(heads)]
        block_accs.append(accs)
        for slot in range(n):
            ahead = slot + SCORE_AHEAD
            if ahead < n:
                qi, c_next = qb, ahead
            elif qsub + 1 < qsubs:
                qi, c_next = qb + 1, ahead - n
            else:
                qi, c_next = jnp.minimum(qb + 1, s // tq - 1), ahead - n
            fill = ahead % SCORE_SLOTS
            value_acc = VALUE_ACC[slot % len(VALUE_ACC)]
            m_old = list(ms)
            ms = [jnp.maximum(ms[hd], mc_ref[slot][hd]) for hd in range(heads)]
            mx = [jnp.full((1, tq), -jnp.inf, F32) for _ in range(heads)]
            for j in range(nt):
                score_acc = SCORE_ACC[tile % len(SCORE_ACC)]
                tile += 1
                for hd in range(heads):
                    rows = slice(j * MXU_TILE, (j + 1) * MXU_TILE)
                    p = jnp.exp2(st_ref[slot][hd, rows, :] - ms[hd]).astype(BF16)
                    pltpu.matmul_push_rhs(p, 1, hd)
                    pltpu.matmul_push_rhs(q_rhs(hd, qi), 0, hd)
                    v_lo = slot * tk + j * MXU_TILE
                    vc = vt_ref[0, hd * V_HEAD:(hd + 1) * V_HEAD, v_lo:v_lo + MXU_TILE]
                    pltpu.matmul_acc_lhs(value_acc, jnp.concatenate([vc, ones], axis=0), hd, 1)
                    pltpu.matmul_acc_lhs(score_acc, k_lhs(hd, c_next, j), hd, 0)

                def pop_tile(j=j, fill=fill, mx=mx, score_acc=score_acc, value_acc=value_acc,
                             m_old=m_old, m_new=list(ms), accs=accs):
                    for hd in range(heads):
                        st = pltpu.matmul_pop(score_acc, (MXU_TILE, tq), F32, hd)
                        st_ref[fill][hd, j * MXU_TILE:(j + 1) * MXU_TILE, :] = st
                        mx[hd] = jnp.maximum(mx[hd], jnp.max(st, axis=0, keepdims=True))
                        if j == nt - 1:
                            mc_ref[fill][hd] = mx[hd]
                            pv = pltpu.matmul_pop(value_acc, (V_HEAD + DENOM_ROWS, tq), F32, hd)
                            accs[hd] = jnp.exp2(m_old[hd] - m_new[hd]) * accs[hd] + pv

                drain(lag - 1)
                pending.append(pop_tile)
    drain(0)
    for qsub, accs in enumerate(block_accs):
        outs = [acc[:V_HEAD] / acc[V_HEAD:V_HEAD + 1] for acc in accs]
        o_ref[0, qsub * tq:(qsub + 1) * tq, :] = jnp.concatenate(outs, axis=0).T.astype(BF16)


def _attn(qt, k, vt, tq, tk, qsubs):
    b, _, s = qt.shape
    heads = LANES // V_HEAD
    groups = MLA_HEADS // heads
    assert s % (qsubs * tq) == 0 and s == SCORE_SLOTS * tk and tq == MXU_TILE and tk % MXU_TILE == 0
    return pl.pallas_call(
        functools.partial(_attn_kernel, tq=tq, tk=tk, heads=heads, qsubs=qsubs),
        grid=(b, groups, s // (qsubs * tq)),
        in_specs=[pl.BlockSpec((1, heads * HEAD_PAD, s), lambda bi, j, i: (bi, j, 0)),
                  pl.BlockSpec((1, s, heads * HEAD_PAD), lambda bi, j, i: (bi, 0, j)),
                  pl.BlockSpec((1, heads * V_HEAD, s), lambda bi, j, i: (bi, j, 0))],
        out_specs=pl.BlockSpec((1, qsubs * tq, heads * V_HEAD), lambda bi, j, i: (bi, i, j)),
        out_shape=jax.ShapeDtypeStruct((b, s, MLA_WIDTH), BF16),
        scratch_shapes=[pltpu.VMEM((SCORE_SLOTS, heads, tk + SCORE_PAD, tq), F32),
                        pltpu.VMEM((SCORE_SLOTS, heads, 1, tq), F32)],
        compiler_params=_params(("parallel", "parallel", "arbitrary")),
        name="attn",
    )(qt, k, vt)


def _dft1_kernel(z_ref, f1_ref, y_ref):
    groups, n1, r, c = z_ref.shape[2:]
    f1 = f1_ref[...].astype(BF16)
    flat = lambda ref, part, g: ref.at[0, part, g].reshape(n1 * r, c)
    for j in range(r):
        rows = pl.ds(j, n1, stride=r)
        gather = lambda part: jnp.concatenate(
            [flat(z_ref, part, g)[rows, :] for g in range(groups)], axis=1).astype(BF16)
        y = _dot(f1[:, :n1], gather(0)) + _dot(f1[:, n1:], gather(1))
        for g in range(groups):
            flat(y_ref, 0, g)[rows, :] = y[:n1, g * c:(g + 1) * c]
            flat(y_ref, 1, g)[rows, :] = y[n1:, g * c:(g + 1) * c]


def _dft1(z, f1, r):
    b, _, groups, n1, n2, c = z.shape
    spec = pl.BlockSpec((1, 2, groups, n1, r, c), lambda bi, j: (bi, 0, 0, 0, j, 0))
    return pl.pallas_call(
        _dft1_kernel,
        grid=(b, n2 // r),
        in_specs=[spec, _const_spec(f1.shape)],
        out_specs=spec,
        out_shape=jax.ShapeDtypeStruct(z.shape, F32),
        compiler_params=_params(("parallel", "parallel")),
        name="dft1",
    )(z, f1)


def _dft2_kernel(y_ref, w_ref, wf_ref, o_ref):
    groups, kb, n2, c = y_ref.shape[2:]
    for t in range(kb):
        w = w_ref[t].astype(BF16)
        part = lambda p: jnp.concatenate(
            [y_ref[0, p, g, t] for g in range(groups)], axis=1).astype(BF16)
        xr = (_dot(w[:, :n2], part(0)) + _dot(w[:, n2:], part(1))).astype(BF16)
        for g in range(groups):
            lo = t * groups * c + g * c
            o_ref[0, :, lo:lo + c] = _dot(xr[:, g * c:(g + 1) * c], wf_ref[g]).astype(BF16)


def _dft2(y, w, wf, kb):
    b, _, groups, n1, n2, c = y.shape
    return pl.pallas_call(
        _dft2_kernel,
        grid=(b, n1 // kb),
        in_specs=[pl.BlockSpec((1, 2, groups, kb, n2, c), lambda bi, j: (bi, 0, 0, j, 0, 0)),
                  pl.BlockSpec((kb, n2, 2 * n2), lambda bi, j: (j, 0, 0)),
                  _const_spec(wf.shape)],
        out_specs=pl.BlockSpec((1, n2, kb * groups * c), lambda bi, j: (bi, 0, j)),
        out_shape=jax.ShapeDtypeStruct((b, n2, n1 * groups * c), BF16),
        compiler_params=_params(("parallel", "parallel")),
        name="dft2",
    )(y, w, wf)


HALO = 16


def _tail_kernel(a_ref, ap_ref, an_ref, f_ref, fp_ref, fn_ref, x_ref, xp_ref, xn_ref,
                 woa_ref, wof_ref, gpm_ref, gpf_ref, wg_ref, wu_ref, cw_ref, cb_ref, wd_ref,
                 gpo_ref, y_ref, act_ref, *, ffc):
    i = pl.program_id(1)
    tm = x_ref.shape[1]
    rows = tm + 2 * HALO
    ext = lambda p, c, n: jnp.concatenate([p[0], c[0], n[0]], axis=0)
    mix = _dot(ext(ap_ref, a_ref, an_ref), woa_ref[...]) + _dot(ext(fp_ref, f_ref, fn_ref), wof_ref[...])
    x1 = ext(xp_ref, x_ref, xn_ref) + _rms(mix, gpm_ref[...])
    row = jax.lax.broadcasted_iota(jnp.int32, (rows, 1), 0)
    inside = ((row >= HALO) | (i > 0)) & ((row < HALO + tm) | (i < pl.num_programs(1) - 1))
    hext = jnp.where(inside, _rms(x1, gpf_ref[...]), 0.0).astype(BF16)
    h = hext[HALO:HALO + tm]
    for c in range(D_FF // ffc):
        sl = slice(c * ffc, (c + 1) * ffc)
        g = _dot(hext, wg_ref[:, sl])
        u = _dot(h, wu_ref[:, sl])
        g_prev = pltpu.roll(g, 1, 0)[HALO:HALO + tm]
        g_next = pltpu.roll(g, rows - 1, 0)[HALO:HALO + tm]
        gate = (cb_ref[:, sl] + g_prev * cw_ref[0:1, sl] + g[HALO:HALO + tm] * cw_ref[1:2, sl]
                + g_next * cw_ref[2:3, sl])
        inner = math.sqrt(2.0 / math.pi) * (gate + 0.044715 * (gate * gate * gate))
        act = 0.5 * gate * (1.0 + jnp.tanh(inner)) * u
        act_ref[:, sl] = act.astype(BF16)
    out = _dot(act_ref[...], wd_ref[...])
    y_ref[0] = x1[HALO:HALO + tm] + _rms(out, gpo_ref[...])


def _tail(a, f, x, woa, wof, gpm, gpf, wg, wu, cw, cb, wd, gpo, tm, ffc):
    b, s, _ = x.shape
    per = tm // HALO
    last = s // HALO - 1
    single = lambda shape: pl.BlockSpec(shape, lambda *_: (0,) * len(shape),
                                        pipeline_mode=pl.Buffered(1))

    def with_halo(width):
        return [pl.BlockSpec((1, tm, width), lambda bi, i: (bi, i, 0)),
                pl.BlockSpec((1, HALO, width), lambda bi, i: (bi, jnp.maximum(i * per - 1, 0), 0)),
                pl.BlockSpec((1, HALO, width), lambda bi, i: (bi, jnp.minimum((i + 1) * per, last), 0))]

    return pl.pallas_call(
        functools.partial(_tail_kernel, ffc=ffc),
        grid=(b, s // tm),
        in_specs=with_halo(MLA_WIDTH) + with_halo(FNET_WIDTH) + with_halo(D_MODEL) + [
            single(woa.shape), single(wof.shape), _const_spec(gpm.shape), _const_spec(gpf.shape),
            single(wg.shape), single(wu.shape), _const_spec(cw.shape), _const_spec(cb.shape),
            single(wd.shape), _const_spec(gpo.shape)],
        out_specs=pl.BlockSpec((1, tm, D_MODEL), lambda bi, i: (bi, i, 0)),
        out_shape=jax.ShapeDtypeStruct(x.shape, F32),
        scratch_shapes=[pltpu.VMEM((tm, D_FF), BF16)],
        compiler_params=_params(("parallel", "arbitrary")),
        name="tail",
    )(a, a, a, f, f, f, x, x, x, woa, wof, gpm, gpf, wg, wu, cw, cb, wd, gpo)


def _rope_tables(s):
    ang = np.arange(s, dtype=np.float64)[:, None] * (
        ROPE_THETA ** (-np.arange(0, QK_ROPE, 2, dtype=np.float64) / QK_ROPE))[None, :]
    cos, sin = np.cos(ang), np.sin(ang)
    zeros = np.zeros((s, QK_NOPE))
    zh = np.zeros((s, HALF_ROPE))
    tail = np.zeros((s, HEAD_PAD - QK_NOPE - QK_ROPE))
    cos_t = np.concatenate([np.ones((s, QK_NOPE)), cos, cos, tail], axis=1)
    sina = np.concatenate([zeros, zh, sin, tail], axis=1)
    sinb = np.concatenate([zeros, -sin, zh, tail], axis=1)
    return tuple(jnp.asarray(t, F32) for t in (cos_t, sina, sinb))


def _angles(num, den):
    return (2.0 * np.pi / den) * (num % den).astype(np.float64)


def _dft_tables(s):
    n2 = DFT_N2
    n1 = s // n2
    c = np.arange(FNET_GROUP_DIM, dtype=np.int64)
    ang = _angles(c[:, None] * c[None, :], FNET_GROUP_DIM)
    cs = np.concatenate([np.cos(ang), np.sin(ang)], axis=1) / math.sqrt(FNET_GROUP_DIM)
    k1 = np.arange(n1, dtype=np.int64)
    a1 = _angles(k1[:, None] * k1[None, :], n1)
    c1, s1 = np.cos(a1), np.sin(a1)
    f1 = np.concatenate([np.concatenate([c1, -s1], axis=1),
                         np.concatenate([s1, c1], axis=1)], axis=0) / math.sqrt(n1)
    k2 = np.arange(n2, dtype=np.int64)
    kk = k1[:, None, None] + n1 * k2[None, :, None]
    a2 = _angles(kk * k2[None, None, :], s)
    w = np.concatenate([np.cos(a2), -np.sin(a2)], axis=2) / math.sqrt(n2)
    return tuple(jnp.asarray(t, F32) for t in (cs, f1, w))


def _prep_weights(g_pre_mix, w_in, g_q, w_uq, g_kv, w_ukv, w_fnet, w_out, g_post_mix,
                  g_pre_ffn, w_gate, w_up, conv_w, conv_b, w_down, g_post_ffn):
    win = w_in[0]
    f_lo = Q_LORA + KV_LORA + QK_ROPE
    kr_cols = jnp.concatenate([jnp.zeros((D_MODEL, QK_NOPE), F32),
                               win[:, Q_LORA + KV_LORA:f_lo],
                               jnp.zeros((D_MODEL, HEAD_PAD - QK_NOPE - QK_ROPE), F32)], axis=1)
    win_p = jnp.concatenate([win[:, :Q_LORA + KV_LORA], win[:, f_lo:], kr_cols], axis=1)
    wuq = w_uq[0].reshape(Q_LORA, MLA_HEADS, QK_NOPE + QK_ROPE)
    wuq = jnp.pad(wuq, ((0, 0), (0, 0), (0, HEAD_PAD - QK_NOPE - QK_ROPE)))
    wuq = wuq.reshape(Q_LORA, MLA_HEADS * HEAD_PAD)
    wukv = w_ukv[0].reshape(KV_LORA, MLA_HEADS, QK_NOPE + V_HEAD)
    wuk = jnp.pad(wukv[..., :QK_NOPE], ((0, 0), (0, 0), (0, HEAD_PAD - QK_NOPE)))
    wukv = jnp.concatenate([wuk.reshape(KV_LORA, MLA_HEADS * HEAD_PAD),
                            wukv[..., QK_NOPE:].reshape(KV_LORA, MLA_WIDTH)], axis=1)
    return dict(
        gpre=g_pre_mix, win=win_p.astype(BF16), gq=g_q, wuq=wuq.astype(BF16), gkv=g_kv,
        wukv=wukv.astype(BF16), wf=w_fnet[0].astype(BF16),
        woa=w_out[0, :MLA_WIDTH].astype(BF16), wof=w_out[0, MLA_WIDTH:].astype(BF16),
        gpm=g_post_mix, gpf=g_pre_ffn, wg=w_gate[0].astype(BF16), wu=w_up[0].astype(BF16),
        cw=conv_w[0], cb=conv_b, wd=w_down[0].astype(BF16), gpo=g_post_ffn)


def _trunk(x, w):
    b, s, _ = x.shape
    n2 = DFT_N2
    n1 = s // n2
    cos, sina, sinb = _rope_tables(s)
    cs, f1, wtab = _dft_tables(s)
    qt, k, vt, z = _proj(x, w["gpre"], w["win"], w["gq"], w["wuq"], w["gkv"], w["wukv"], cs,
                         cos, sina, sinb, tm=512)
    a = _attn(qt, k, vt, tq=MXU_TILE, tk=s // SCORE_SLOTS, qsubs=ATTN_STEP_TILES // (s // MXU_TILE))
    y = _dft1(z.reshape(b, 2, FNET_GROUPS, n1, n2, FNET_GROUP_DIM), f1, r=16)
    f = _dft2(y, wtab, w["wf"], kb=8)
    f = f.reshape(b, s, FNET_WIDTH)
    return _tail(a, f, x, w["woa"], w["wof"], w["gpm"], w["gpf"], w["wg"], w["wu"], w["cw"],
                 w["cb"], w["wd"], w["gpo"], tm=512, ffc=256)


def kernel(x_prompt, x_sample, g_pre_mix, w_in, g_q, w_uq, g_kv, w_ukv, w_fnet, w_out,
           g_post_mix, g_pre_ffn, w_gate, w_up, conv_w, conv_b, w_down, g_post_ffn):
    w = _prep_weights(g_pre_mix, w_in, g_q, w_uq, g_kv, w_ukv, w_fnet, w_out, g_post_mix,
                      g_pre_ffn, w_gate, w_up, conv_w, conv_b, w_down, g_post_ffn)
    return _trunk(x_prompt, w), _trunk(x_sample, w)
```

```python
import functools
import math

import jax
import jax.numpy as jnp
import numpy as np
from jax.experimental import pallas as pl
from jax.experimental.pallas import tpu as pltpu

D_MODEL = 1024
MLA_HEADS = 8
QK_NOPE = 64
QK_ROPE = 32
V_HEAD = 64
MLA_WIDTH = MLA_HEADS * V_HEAD
Q_LORA = 256
KV_LORA = 256
FNET_GROUPS = 4
FNET_WIDTH = 512
FNET_GROUP_DIM = 128
D_FF = 2816
ROPE_THETA = 10000.0
RMS_EPS = 1e-6
ATTN_SCALE = 1.0 / math.sqrt(QK_NOPE + QK_ROPE)
Q_SCALE = ATTN_SCALE * math.log2(math.e)

LANES = 128
HEAD_PAD = 128
HALF_ROPE = QK_ROPE // 2
DENOM_ROWS = 16
SCORE_PAD = 8
MXU_TILE = 256
SCORE_ACC = (0, 64, 128)
VALUE_ACC = (192, 224)
SCORE_AHEAD = 2
SCORE_SLOTS = 4
ATTN_STEP_TILES = 64
DFT_N2 = 128
VMEM_LIMIT = 56 * 1024 * 1024

BF16 = jnp.bfloat16
F32 = jnp.float32


def _rms(x, g):
    return x * jax.lax.rsqrt(jnp.mean(x * x, axis=-1, keepdims=True) + RMS_EPS) * g


def _dot(a, b):
    return jnp.dot(a, b, preferred_element_type=F32)


def _params(sem):
    return pltpu.CompilerParams(dimension_semantics=sem, vmem_limit_bytes=VMEM_LIMIT)


def _const_spec(shape):
    zeros = (0,) * len(shape)
    return pl.BlockSpec(shape, lambda *_: zeros)


def _proj_kernel(x_ref, gpre_ref, win_ref, gq_ref, wuqt_ref, gkv_ref, wuk_ref, wuvt_ref, cs_ref,
                 cos_ref, sina_ref, sinb_ref, cost_ref, sint_ref, qt_ref, k_ref, vt_ref, z_ref):
    tm = x_ref.shape[1]
    h = _rms(x_ref[0], gpre_ref[...]).astype(BF16)
    p = _dot(h, win_ref[...])
    cq = _rms(p[:, :Q_LORA], gq_ref[...])
    qt = _dot(wuqt_ref[...], cq.T.astype(BF16))
    ckv = _rms(p[:, Q_LORA:Q_LORA + KV_LORA], gkv_ref[...])
    kn = _dot(ckv.astype(BF16), wuk_ref[...])
    vt_ref[0] = _dot(wuvt_ref[...], ckv.T.astype(BF16)).astype(BF16)
    cos, sina, sinb = cos_ref[...], sina_ref[...], sinb_ref[...]
    t = p[:, 2 * FNET_WIDTH:]
    kr = (t * cos + pltpu.roll(t, HALF_ROPE, 1) * sina
          + pltpu.roll(t, HEAD_PAD - HALF_ROPE, 1) * sinb)
    cos_t, sin_t = cost_ref[...], sint_ref[...]
    lo, mid, hi = QK_NOPE, QK_NOPE + HALF_ROPE, QK_NOPE + QK_ROPE
    for hd in range(MLA_HEADS):
        base = hd * HEAD_PAD
        x1, x2 = qt[base + lo:base + mid], qt[base + mid:base + hi]
        qt_ref[0, base:base + lo, :] = (qt[base:base + lo] * Q_SCALE).astype(BF16)
        qt_ref[0, base + lo:base + mid, :] = ((x1 * cos_t - x2 * sin_t) * Q_SCALE).astype(BF16)
        qt_ref[0, base + mid:base + hi, :] = ((x2 * cos_t + x1 * sin_t) * Q_SCALE).astype(BF16)
        qt_ref[0, base + hi:base + HEAD_PAD, :] = jnp.zeros((HEAD_PAD - hi, tm), BF16)
        sl = slice(base, base + HEAD_PAD)
        k_ref[0, :, sl] = (kn[:, sl] + kr).astype(BF16)
    f = p[:, Q_LORA + KV_LORA:2 * FNET_WIDTH].astype(BF16)
    cs = cs_ref[...].astype(BF16)
    for g in range(FNET_GROUPS):
        sl = slice(g * FNET_GROUP_DIM, (g + 1) * FNET_GROUP_DIM)
        ab = _dot(f[:, sl], cs)
        z_ref[0, 0, g] = ab[:, :FNET_GROUP_DIM]
        z_ref[0, 1, g] = ab[:, FNET_GROUP_DIM:]


def _proj(x, gpre, win, gq, wuqt, gkv, wuk, wuvt, cs, rope_tabs, tm):
    b, s, _ = x.shape
    wide = MLA_HEADS * HEAD_PAD
    tok = lambda w: pl.BlockSpec((1, tm, w), lambda bi, i: (bi, i, 0))
    tab = pl.BlockSpec((tm, HEAD_PAD), lambda bi, i: (i, 0))
    tab_t = pl.BlockSpec((HALF_ROPE, tm), lambda bi, i: (0, i))
    return pl.pallas_call(
        _proj_kernel,
        grid=(b, s // tm),
        in_specs=[tok(D_MODEL), _const_spec(gpre.shape), _const_spec(win.shape),
                  _const_spec(gq.shape), _const_spec(wuqt.shape), _const_spec(gkv.shape),
                  _const_spec(wuk.shape), _const_spec(wuvt.shape), _const_spec(cs.shape),
                  tab, tab, tab, tab_t, tab_t],
        out_specs=[pl.BlockSpec((1, wide, tm), lambda bi, i: (bi, 0, i)),
                   tok(wide),
                   pl.BlockSpec((1, MLA_WIDTH, tm), lambda bi, i: (bi, 0, i)),
                   pl.BlockSpec((1, 2, FNET_GROUPS, tm, FNET_GROUP_DIM),
                                lambda bi, i: (bi, 0, 0, i, 0))],
        out_shape=[jax.ShapeDtypeStruct((b, wide, s), BF16),
                   jax.ShapeDtypeStruct((b, s, wide), BF16),
                   jax.ShapeDtypeStruct((b, MLA_WIDTH, s), BF16),
                   jax.ShapeDtypeStruct((b, 2, FNET_GROUPS, s, FNET_GROUP_DIM), F32)],
        compiler_params=_params(("parallel", "parallel")),
        name="proj",
    )(x, gpre, win, gq, wuqt, gkv, wuk, wuvt, cs, *rope_tabs)


def _attn_kernel(qt_ref, k_ref, vt_ref, o_ref, st_all, mc_all, *, tq, tk, heads, qsubs):
    st_ref = tuple(st_all.at[k] for k in range(SCORE_SLOTS))
    mc_ref = tuple(mc_all.at[k] for k in range(SCORE_SLOTS))
    s = k_ref.shape[1]
    n = s // tk
    nt = tk // MXU_TILE
    i = pl.program_id(2)
    ones = jnp.ones((DENOM_ROWS, MXU_TILE), BF16)
    pad_k = jnp.zeros((MXU_TILE, MXU_TILE - HEAD_PAD), BF16)
    pad_q = jnp.zeros((MXU_TILE - HEAD_PAD, tq), BF16)

    def q_rhs(hd, qi):
        q_off = qi * tq if isinstance(qi, int) else pl.multiple_of(qi * tq, tq)
        qt = qt_ref[0, hd * HEAD_PAD:(hd + 1) * HEAD_PAD, pl.ds(q_off, tq)]
        return jnp.concatenate([qt, pad_q], axis=0)

    def k_lhs(hd, c, j):
        k_lo = c * tk + j * MXU_TILE
        kc = k_ref[0, k_lo:k_lo + MXU_TILE, hd * HEAD_PAD:(hd + 1) * HEAD_PAD]
        return jnp.concatenate([kc, pad_k], axis=1)

    def pop_scores(hd, j, slot, mx):
        st = pltpu.matmul_pop(SCORE_ACC[j % len(SCORE_ACC)], (MXU_TILE, tq), F32, hd)
        st_ref[slot][hd, j * MXU_TILE:(j + 1) * MXU_TILE, :] = st
        return jnp.maximum(mx, jnp.max(st, axis=0, keepdims=True))

    lag = len(SCORE_ACC) - 1

    @pl.when(i == 0)
    def _():
        for c in range(SCORE_AHEAD):
            mx = [jnp.full((1, tq), -jnp.inf, F32) for _ in range(heads)]
            for hd in range(heads):
                pltpu.matmul_push_rhs(q_rhs(hd, 0), 0, hd)
            for j in range(nt + lag):
                for hd in range(heads):
                    if j < nt:
                        pltpu.matmul_acc_lhs(SCORE_ACC[j % len(SCORE_ACC)], k_lhs(hd, c, j), hd,
                                             0 if j == 0 else None)
                    if j >= lag:
                        mx[hd] = pop_scores(hd, j - lag, c, mx[hd])
            for hd in range(heads):
                mc_ref[c][hd] = mx[hd]

    pending = []
    tile = 0
    block_accs = []

    def drain(keep):
        while len(pending) > keep:
            pending.pop(0)()

    for qsub in range(qsubs):
        qb = i * qsubs + qsub
        ms = [jnp.full((1, tq), -jnp.inf, F32) for _ in range(heads)]
        accs = [jnp.zeros((V_HEAD + DENOM_ROWS, tq), F32) for _ in range(heads)]
        block_accs.append(accs)
        for slot in range(n):
            ahead = slot + SCORE_AHEAD
            if ahead < n:
                qi, c_next = qb, ahead
            elif qsub + 1 < qsubs:
                qi, c_next = qb + 1, ahead - n
            else:
                qi, c_next = jnp.minimum(qb + 1, s // tq - 1), ahead - n
            fill = ahead % SCORE_SLOTS
            value_acc = VALUE_ACC[slot % len(VALUE_ACC)]
            m_old = list(ms)
            ms = [jnp.maximum(ms[hd], mc_ref[slot][hd]) for hd in range(heads)]
            mx = [jnp.full((1, tq), -jnp.inf, F32) for _ in range(heads)]
            for j in range(nt):
                score_acc = SCORE_ACC[tile % len(SCORE_ACC)]
                tile += 1
                for hd in range(heads):
                    rows = slice(j * MXU_TILE, (j + 1) * MXU_TILE)
                    p = jnp.exp2(st_ref[slot][hd, rows, :] - ms[hd]).astype(BF16)
                    pltpu.matmul_push_rhs(p, 1, hd)
                    pltpu.matmul_push_rhs(q_rhs(hd, qi), 0, hd)
                    v_lo = slot * tk + j * MXU_TILE
                    vc = vt_ref[0, hd * V_HEAD:(hd + 1) * V_HEAD, v_lo:v_lo + MXU_TILE]
                    pltpu.matmul_acc_lhs(value_acc, jnp.concatenate([vc, ones], axis=0), hd, 1)
                    pltpu.matmul_acc_lhs(score_acc, k_lhs(hd, c_next, j), hd, 0)

                def pop_tile(j=j, fill=fill, mx=mx, score_acc=score_acc, value_acc=value_acc,
                             m_old=m_old, m_new=list(ms), accs=accs):
                    for hd in range(heads):
                        st = pltpu.matmul_pop(score_acc, (MXU_TILE, tq), F32, hd)
                        st_ref[fill][hd, j * MXU_TILE:(j + 1) * MXU_TILE, :] = st
                        mx[hd] = jnp.maximum(mx[hd], jnp.max(st, axis=0, keepdims=True))
                        if j == nt - 1:
                            mc_ref[fill][hd] = mx[hd]
                            pv = pltpu.matmul_pop(value_acc, (V_HEAD + DENOM_ROWS, tq), F32, hd)
                            accs[hd] = jnp.exp2(m_old[hd] - m_new[hd]) * accs[hd] + pv

                drain(lag - 1)
                pending.append(pop_tile)
    drain(0)
    for qsub, accs in enumerate(block_accs):
        outs = [acc[:V_HEAD] / acc[V_HEAD:V_HEAD + 1] for acc in accs]
        o_ref[0, qsub * tq:(qsub + 1) * tq, :] = jnp.concatenate(outs, axis=0).T.astype(BF16)


def _attn(qt, k, vt, tq, tk, qsubs):
    b, _, s = qt.shape
    heads = LANES // V_HEAD
    groups = MLA_HEADS // heads
    assert s % (qsubs * tq) == 0 and s == SCORE_SLOTS * tk and tq == MXU_TILE and tk % MXU_TILE == 0
    return pl.pallas_call(
        functools.partial(_attn_kernel, tq=tq, tk=tk, heads=heads, qsubs=qsubs),
        grid=(b, groups, s // (qsubs * tq)),
        in_specs=[pl.BlockSpec((1, heads * HEAD_PAD, s), lambda bi, j, i: (bi, j, 0)),
                  pl.BlockSpec((1, s, heads * HEAD_PAD), lambda bi, j, i: (bi, 0, j)),
                  pl.BlockSpec((1, heads * V_HEAD, s), lambda bi, j, i: (bi, j, 0))],
        out_specs=pl.BlockSpec((1, qsubs * tq, heads * V_HEAD), lambda bi, j, i: (bi, i, j)),
        out_shape=jax.ShapeDtypeStruct((b, s, MLA_WIDTH), BF16),
        scratch_shapes=[pltpu.VMEM((SCORE_SLOTS, heads, tk + SCORE_PAD, tq), F32),
                        pltpu.VMEM((SCORE_SLOTS, heads, 1, tq), F32)],
        compiler_params=_params(("parallel", "parallel", "arbitrary")),
        name="attn",
    )(qt, k, vt)


def _dft1_kernel(z_ref, f1_ref, y_ref):
    groups, n1, r, c = z_ref.shape[2:]
    f1 = f1_ref[...].astype(BF16)
    flat = lambda ref, part, g: ref.at[0, part, g].reshape(n1 * r, c)
    for j in range(r):
        rows = pl.ds(j, n1, stride=r)
        gather = lambda part: jnp.concatenate(
            [flat(z_ref, part, g)[rows, :] for g in range(groups)], axis=1).astype(BF16)
        y = _dot(f1[:, :n1], gather(0)) + _dot(f1[:, n1:], gather(1))
        for g in range(groups):
            flat(y_ref, 0, g)[rows, :] = y[:n1, g * c:(g + 1) * c]
            flat(y_ref, 1, g)[rows, :] = y[n1:, g * c:(g + 1) * c]


def _dft1(z, f1, r):
    b, _, groups, n1, n2, c = z.shape
    spec = pl.BlockSpec((1, 2, groups, n1, r, c), lambda bi, j: (bi, 0, 0, 0, j, 0))
    return pl.pallas_call(
        _dft1_kernel,
        grid=(b, n2 // r),
        in_specs=[spec, _const_spec(f1.shape)],
        out_specs=spec,
        out_shape=jax.ShapeDtypeStruct(z.shape, F32),
        compiler_params=_params(("parallel", "parallel")),
        name="dft1",
    )(z, f1)


def _dft2_kernel(y_ref, w_ref, wf_ref, o_ref):
    groups, kb, n2, c = y_ref.shape[2:]
    for t in range(kb):
        w = w_ref[t].astype(BF16)
        part = lambda p: jnp.concatenate(
            [y_ref[0, p, g, t] for g in range(groups)], axis=1).astype(BF16)
        xr = (_dot(w[:, :n2], part(0)) + _dot(w[:, n2:], part(1))).astype(BF16)
        for g in range(groups):
            lo = t * groups * c + g * c
            o_ref[0, :, lo:lo + c] = _dot(xr[:, g * c:(g + 1) * c], wf_ref[g]).astype(BF16)


def _dft2(y, w, wf, kb):
    b, _, groups, n1, n2, c = y.shape
    return pl.pallas_call(
        _dft2_kernel,
        grid=(b, n1 // kb),
        in_specs=[pl.BlockSpec((1, 2, groups, kb, n2, c), lambda bi, j: (bi, 0, 0, j, 0, 0)),
                  pl.BlockSpec((kb, n2, 2 * n2), lambda bi, j: (j, 0, 0)),
                  _const_spec(wf.shape)],
        out_specs=pl.BlockSpec((1, n2, kb * groups * c), lambda bi, j: (bi, 0, j)),
        out_shape=jax.ShapeDtypeStruct((b, n2, n1 * groups * c), BF16),
        compiler_params=_params(("parallel", "parallel")),
        name="dft2",
    )(y, w, wf)


HALO = 16


def _tail_kernel(a_ref, ap_ref, an_ref, f_ref, fp_ref, fn_ref, x_ref, xp_ref, xn_ref,
                 woa_ref, wof_ref, gpm_ref, gpf_ref, wg_ref, wu_ref, cw_ref, cb_ref, wd_ref,
                 gpo_ref, y_ref, act_ref, *, ffc):
    i = pl.program_id(1)
    tm = x_ref.shape[1]
    rows = tm + 2 * HALO
    ext = lambda p, c, n: jnp.concatenate([p[0], c[0], n[0]], axis=0)
    mix = _dot(ext(ap_ref, a_ref, an_ref), woa_ref[...]) + _dot(ext(fp_ref, f_ref, fn_ref), wof_ref[...])
    x1 = ext(xp_ref, x_ref, xn_ref) + _rms(mix, gpm_ref[...])
    row = jax.lax.broadcasted_iota(jnp.int32, (rows, 1), 0)
    inside = ((row >= HALO) | (i > 0)) & ((row < HALO + tm) | (i < pl.num_programs(1) - 1))
    hext = jnp.where(inside, _rms(x1, gpf_ref[...]), 0.0).astype(BF16)
    h = hext[HALO:HALO + tm]
    for c in range(D_FF // ffc):
        sl = slice(c * ffc, (c + 1) * ffc)
        g = _dot(hext, wg_ref[:, sl])
        u = _dot(h, wu_ref[:, sl])
        g_prev = pltpu.roll(g, 1, 0)[HALO:HALO + tm]
        g_next = pltpu.roll(g, rows - 1, 0)[HALO:HALO + tm]
        gate = (cb_ref[:, sl] + g_prev * cw_ref[0:1, sl] + g[HALO:HALO + tm] * cw_ref[1:2, sl]
                + g_next * cw_ref[2:3, sl])
        inner = math.sqrt(2.0 / math.pi) * (gate + 0.044715 * (gate * gate * gate))
        act = 0.5 * gate * (1.0 + jnp.tanh(inner)) * u
        act_ref[:, sl] = act.astype(BF16)
    out = _dot(act_ref[...], wd_ref[...])
    y_ref[0] = x1[HALO:HALO + tm] + _rms(out, gpo_ref[...])


def _tail(a, f, x, woa, wof, gpm, gpf, wg, wu, cw, cb, wd, gpo, tm, ffc):
    b, s, _ = x.shape
    per = tm // HALO
    last = s // HALO - 1
    single = lambda shape: pl.BlockSpec(shape, lambda *_: (0,) * len(shape),
                                        pipeline_mode=pl.Buffered(1))

    def with_halo(width):
        return [pl.BlockSpec((1, tm, width), lambda bi, i: (bi, i, 0)),
                pl.BlockSpec((1, HALO, width), lambda bi, i: (bi, jnp.maximum(i * per - 1, 0), 0)),
                pl.BlockSpec((1, HALO, width), lambda bi, i: (bi, jnp.minimum((i + 1) * per, last), 0))]

    return pl.pallas_call(
        functools.partial(_tail_kernel, ffc=ffc),
        grid=(b, s // tm),
        in_specs=with_halo(MLA_WIDTH) + with_halo(FNET_WIDTH) + with_halo(D_MODEL) + [
            single(woa.shape), single(wof.shape), _const_spec(gpm.shape), _const_spec(gpf.shape),
            single(wg.shape), single(wu.shape), _const_spec(cw.shape), _const_spec(cb.shape),
            single(wd.shape), _const_spec(gpo.shape)],
        out_specs=pl.BlockSpec((1, tm, D_MODEL), lambda bi, i: (bi, i, 0)),
        out_shape=jax.ShapeDtypeStruct(x.shape, F32),
        scratch_shapes=[pltpu.VMEM((tm, D_FF), BF16)],
        compiler_params=_params(("parallel", "arbitrary")),
        name="tail",
    )(a, a, a, f, f, f, x, x, x, woa, wof, gpm, gpf, wg, wu, cw, cb, wd, gpo)


def _rope_tables(s):
    ang = np.arange(s, dtype=np.float64)[:, None] * (
        ROPE_THETA ** (-np.arange(0, QK_ROPE, 2, dtype=np.float64) / QK_ROPE))[None, :]
    cos, sin = np.cos(ang), np.sin(ang)
    zeros = np.zeros((s, QK_NOPE))
    zh = np.zeros((s, HALF_ROPE))
    tail = np.zeros((s, HEAD_PAD - QK_NOPE - QK_ROPE))
    cos_t = np.concatenate([np.ones((s, QK_NOPE)), cos, cos, tail], axis=1)
    sina = np.concatenate([zeros, zh, sin, tail], axis=1)
    sinb = np.concatenate([zeros, -sin, zh, tail], axis=1)
    return tuple(jnp.asarray(t, F32) for t in (cos_t, sina, sinb, cos.T, sin.T))


def _angles(num, den):
    return (2.0 * np.pi / den) * (num % den).astype(np.float64)


def _dft_tables(s):
    n2 = DFT_N2
    n1 = s // n2
    c = np.arange(FNET_GROUP_DIM, dtype=np.int64)
    ang = _angles(c[:, None] * c[None, :], FNET_GROUP_DIM)
    cs = np.concatenate([np.cos(ang), np.sin(ang)], axis=1) / math.sqrt(FNET_GROUP_DIM)
    k1 = np.arange(n1, dtype=np.int64)
    a1 = _angles(k1[:, None] * k1[None, :], n1)
    c1, s1 = np.cos(a1), np.sin(a1)
    f1 = np.concatenate([np.concatenate([c1, -s1], axis=1),
                         np.concatenate([s1, c1], axis=1)], axis=0) / math.sqrt(n1)
    k2 = np.arange(n2, dtype=np.int64)
    kk = k1[:, None, None] + n1 * k2[None, :, None]
    a2 = _angles(kk * k2[None, None, :], s)
    w = np.concatenate([np.cos(a2), -np.sin(a2)], axis=2) / math.sqrt(n2)
    return tuple(jnp.asarray(t, F32) for t in (cs, f1, w))


def _prep_weights(g_pre_mix, w_in, g_q, w_uq, g_kv, w_ukv, w_fnet, w_out, g_post_mix,
                  g_pre_ffn, w_gate, w_up, conv_w, conv_b, w_down, g_post_ffn):
    win = w_in[0]
    f_lo = Q_LORA + KV_LORA + QK_ROPE
    kr_cols = jnp.concatenate([jnp.zeros((D_MODEL, QK_NOPE), F32),
                               win[:, Q_LORA + KV_LORA:f_lo],
                               jnp.zeros((D_MODEL, HEAD_PAD - QK_NOPE - QK_ROPE), F32)], axis=1)
    win_p = jnp.concatenate([win[:, :Q_LORA + KV_LORA], win[:, f_lo:], kr_cols], axis=1)
    wuq = w_uq[0].reshape(Q_LORA, MLA_HEADS, QK_NOPE + QK_ROPE)
    wuq = jnp.pad(wuq, ((0, 0), (0, 0), (0, HEAD_PAD - QK_NOPE - QK_ROPE)))
    wuq = wuq.reshape(Q_LORA, MLA_HEADS * HEAD_PAD)
    wukv = w_ukv[0].reshape(KV_LORA, MLA_HEADS, QK_NOPE + V_HEAD)
    wuk = jnp.pad(wukv[..., :QK_NOPE], ((0, 0), (0, 0), (0, HEAD_PAD - QK_NOPE)))
    wuk = wuk.reshape(KV_LORA, MLA_HEADS * HEAD_PAD)
    wuv = wukv[..., QK_NOPE:].reshape(KV_LORA, MLA_WIDTH)
    return dict(
        gpre=g_pre_mix, win=win_p.astype(BF16), gq=g_q, wuqt=wuq.T.astype(BF16), gkv=g_kv,
        wuk=wuk.astype(BF16), wuvt=wuv.T.astype(BF16), wf=w_fnet[0].astype(BF16),
        woa=w_out[0, :MLA_WIDTH].astype(BF16), wof=w_out[0, MLA_WIDTH:].astype(BF16),
        gpm=g_post_mix, gpf=g_pre_ffn, wg=w_gate[0].astype(BF16), wu=w_up[0].astype(BF16),
        cw=conv_w[0], cb=conv_b, wd=w_down[0].astype(BF16), gpo=g_post_ffn)


def _trunk(x, w):
    b, s, _ = x.shape
    n2 = DFT_N2
    n1 = s // n2
    rope_tabs = _rope_tables(s)
    cs, f1, wtab = _dft_tables(s)
    qt, k, vt, z = _proj(x, w["gpre"], w["win"], w["gq"], w["wuqt"], w["gkv"], w["wuk"], w["wuvt"],
                         cs, rope_tabs, tm=512)
    a = _attn(qt, k, vt, tq=MXU_TILE, tk=s // SCORE_SLOTS, qsubs=ATTN_STEP_TILES // (s // MXU_TILE))
    y = _dft1(z.reshape(b, 2, FNET_GROUPS, n1, n2, FNET_GROUP_DIM), f1, r=16)
    f = _dft2(y, wtab, w["wf"], kb=8)
    f = f.reshape(b, s, FNET_WIDTH)
    return _tail(a, f, x, w["woa"], w["wof"], w["gpm"], w["gpf"], w["wg"], w["wu"], w["cw"],
                 w["cb"], w["wd"], w["gpo"], tm=512, ffc=256)


def kernel(x_prompt, x_sample, g_pre_mix, w_in, g_q, w_uq, g_kv, w_ukv, w_fnet, w_out,
           g_post_mix, g_pre_ffn, w_gate, w_up, conv_w, conv_b, w_down, g_post_ffn):
    w = _prep_weights(g_pre_mix, w_in, g_q, w_uq, g_kv, w_ukv, w_fnet, w_out, g_post_mix,
                      g_pre_ffn, w_gate, w_up, conv_w, conv_b, w_down, g_post_ffn)
    return _trunk(x_prompt, w), _trunk(x_sample, w)
```

```python
import functools
import math

import jax
import jax.numpy as jnp
import numpy as np
from jax.experimental import pallas as pl
from jax.experimental.pallas import tpu as pltpu

D_MODEL = 1024
MLA_HEADS = 8
QK_NOPE = 64
QK_ROPE = 32
V_HEAD = 64
MLA_WIDTH = MLA_HEADS * V_HEAD
Q_LORA = 256
KV_LORA = 256
FNET_GROUPS = 4
FNET_WIDTH = 512
FNET_GROUP_DIM = 128
D_FF = 2816
ROPE_THETA = 10000.0
RMS_EPS = 1e-6
ATTN_SCALE = 1.0 / math.sqrt(QK_NOPE + QK_ROPE)
Q_SCALE = ATTN_SCALE * math.log2(math.e)

LANES = 128
HEAD_PAD = 128
HALF_ROPE = QK_ROPE // 2
DENOM_ROWS = 16
SCORE_PAD = 8
MXU_TILE = 256
SCORE_ACC = (0, 64, 128)
VALUE_ACC = (192, 224)
SCORE_AHEAD = 2
SCORE_SLOTS = 4
ATTN_STEP_TILES = 64
DFT_N2 = 128
VMEM_LIMIT = 56 * 1024 * 1024

BF16 = jnp.bfloat16
F32 = jnp.float32


def _rms(x, g):
    return x * jax.lax.rsqrt(jnp.mean(x * x, axis=-1, keepdims=True) + RMS_EPS) * g


def _dot(a, b):
    return jnp.dot(a, b, preferred_element_type=F32)


def _params(sem):
    return pltpu.CompilerParams(dimension_semantics=sem, vmem_limit_bytes=VMEM_LIMIT)


def _const_spec(shape):
    zeros = (0,) * len(shape)
    return pl.BlockSpec(shape, lambda *_: zeros)


def _proj_kernel(x_ref, gpre_ref, win_ref, gq_ref, wuqt_ref, gkv_ref, wuk_ref, wuvt_ref, cs_ref,
                 cos_ref, sina_ref, sinb_ref, cost_ref, sint_ref, qt_ref, k_ref, vt_ref, z_ref):
    tm = x_ref.shape[1]
    h = _rms(x_ref[0], gpre_ref[...]).astype(BF16)
    p = _dot(h, win_ref[...])
    cq = _rms(p[:, :Q_LORA], gq_ref[...])
    qt = _dot(wuqt_ref[...], cq.T.astype(BF16))
    ckv = _rms(p[:, Q_LORA:Q_LORA + KV_LORA], gkv_ref[...])
    kn = _dot(ckv.astype(BF16), wuk_ref[...])
    vt_ref[0] = _dot(wuvt_ref[...], ckv.T.astype(BF16)).astype(BF16)
    cos, sina, sinb = cos_ref[...], sina_ref[...], sinb_ref[...]
    t = p[:, 2 * FNET_WIDTH:]
    kr = (t * cos + pltpu.roll(t, HALF_ROPE, 1) * sina
          + pltpu.roll(t, HEAD_PAD - HALF_ROPE, 1) * sinb)
    cos_t, sin_t = cost_ref[...], sint_ref[...]
    lo, mid, hi = QK_NOPE, QK_NOPE + HALF_ROPE, QK_NOPE + QK_ROPE
    for hd in range(MLA_HEADS):
        base = hd * HEAD_PAD
        x1, x2 = qt[base + lo:base + mid], qt[base + mid:base + hi]
        qt_ref[0, base:base + lo, :] = (qt[base:base + lo] * Q_SCALE).astype(BF16)
        qt_ref[0, base + lo:base + mid, :] = ((x1 * cos_t - x2 * sin_t) * Q_SCALE).astype(BF16)
        qt_ref[0, base + mid:base + hi, :] = ((x2 * cos_t + x1 * sin_t) * Q_SCALE).astype(BF16)
        qt_ref[0, base + hi:base + HEAD_PAD, :] = jnp.zeros((HEAD_PAD - hi, tm), BF16)
        sl = slice(base, base + HEAD_PAD)
        k_ref[0, :, sl] = (kn[:, sl] + kr).astype(BF16)
    f = p[:, Q_LORA + KV_LORA:2 * FNET_WIDTH].astype(BF16)
    cs = cs_ref[...].astype(BF16)
    for g in range(FNET_GROUPS):
        sl = slice(g * FNET_GROUP_DIM, (g + 1) * FNET_GROUP_DIM)
        ab = _dot(f[:, sl], cs)
        z_ref[0, 0, g] = ab[:, :FNET_GROUP_DIM]
        z_ref[0, 1, g] = ab[:, FNET_GROUP_DIM:]


def _proj(x, gpre, win, gq, wuqt, gkv, wuk, wuvt, cs, rope_tabs, tm):
    b, s, _ = x.shape
    wide = MLA_HEADS * HEAD_PAD
    tok = lambda w: pl.BlockSpec((1, tm, w), lambda bi, i: (bi, i, 0))
    tab = pl.BlockSpec((tm, HEAD_PAD), lambda bi, i: (i, 0))
    tab_t = pl.BlockSpec((HALF_ROPE, tm), lambda bi, i: (0, i))
    return pl.pallas_call(
        _proj_kernel,
        grid=(b, s // tm),
        in_specs=[tok(D_MODEL), _const_spec(gpre.shape), _const_spec(win.shape),
                  _const_spec(gq.shape), _const_spec(wuqt.shape), _const_spec(gkv.shape),
                  _const_spec(wuk.shape), _const_spec(wuvt.shape), _const_spec(cs.shape),
                  tab, tab, tab, tab_t, tab_t],
        out_specs=[pl.BlockSpec((1, wide, tm), lambda bi, i: (bi, 0, i)),
                   tok(wide),
                   pl.BlockSpec((1, MLA_WIDTH, tm), lambda bi, i: (bi, 0, i)),
                   pl.BlockSpec((1, 2, FNET_GROUPS, tm, FNET_GROUP_DIM),
                                lambda bi, i: (bi, 0, 0, i, 0))],
        out_shape=[jax.ShapeDtypeStruct((b, wide, s), BF16),
                   jax.ShapeDtypeStruct((b, s, wide), BF16),
                   jax.ShapeDtypeStruct((b, MLA_WIDTH, s), BF16),
                   jax.ShapeDtypeStruct((b, 2, FNET_GROUPS, s, FNET_GROUP_DIM), F32)],
        compiler_params=_params(("parallel", "parallel")),
        name="proj",
    )(x, gpre, win, gq, wuqt, gkv, wuk, wuvt, cs, *rope_tabs)


def _attn_kernel(qt_ref, k_ref, vt_ref, o_ref, st_all, mc_all, *, tq, tk, heads, qsubs):
    st_ref = tuple(st_all.at[k] for k in range(SCORE_SLOTS))
    mc_ref = tuple(mc_all.at[k] for k in range(SCORE_SLOTS))
    s = k_ref.shape[1]
    n = s // tk
    nt = tk // MXU_TILE
    i = pl.program_id(2)
    ones = jnp.ones((DENOM_ROWS, MXU_TILE), BF16)
    pad_k = jnp.zeros((MXU_TILE, MXU_TILE - HEAD_PAD), BF16)
    pad_q = jnp.zeros((MXU_TILE - HEAD_PAD, tq), BF16)

    def q_rhs(hd, qi):
        q_off = qi * tq if isinstance(qi, int) else pl.multiple_of(qi * tq, tq)
        qt = qt_ref[0, hd * HEAD_PAD:(hd + 1) * HEAD_PAD, pl.ds(q_off, tq)]
        return jnp.concatenate([qt, pad_q], axis=0)

    def k_lhs(hd, c, j):
        k_lo = c * tk + j * MXU_TILE
        kc = k_ref[0, k_lo:k_lo + MXU_TILE, hd * HEAD_PAD:(hd + 1) * HEAD_PAD]
        return jnp.concatenate([kc, pad_k], axis=1)

    def pop_scores(hd, j, slot, mx):
        st = pltpu.matmul_pop(SCORE_ACC[j % len(SCORE_ACC)], (MXU_TILE, tq), F32, hd)
        st_ref[slot][hd, j * MXU_TILE:(j + 1) * MXU_TILE, :] = st
        return jnp.maximum(mx, jnp.max(st, axis=0, keepdims=True))

    lag = len(SCORE_ACC) - 1

    @pl.when(i == 0)
    def _():
        for c in range(SCORE_AHEAD):
            mx = [jnp.full((1, tq), -jnp.inf, F32) for _ in range(heads)]
            for hd in range(heads):
                pltpu.matmul_push_rhs(q_rhs(hd, 0), 0, hd)
            for j in range(nt + lag):
                for hd in range(heads):
                    if j < nt:
                        pltpu.matmul_acc_lhs(SCORE_ACC[j % len(SCORE_ACC)], k_lhs(hd, c, j), hd,
                                             0 if j == 0 else None)
                    if j >= lag:
                        mx[hd] = pop_scores(hd, j - lag, c, mx[hd])
            for hd in range(heads):
                mc_ref[c][hd] = mx[hd]

    pending = []
    tile = 0
    block_accs = []

    def drain(keep):
        while len(pending) > keep:
            pending.pop(0)()

    for qsub in range(qsubs):
        qb = i * qsubs + qsub
        ms = [jnp.full((1, tq), -jnp.inf, F32) for _ in range(heads)]
        accs = [jnp.zeros((V_HEAD + DENOM_ROWS, tq), F32) for _ in range(heads)]
        block_accs.append(accs)
        for slot in range(n):
            ahead = slot + SCORE_AHEAD
            if ahead < n:
                qi, c_next = qb, ahead
            elif qsub + 1 < qsubs:
                qi, c_next = qb + 1, ahead - n
            else:
                qi, c_next = jnp.minimum(qb + 1, s // tq - 1), ahead - n
            fill = ahead % SCORE_SLOTS
            value_acc = VALUE_ACC[slot % len(VALUE_ACC)]
            m_old = list(ms)
            ms = [jnp.maximum(ms[hd], mc_ref[slot][hd]) for hd in range(heads)]
            mx = [jnp.full((1, tq), -jnp.inf, F32) for _ in range(heads)]
            for j in range(nt):
                score_acc = SCORE_ACC[tile % len(SCORE_ACC)]
                tile += 1
                for hd in range(heads):
                    rows = slice(j * MXU_TILE, (j + 1) * MXU_TILE)
                    p = jnp.exp2(st_ref[slot][hd, rows, :] - ms[hd]).astype(BF16)
                    pltpu.matmul_push_rhs(p, 1, hd)
                    pltpu.matmul_push_rhs(q_rhs(hd, qi), 0, hd)
                    v_lo = slot * tk + j * MXU_TILE
                    vc = vt_ref[0, hd * V_HEAD:(hd + 1) * V_HEAD, v_lo:v_lo + MXU_TILE]
                    pltpu.matmul_acc_lhs(value_acc, jnp.concatenate([vc, ones], axis=0), hd, 1)
                    pltpu.matmul_acc_lhs(score_acc, k_lhs(hd, c_next, j), hd, 0)

                def pop_tile(j=j, fill=fill, mx=mx, score_acc=score_acc, value_acc=value_acc,
                             m_old=m_old, m_new=list(ms), accs=accs):
                    for hd in range(heads):
                        st = pltpu.matmul_pop(score_acc, (MXU_TILE, tq), F32, hd)
                        st_ref[fill][hd, j * MXU_TILE:(j + 1) * MXU_TILE, :] = st
                        mx[hd] = jnp.maximum(mx[hd], jnp.max(st, axis=0, keepdims=True))
                        if j == nt - 1:
                            mc_ref[fill][hd] = mx[hd]
                            pv = pltpu.matmul_pop(value_acc, (V_HEAD + DENOM_ROWS, tq), F32, hd)
                            accs[hd] = jnp.exp2(m_old[hd] - m_new[hd]) * accs[hd] + pv

                drain(lag - 1)
                pending.append(pop_tile)
    drain(0)
    for qsub, accs in enumerate(block_accs):
        outs = [acc[:V_HEAD] / acc[V_HEAD:V_HEAD + 1] for acc in accs]
        o_ref[0, qsub * tq:(qsub + 1) * tq, :] = jnp.concatenate(outs, axis=0).T.astype(BF16)


def _attn(qt, k, vt, tq, tk, qsubs):
    b, _, s = qt.shape
    heads = LANES // V_HEAD
    groups = MLA_HEADS // heads
    assert s % (qsubs * tq) == 0 and s == SCORE_SLOTS * tk and tq == MXU_TILE and tk % MXU_TILE == 0
    return pl.pallas_call(
        functools.partial(_attn_kernel, tq=tq, tk=tk, heads=heads, qsubs=qsubs),
        grid=(b, groups, s // (qsubs * tq)),
        in_specs=[pl.BlockSpec((1, heads * HEAD_PAD, s), lambda bi, j, i: (bi, j, 0)),
                  pl.BlockSpec((1, s, heads * HEAD_PAD), lambda bi, j, i: (bi, 0, j)),
                  pl.BlockSpec((1, heads * V_HEAD, s), lambda bi, j, i: (bi, j, 0))],
        out_specs=pl.BlockSpec((1, qsubs * tq, heads * V_HEAD), lambda bi, j, i: (bi, i, j)),
        out_shape=jax.ShapeDtypeStruct((b, s, MLA_WIDTH), BF16),
        scratch_shapes=[pltpu.VMEM((SCORE_SLOTS, heads, tk + SCORE_PAD, tq), F32),
                        pltpu.VMEM((SCORE_SLOTS, heads, 1, tq), F32)],
        compiler_params=_params(("parallel", "parallel", "arbitrary")),
        name="attn",
    )(qt, k, vt)


def _dft1_kernel(z_ref, f1_ref, y_ref):
    groups, n1, r, c = z_ref.shape[2:]
    f1 = f1_ref[...].astype(BF16)
    flat = lambda ref, part, g: ref.at[0, part, g].reshape(n1 * r, c)
    for j in range(r):
        rows = pl.ds(j, n1, stride=r)
        gather = lambda part: jnp.concatenate(
            [flat(z_ref, part, g)[rows, :] for g in range(groups)], axis=1).astype(BF16)
        y = _dot(f1[:, :n1], gather(0)) + _dot(f1[:, n1:], gather(1))
        for g in range(groups):
            flat(y_ref, 0, g)[rows, :] = y[:n1, g * c:(g + 1) * c]
            flat(y_ref, 1, g)[rows, :] = y[n1:, g * c:(g + 1) * c]


def _dft1(z, f1, r):
    b, _, groups, n1, n2, c = z.shape
    spec = pl.BlockSpec((1, 2, groups, n1, r, c), lambda bi, j: (bi, 0, 0, 0, j, 0))
    return pl.pallas_call(
        _dft1_kernel,
        grid=(b, n2 // r),
        in_specs=[spec, _const_spec(f1.shape)],
        out_specs=spec,
        out_shape=jax.ShapeDtypeStruct(z.shape, F32),
        compiler_params=_params(("parallel", "parallel")),
        name="dft1",
    )(z, f1)


def _dft2_kernel(y_ref, w_ref, wf_ref, o_ref):
    groups, kb, n2, c = y_ref.shape[2:]
    for t in range(kb):
        w = w_ref[t].astype(BF16)
        part = lambda p: jnp.concatenate(
            [y_ref[0, p, g, t] for g in range(groups)], axis=1).astype(BF16)
        xr = (_dot(w[:, :n2], part(0)) + _dot(w[:, n2:], part(1))).astype(BF16)
        for g in range(groups):
            rows = o_ref.at[0, g].reshape(n2 * kb, c)
            rows[pl.ds(t, n2, stride=kb), :] = _dot(xr[:, g * c:(g + 1) * c], wf_ref[g])


def _dft2(y, w, wf, kb):
    b, _, groups, n1, n2, c = y.shape
    return pl.pallas_call(
        _dft2_kernel,
        grid=(b, n1 // kb),
        in_specs=[pl.BlockSpec((1, 2, groups, kb, n2, c), lambda bi, j: (bi, 0, 0, j, 0, 0)),
                  pl.BlockSpec((kb, n2, 2 * n2), lambda bi, j: (j, 0, 0)),
                  _const_spec(wf.shape)],
        out_specs=pl.BlockSpec((1, groups, n2, kb, c), lambda bi, j: (bi, 0, 0, j, 0)),
        out_shape=jax.ShapeDtypeStruct((b, groups, n2, n1, c), F32),
        compiler_params=_params(("parallel", "parallel")),
        name="dft2",
    )(y, w, wf)


HALO = 16


def _tail_kernel(a_ref, ap_ref, an_ref, f_ref, fp_ref, fn_ref, x_ref, xp_ref, xn_ref,
                 woa_ref, wof_ref, gpm_ref, gpf_ref, wg_ref, wu_ref, cw_ref, cb_ref, wd_ref,
                 gpo_ref, y_ref, act_ref, *, ffc):
    i = pl.program_id(1)
    tm = x_ref.shape[1]
    rows = tm + 2 * HALO
    ext = lambda p, c, n: jnp.concatenate([p[0], c[0], n[0]], axis=0)
    f_ext = jnp.concatenate(
        [jnp.concatenate([fp_ref[0, g], f_ref[0, g], fn_ref[0, g]], axis=0)
         for g in range(FNET_GROUPS)], axis=1).astype(BF16)
    mix = _dot(ext(ap_ref, a_ref, an_ref), woa_ref[...]) + _dot(f_ext, wof_ref[...])
    x1 = ext(xp_ref, x_ref, xn_ref) + _rms(mix, gpm_ref[...])
    row = jax.lax.broadcasted_iota(jnp.int32, (rows, 1), 0)
    inside = ((row >= HALO) | (i > 0)) & ((row < HALO + tm) | (i < pl.num_programs(1) - 1))
    hext = jnp.where(inside, _rms(x1, gpf_ref[...]), 0.0).astype(BF16)
    h = hext[HALO:HALO + tm]
    for c in range(D_FF // ffc):
        sl = slice(c * ffc, (c + 1) * ffc)
        g = _dot(hext, wg_ref[:, sl])
        u = _dot(h, wu_ref[:, sl])
        g_prev = pltpu.roll(g, 1, 0)[HALO:HALO + tm]
        g_next = pltpu.roll(g, rows - 1, 0)[HALO:HALO + tm]
        gate = (cb_ref[:, sl] + g_prev * cw_ref[0:1, sl] + g[HALO:HALO + tm] * cw_ref[1:2, sl]
                + g_next * cw_ref[2:3, sl])
        inner = math.sqrt(2.0 / math.pi) * (gate + 0.044715 * (gate * gate * gate))
        act = 0.5 * gate * (1.0 + jnp.tanh(inner)) * u
        act_ref[:, sl] = act.astype(BF16)
    out = _dot(act_ref[...], wd_ref[...])
    y_ref[0] = x1[HALO:HALO + tm] + _rms(out, gpo_ref[...])


def _tail(a, f, x, woa, wof, gpm, gpf, wg, wu, cw, cb, wd, gpo, tm, ffc):
    b, s, _ = x.shape
    per = tm // HALO
    last = s // HALO - 1
    single = lambda shape: pl.BlockSpec(shape, lambda *_: (0,) * len(shape),
                                        pipeline_mode=pl.Buffered(1))

    def with_halo(width):
        return [pl.BlockSpec((1, tm, width), lambda bi, i: (bi, i, 0)),
                pl.BlockSpec((1, HALO, width), lambda bi, i: (bi, jnp.maximum(i * per - 1, 0), 0)),
                pl.BlockSpec((1, HALO, width), lambda bi, i: (bi, jnp.minimum((i + 1) * per, last), 0))]

    grouped = lambda rows, row_block: pl.BlockSpec(
        (1, FNET_GROUPS, rows, FNET_GROUP_DIM), lambda bi, i: (bi, 0, row_block(i), 0))
    f_specs = [grouped(tm, lambda i: i),
               grouped(HALO, lambda i: jnp.maximum(i * per - 1, 0)),
               grouped(HALO, lambda i: jnp.minimum((i + 1) * per, last))]
    return pl.pallas_call(
        functools.partial(_tail_kernel, ffc=ffc),
        grid=(b, s // tm),
        in_specs=with_halo(MLA_WIDTH) + f_specs + with_halo(D_MODEL) + [
            single(woa.shape), single(wof.shape), _const_spec(gpm.shape), _const_spec(gpf.shape),
            single(wg.shape), single(wu.shape), _const_spec(cw.shape), _const_spec(cb.shape),
            single(wd.shape), _const_spec(gpo.shape)],
        out_specs=pl.BlockSpec((1, tm, D_MODEL), lambda bi, i: (bi, i, 0)),
        out_shape=jax.ShapeDtypeStruct(x.shape, F32),
        scratch_shapes=[pltpu.VMEM((tm, D_FF), BF16)],
        compiler_params=_params(("parallel", "arbitrary")),
        name="tail",
    )(a, a, a, f, f, f, x, x, x, woa, wof, gpm, gpf, wg, wu, cw, cb, wd, gpo)


def _rope_tables(s):
    ang = np.arange(s, dtype=np.float64)[:, None] * (
        ROPE_THETA ** (-np.arange(0, QK_ROPE, 2, dtype=np.float64) / QK_ROPE))[None, :]
    cos, sin = np.cos(ang), np.sin(ang)
    zeros = np.zeros((s, QK_NOPE))
    zh = np.zeros((s, HALF_ROPE))
    tail = np.zeros((s, HEAD_PAD - QK_NOPE - QK_ROPE))
    cos_t = np.concatenate([np.ones((s, QK_NOPE)), cos, cos, tail], axis=1)
    sina = np.concatenate([zeros, zh, sin, tail], axis=1)
    sinb = np.concatenate([zeros, -sin, zh, tail], axis=1)
    return tuple(jnp.asarray(t, F32) for t in (cos_t, sina, sinb, cos.T, sin.T))


def _angles(num, den):
    return (2.0 * np.pi / den) * (num % den).astype(np.float64)


def _dft_tables(s):
    n2 = DFT_N2
    n1 = s // n2
    c = np.arange(FNET_GROUP_DIM, dtype=np.int64)
    ang = _angles(c[:, None] * c[None, :], FNET_GROUP_DIM)
    cs = np.concatenate([np.cos(ang), np.sin(ang)], axis=1) / math.sqrt(FNET_GROUP_DIM)
    k1 = np.arange(n1, dtype=np.int64)
    a1 = _angles(k1[:, None] * k1[None, :], n1)
    c1, s1 = np.cos(a1), np.sin(a1)
    f1 = np.concatenate([np.concatenate([c1, -s1], axis=1),
                         np.concatenate([s1, c1], axis=1)], axis=0) / math.sqrt(n1)
    k2 = np.arange(n2, dtype=np.int64)
    kk = k1[:, None, None] + n1 * k2[None, :, None]
    a2 = _angles(kk * k2[None, None, :], s)
    w = np.concatenate([np.cos(a2), -np.sin(a2)], axis=2) / math.sqrt(n2)
    return tuple(jnp.asarray(t, F32) for t in (cs, f1, w))


def _prep_weights(g_pre_mix, w_in, g_q, w_uq, g_kv, w_ukv, w_fnet, w_out, g_post_mix,
                  g_pre_ffn, w_gate, w_up, conv_w, conv_b, w_down, g_post_ffn):
    win = w_in[0]
    f_lo = Q_LORA + KV_LORA + QK_ROPE
    kr_cols = jnp.concatenate([jnp.zeros((D_MODEL, QK_NOPE), F32),
                               win[:, Q_LORA + KV_LORA:f_lo],
                               jnp.zeros((D_MODEL, HEAD_PAD - QK_NOPE - QK_ROPE), F32)], axis=1)
    win_p = jnp.concatenate([win[:, :Q_LORA + KV_LORA], win[:, f_lo:], kr_cols], axis=1)
    wuq = w_uq[0].reshape(Q_LORA, MLA_HEADS, QK_NOPE + QK_ROPE)
    wuq = jnp.pad(wuq, ((0, 0), (0, 0), (0, HEAD_PAD - QK_NOPE - QK_ROPE)))
    wuq = wuq.reshape(Q_LORA, MLA_HEADS * HEAD_PAD)
    wukv = w_ukv[0].reshape(KV_LORA, MLA_HEADS, QK_NOPE + V_HEAD)
    wuk = jnp.pad(wukv[..., :QK_NOPE], ((0, 0), (0, 0), (0, HEAD_PAD - QK_NOPE)))
    wuk = wuk.reshape(KV_LORA, MLA_HEADS * HEAD_PAD)
    wuv = wukv[..., QK_NOPE:].reshape(KV_LORA, MLA_WIDTH)
    return dict(
        gpre=g_pre_mix, win=win_p.astype(BF16), gq=g_q, wuqt=wuq.T.astype(BF16), gkv=g_kv,
        wuk=wuk.astype(BF16), wuvt=wuv.T.astype(BF16), wf=w_fnet[0].astype(BF16),
        woa=w_out[0, :MLA_WIDTH].astype(BF16), wof=w_out[0, MLA_WIDTH:].astype(BF16),
        gpm=g_post_mix, gpf=g_pre_ffn, wg=w_gate[0].astype(BF16), wu=w_up[0].astype(BF16),
        cw=conv_w[0], cb=conv_b, wd=w_down[0].astype(BF16), gpo=g_post_ffn)


def _trunk(x, w):
    b, s, _ = x.shape
    n2 = DFT_N2
    n1 = s // n2
    rope_tabs = _rope_tables(s)
    cs, f1, wtab = _dft_tables(s)
    qt, k, vt, z = _proj(x, w["gpre"], w["win"], w["gq"], w["wuqt"], w["gkv"], w["wuk"], w["wuvt"],
                         cs, rope_tabs, tm=512)
    a = _attn(qt, k, vt, tq=MXU_TILE, tk=s // SCORE_SLOTS, qsubs=ATTN_STEP_TILES // (s // MXU_TILE))
    y = _dft1(z.reshape(b, 2, FNET_GROUPS, n1, n2, FNET_GROUP_DIM), f1, r=16)
    f = _dft2(y, wtab, w["wf"], kb=8)
    f = f.reshape(b, FNET_GROUPS, s, FNET_GROUP_DIM)
    return _tail(a, f, x, w["woa"], w["wof"], w["gpm"], w["gpf"], w["wg"], w["wu"], w["cw"],
                 w["cb"], w["wd"], w["gpo"], tm=512, ffc=256)


def kernel(x_prompt, x_sample, g_pre_mix, w_in, g_q, w_uq, g_kv, w_ukv, w_fnet, w_out,
           g_post_mix, g_pre_ffn, w_gate, w_up, conv_w, conv_b, w_down, g_post_ffn):
    w = _prep_weights(g_pre_mix, w_in, g_q, w_uq, g_kv, w_ukv, w_fnet, w_out, g_post_mix,
                      g_pre_ffn, w_gate, w_up, conv_w, conv_b, w_down, g_post_ffn)
    return _trunk(x_prompt, w), _trunk(x_sample, w)
```

```python
import functools
import math

import jax
import jax.numpy as jnp
import numpy as np
from jax.experimental import pallas as pl
from jax.experimental.pallas import tpu as pltpu

D_MODEL = 1024
MLA_HEADS = 8
QK_NOPE = 64
QK_ROPE = 32
V_HEAD = 64
MLA_WIDTH = MLA_HEADS * V_HEAD
Q_LORA = 256
KV_LORA = 256
FNET_GROUPS = 4
FNET_WIDTH = 512
FNET_GROUP_DIM = 128
D_FF = 2816
ROPE_THETA = 10000.0
RMS_EPS = 1e-6
ATTN_SCALE = 1.0 / math.sqrt(QK_NOPE + QK_ROPE)
Q_SCALE = ATTN_SCALE * math.log2(math.e)

LANES = 128
HEAD_PAD = 128
HALF_ROPE = QK_ROPE // 2
DENOM_ROWS = 16
SCORE_PAD = 8
MXU_TILE = 256
SCORE_ACC = (0, 64, 128)
VALUE_ACC = (192, 224)
SCORE_AHEAD = 2
SCORE_SLOTS = 4
ATTN_QUERY_BLOCKS = 4
TOKEN_TILE = 512
FF_CHUNK = 256
DFT1_ROWS = 16
DFT2_ROWS = 8
DFT_N2 = 128
VMEM_LIMIT = 56 * 1024 * 1024

BF16 = jnp.bfloat16
F32 = jnp.float32


def _rms(x, g):
    return x * jax.lax.rsqrt(jnp.mean(x * x, axis=-1, keepdims=True) + RMS_EPS) * g


def _dot(a, b):
    return jnp.dot(a, b, preferred_element_type=F32)


def _params(sem):
    return pltpu.CompilerParams(dimension_semantics=sem, vmem_limit_bytes=VMEM_LIMIT)


def _const_spec(shape):
    zeros = (0,) * len(shape)
    return pl.BlockSpec(shape, lambda *_: zeros)


def _proj_kernel(x_ref, gpre_ref, win_ref, gq_ref, wuqt_ref, gkv_ref, wuk_ref, wuvt_ref, cs_ref,
                 cos_ref, sina_ref, sinb_ref, cost_ref, sint_ref, qt_ref, k_ref, vt_ref, z_ref):
    tm = x_ref.shape[1]
    h = _rms(x_ref[0], gpre_ref[...]).astype(BF16)
    p = _dot(h, win_ref[...])
    cq = _rms(p[:, :Q_LORA], gq_ref[...])
    qt = _dot(wuqt_ref[...], cq.T.astype(BF16))
    ckv = _rms(p[:, Q_LORA:Q_LORA + KV_LORA], gkv_ref[...])
    kn = _dot(ckv.astype(BF16), wuk_ref[...])
    vt_ref[0] = _dot(wuvt_ref[...], ckv.T.astype(BF16)).astype(BF16)
    cos, sina, sinb = cos_ref[...], sina_ref[...], sinb_ref[...]
    t = p[:, 2 * FNET_WIDTH:]
    kr = (t * cos + pltpu.roll(t, HALF_ROPE, 1) * sina
          + pltpu.roll(t, HEAD_PAD - HALF_ROPE, 1) * sinb)
    cos_t, sin_t = cost_ref[...], sint_ref[...]
    lo, mid, hi = QK_NOPE, QK_NOPE + HALF_ROPE, QK_NOPE + QK_ROPE
    for hd in range(MLA_HEADS):
        base = hd * HEAD_PAD
        x1, x2 = qt[base + lo:base + mid], qt[base + mid:base + hi]
        qt_ref[0, base:base + lo, :] = (qt[base:base + lo] * Q_SCALE).astype(BF16)
        qt_ref[0, base + lo:base + mid, :] = ((x1 * cos_t - x2 * sin_t) * Q_SCALE).astype(BF16)
        qt_ref[0, base + mid:base + hi, :] = ((x2 * cos_t + x1 * sin_t) * Q_SCALE).astype(BF16)
        qt_ref[0, base + hi:base + HEAD_PAD, :] = jnp.zeros((HEAD_PAD - hi, tm), BF16)
        sl = slice(base, base + HEAD_PAD)
        k_ref[0, :, sl] = (kn[:, sl] + kr).astype(BF16)
    f = p[:, Q_LORA + KV_LORA:2 * FNET_WIDTH].astype(BF16)
    cs = cs_ref[...].astype(BF16)
    for g in range(FNET_GROUPS):
        sl = slice(g * FNET_GROUP_DIM, (g + 1) * FNET_GROUP_DIM)
        ab = _dot(f[:, sl], cs)
        z_ref[0, 0, g] = ab[:, :FNET_GROUP_DIM]
        z_ref[0, 1, g] = ab[:, FNET_GROUP_DIM:]


def _proj(x, gpre, win, gq, wuqt, gkv, wuk, wuvt, cs, rope_tabs, tm):
    b, s, _ = x.shape
    wide = MLA_HEADS * HEAD_PAD
    tok = lambda w: pl.BlockSpec((1, tm, w), lambda bi, i: (bi, i, 0))
    tab = pl.BlockSpec((tm, HEAD_PAD), lambda bi, i: (i, 0))
    tab_t = pl.BlockSpec((HALF_ROPE, tm), lambda bi, i: (0, i))
    return pl.pallas_call(
        _proj_kernel,
        grid=(b, s // tm),
        in_specs=[tok(D_MODEL), _const_spec(gpre.shape), _const_spec(win.shape),
                  _const_spec(gq.shape), _const_spec(wuqt.shape), _const_spec(gkv.shape),
                  _const_spec(wuk.shape), _const_spec(wuvt.shape), _const_spec(cs.shape),
                  tab, tab, tab, tab_t, tab_t],
        out_specs=[pl.BlockSpec((1, wide, tm), lambda bi, i: (bi, 0, i)),
                   tok(wide),
                   pl.BlockSpec((1, MLA_WIDTH, tm), lambda bi, i: (bi, 0, i)),
                   pl.BlockSpec((1, 2, FNET_GROUPS, tm, FNET_GROUP_DIM),
                                lambda bi, i: (bi, 0, 0, i, 0))],
        out_shape=[jax.ShapeDtypeStruct((b, wide, s), BF16),
                   jax.ShapeDtypeStruct((b, s, wide), BF16),
                   jax.ShapeDtypeStruct((b, MLA_WIDTH, s), BF16),
                   jax.ShapeDtypeStruct((b, 2, FNET_GROUPS, s, FNET_GROUP_DIM), F32)],
        compiler_params=_params(("parallel", "parallel")),
        name="proj",
    )(x, gpre, win, gq, wuqt, gkv, wuk, wuvt, cs, *rope_tabs)


def _attn_kernel(qt_ref, k_ref, vt_ref, o_ref, st_all, mc_all, *, tq, tk, heads, qsubs):
    st_ref = tuple(st_all.at[k] for k in range(SCORE_SLOTS))
    mc_ref = tuple(mc_all.at[k] for k in range(SCORE_SLOTS))
    s = k_ref.shape[1]
    n = s // tk
    nt = tk // MXU_TILE
    i = pl.program_id(2)
    ones = jnp.ones((DENOM_ROWS, MXU_TILE), BF16)
    pad_k = jnp.zeros((MXU_TILE, MXU_TILE - HEAD_PAD), BF16)
    pad_q = jnp.zeros((MXU_TILE - HEAD_PAD, tq), BF16)

    def q_rhs(hd, qi):
        q_off = qi * tq if isinstance(qi, int) else pl.multiple_of(qi * tq, tq)
        qt = qt_ref[0, hd * HEAD_PAD:(hd + 1) * HEAD_PAD, pl.ds(q_off, tq)]
        return jnp.concatenate([qt, pad_q], axis=0)

    def k_lhs(hd, c, j):
        k_lo = c * tk + j * MXU_TILE
        kc = k_ref[0, k_lo:k_lo + MXU_TILE, hd * HEAD_PAD:(hd + 1) * HEAD_PAD]
        return jnp.concatenate([kc, pad_k], axis=1)

    def pop_scores(hd, j, slot, mx):
        st = pltpu.matmul_pop(SCORE_ACC[j % len(SCORE_ACC)], (MXU_TILE, tq), F32, hd)
        st_ref[slot][hd, j * MXU_TILE:(j + 1) * MXU_TILE, :] = st
        return jnp.maximum(mx, jnp.max(st, axis=0, keepdims=True))

    lag = len(SCORE_ACC) - 1

    @pl.when(i == 0)
    def _():
        for c in range(SCORE_AHEAD):
            mx = [jnp.full((1, tq), -jnp.inf, F32) for _ in range(heads)]
            for hd in range(heads):
                pltpu.matmul_push_rhs(q_rhs(hd, 0), 0, hd)
            for j in range(nt + lag):
                for hd in range(heads):
                    if j < nt:
                        pltpu.matmul_acc_lhs(SCORE_ACC[j % len(SCORE_ACC)], k_lhs(hd, c, j), hd,
                                             0 if j == 0 else None)
                    if j >= lag:
                        mx[hd] = pop_scores(hd, j - lag, c, mx[hd])
            for hd in range(heads):
                mc_ref[c][hd] = mx[hd]

    pending = []
    tile = 0
    block_accs = []

    def drain(keep):
        while len(pending) > keep:
            pending.pop(0)()

    for qsub in range(qsubs):
        qb = i * qsubs + qsub
        ms = [jnp.full((1, tq), -jnp.inf, F32) for _ in range(heads)]
        accs = [jnp.zeros((V_HEAD + DENOM_ROWS, tq), F32) for _ in range(heads)]
        block_accs.append(accs)
        for slot in range(n):
            ahead = slot + SCORE_AHEAD
            if ahead < n:
                qi, c_next = qb, ahead
            elif qsub + 1 < qsubs:
                qi, c_next = qb + 1, ahead - n
            else:
                qi, c_next = jnp.minimum(qb + 1, s // tq - 1), ahead - n
            fill = ahead % SCORE_SLOTS
            value_acc = VALUE_ACC[slot % len(VALUE_ACC)]
            m_old = list(ms)
            ms = [jnp.maximum(ms[hd], mc_ref[slot][hd]) for hd in range(heads)]
            mx = [jnp.full((1, tq), -jnp.inf, F32) for _ in range(heads)]
            for j in range(nt):
                score_acc = SCORE_ACC[tile % len(SCORE_ACC)]
                tile += 1
                for hd in range(heads):
                    rows = slice(j * MXU_TILE, (j + 1) * MXU_TILE)
                    p = jnp.exp2(st_ref[slot][hd, rows, :] - ms[hd]).astype(BF16)
                    pltpu.matmul_push_rhs(p, 1, hd)
                    pltpu.matmul_push_rhs(q_rhs(hd, qi), 0, hd)
                    v_lo = slot * tk + j * MXU_TILE
                    vc = vt_ref[0, hd * V_HEAD:(hd + 1) * V_HEAD, v_lo:v_lo + MXU_TILE]
                    pltpu.matmul_acc_lhs(value_acc, jnp.concatenate([vc, ones], axis=0), hd, 1)
                    pltpu.matmul_acc_lhs(score_acc, k_lhs(hd, c_next, j), hd, 0)

                def pop_tile(j=j, fill=fill, mx=mx, score_acc=score_acc, value_acc=value_acc,
                             m_old=m_old, m_new=list(ms), accs=accs):
                    for hd in range(heads):
                        st = pltpu.matmul_pop(score_acc, (MXU_TILE, tq), F32, hd)
                        st_ref[fill][hd, j * MXU_TILE:(j + 1) * MXU_TILE, :] = st
                        mx[hd] = jnp.maximum(mx[hd], jnp.max(st, axis=0, keepdims=True))
                        if j == nt - 1:
                            mc_ref[fill][hd] = mx[hd]
                            pv = pltpu.matmul_pop(value_acc, (V_HEAD + DENOM_ROWS, tq), F32, hd)
                            accs[hd] = jnp.exp2(m_old[hd] - m_new[hd]) * accs[hd] + pv

                drain(lag - 1)
                pending.append(pop_tile)
    drain(0)
    for qsub, accs in enumerate(block_accs):
        outs = [acc[:V_HEAD] / acc[V_HEAD:V_HEAD + 1] for acc in accs]
        o_ref[0, qsub * tq:(qsub + 1) * tq, :] = jnp.concatenate(outs, axis=0).T.astype(BF16)


def _attn(qt, k, vt, tq, tk, qsubs):
    b, _, s = qt.shape
    heads = LANES // V_HEAD
    groups = MLA_HEADS // heads
    assert s % (qsubs * tq) == 0 and s == SCORE_SLOTS * tk and tq == MXU_TILE and tk % MXU_TILE == 0
    return pl.pallas_call(
        functools.partial(_attn_kernel, tq=tq, tk=tk, heads=heads, qsubs=qsubs),
        grid=(b, groups, s // (qsubs * tq)),
        in_specs=[pl.BlockSpec((1, heads * HEAD_PAD, s), lambda bi, j, i: (bi, j, 0)),
                  pl.BlockSpec((1, s, heads * HEAD_PAD), lambda bi, j, i: (bi, 0, j)),
                  pl.BlockSpec((1, heads * V_HEAD, s), lambda bi, j, i: (bi, j, 0))],
        out_specs=pl.BlockSpec((1, qsubs * tq, heads * V_HEAD), lambda bi, j, i: (bi, i, j)),
        out_shape=jax.ShapeDtypeStruct((b, s, MLA_WIDTH), BF16),
        scratch_shapes=[pltpu.VMEM((SCORE_SLOTS, heads, tk + SCORE_PAD, tq), F32),
                        pltpu.VMEM((SCORE_SLOTS, heads, 1, tq), F32)],
        compiler_params=_params(("parallel", "parallel", "arbitrary")),
        name="attn",
    )(qt, k, vt)


def _dft1_kernel(z_ref, f1_ref, y_ref):
    groups, n1, r, c = z_ref.shape[2:]
    f1 = f1_ref[...].astype(BF16)
    flat = lambda ref, part, g: ref.at[0, part, g].reshape(n1 * r, c)
    for j in range(r):
        rows = pl.ds(j, n1, stride=r)
        gather = lambda part: jnp.concatenate(
            [flat(z_ref, part, g)[rows, :] for g in range(groups)], axis=1).astype(BF16)
        y = _dot(f1[:, :n1], gather(0)) + _dot(f1[:, n1:], gather(1))
        for g in range(groups):
            flat(y_ref, 0, g)[rows, :] = y[:n1, g * c:(g + 1) * c]
            flat(y_ref, 1, g)[rows, :] = y[n1:, g * c:(g + 1) * c]


def _dft1(z, f1, r):
    b, _, groups, n1, n2, c = z.shape
    spec = pl.BlockSpec((1, 2, groups, n1, r, c), lambda bi, j: (bi, 0, 0, 0, j, 0))
    return pl.pallas_call(
        _dft1_kernel,
        grid=(b, n2 // r),
        in_specs=[spec, _const_spec(f1.shape)],
        out_specs=spec,
        out_shape=jax.ShapeDtypeStruct(z.shape, F32),
        compiler_params=_params(("parallel", "parallel")),
        name="dft1",
    )(z, f1)


def _dft2_kernel(y_ref, w_ref, wf_ref, o_ref):
    groups, kb, n2, c = y_ref.shape[2:]
    for t in range(kb):
        w = w_ref[t].astype(BF16)
        part = lambda p: jnp.concatenate(
            [y_ref[0, p, g, t] for g in range(groups)], axis=1).astype(BF16)
        xr = (_dot(w[:, :n2], part(0)) + _dot(w[:, n2:], part(1))).astype(BF16)
        for g in range(groups):
            rows = o_ref.at[0, g].reshape(n2 * kb, c)
            rows[pl.ds(t, n2, stride=kb), :] = _dot(xr[:, g * c:(g + 1) * c], wf_ref[g])


def _dft2(y, w, wf, kb):
    b, _, groups, n1, n2, c = y.shape
    return pl.pallas_call(
        _dft2_kernel,
        grid=(b, n1 // kb),
        in_specs=[pl.BlockSpec((1, 2, groups, kb, n2, c), lambda bi, j: (bi, 0, 0, j, 0, 0)),
                  pl.BlockSpec((kb, n2, 2 * n2), lambda bi, j: (j, 0, 0)),
                  _const_spec(wf.shape)],
        out_specs=pl.BlockSpec((1, groups, n2, kb, c), lambda bi, j: (bi, 0, 0, j, 0)),
        out_shape=jax.ShapeDtypeStruct((b, groups, n2, n1, c), F32),
        compiler_params=_params(("parallel", "parallel")),
        name="dft2",
    )(y, w, wf)


HALO = 16


def _tail_kernel(a_ref, ap_ref, an_ref, f_ref, fp_ref, fn_ref, x_ref, xp_ref, xn_ref,
                 woa_ref, wof_ref, gpm_ref, gpf_ref, wg_ref, wu_ref, cw_ref, cb_ref, wd_ref,
                 gpo_ref, y_ref, act_ref, *, ffc):
    i = pl.program_id(1)
    tm = x_ref.shape[1]
    rows = tm + 2 * HALO
    ext = lambda p, c, n: jnp.concatenate([p[0], c[0], n[0]], axis=0)
    f_ext = jnp.concatenate(
        [jnp.concatenate([fp_ref[0, g], f_ref[0, g], fn_ref[0, g]], axis=0)
         for g in range(FNET_GROUPS)], axis=1).astype(BF16)
    mix = _dot(ext(ap_ref, a_ref, an_ref), woa_ref[...]) + _dot(f_ext, wof_ref[...])
    x1 = ext(xp_ref, x_ref, xn_ref) + _rms(mix, gpm_ref[...])
    row = jax.lax.broadcasted_iota(jnp.int32, (rows, 1), 0)
    inside = ((row >= HALO) | (i > 0)) & ((row < HALO + tm) | (i < pl.num_programs(1) - 1))
    hext = jnp.where(inside, _rms(x1, gpf_ref[...]), 0.0).astype(BF16)
    h = hext[HALO:HALO + tm]
    for c in range(D_FF // ffc):
        sl = slice(c * ffc, (c + 1) * ffc)
        g = _dot(hext, wg_ref[:, sl])
        u = _dot(h, wu_ref[:, sl])
        g_prev = pltpu.roll(g, 1, 0)[HALO:HALO + tm]
        g_next = pltpu.roll(g, rows - 1, 0)[HALO:HALO + tm]
        gate = (cb_ref[:, sl] + g_prev * cw_ref[0:1, sl] + g[HALO:HALO + tm] * cw_ref[1:2, sl]
                + g_next * cw_ref[2:3, sl])
        inner = math.sqrt(2.0 / math.pi) * (gate + 0.044715 * (gate * gate * gate))
        act = 0.5 * gate * (1.0 + jnp.tanh(inner)) * u
        act_ref[:, sl] = act.astype(BF16)
    out = _dot(act_ref[...], wd_ref[...])
    y_ref[0] = x1[HALO:HALO + tm] + _rms(out, gpo_ref[...])


def _tail(a, f, x, woa, wof, gpm, gpf, wg, wu, cw, cb, wd, gpo, tm, ffc):
    b, s, _ = x.shape
    per = tm // HALO
    last = s // HALO - 1
    single = lambda shape: pl.BlockSpec(shape, lambda *_: (0,) * len(shape),
                                        pipeline_mode=pl.Buffered(1))

    def with_halo(width):
        return [pl.BlockSpec((1, tm, width), lambda bi, i: (bi, i, 0)),
                pl.BlockSpec((1, HALO, width), lambda bi, i: (bi, jnp.maximum(i * per - 1, 0), 0)),
                pl.BlockSpec((1, HALO, width), lambda bi, i: (bi, jnp.minimum((i + 1) * per, last), 0))]

    grouped = lambda rows, row_block: pl.BlockSpec(
        (1, FNET_GROUPS, rows, FNET_GROUP_DIM), lambda bi, i: (bi, 0, row_block(i), 0))
    f_specs = [grouped(tm, lambda i: i),
               grouped(HALO, lambda i: jnp.maximum(i * per - 1, 0)),
               grouped(HALO, lambda i: jnp.minimum((i + 1) * per, last))]
    return pl.pallas_call(
        functools.partial(_tail_kernel, ffc=ffc),
        grid=(b, s // tm),
        in_specs=with_halo(MLA_WIDTH) + f_specs + with_halo(D_MODEL) + [
            single(woa.shape), single(wof.shape), _const_spec(gpm.shape), _const_spec(gpf.shape),
            single(wg.shape), single(wu.shape), _const_spec(cw.shape), _const_spec(cb.shape),
            single(wd.shape), _const_spec(gpo.shape)],
        out_specs=pl.BlockSpec((1, tm, D_MODEL), lambda bi, i: (bi, i, 0)),
        out_shape=jax.ShapeDtypeStruct(x.shape, F32),
        scratch_shapes=[pltpu.VMEM((tm, D_FF), BF16)],
        compiler_params=_params(("parallel", "arbitrary")),
        name="tail",
    )(a, a, a, f, f, f, x, x, x, woa, wof, gpm, gpf, wg, wu, cw, cb, wd, gpo)


def _rope_tables(s):
    ang = np.arange(s, dtype=np.float64)[:, None] * (
        ROPE_THETA ** (-np.arange(0, QK_ROPE, 2, dtype=np.float64) / QK_ROPE))[None, :]
    cos, sin = np.cos(ang), np.sin(ang)
    zeros = np.zeros((s, QK_NOPE))
    zh = np.zeros((s, HALF_ROPE))
    tail = np.zeros((s, HEAD_PAD - QK_NOPE - QK_ROPE))
    cos_t = np.concatenate([np.ones((s, QK_NOPE)), cos, cos, tail], axis=1)
    sina = np.concatenate([zeros, zh, sin, tail], axis=1)
    sinb = np.concatenate([zeros, -sin, zh, tail], axis=1)
    return tuple(jnp.asarray(t, F32) for t in (cos_t, sina, sinb, cos.T, sin.T))


def _angles(num, den):
    return (2.0 * np.pi / den) * (num % den).astype(np.float64)


def _dft_tables(s):
    n2 = DFT_N2
    n1 = s // n2
    c = np.arange(FNET_GROUP_DIM, dtype=np.int64)
    ang = _angles(c[:, None] * c[None, :], FNET_GROUP_DIM)
    cs = np.concatenate([np.cos(ang), np.sin(ang)], axis=1) / math.sqrt(FNET_GROUP_DIM)
    k1 = np.arange(n1, dtype=np.int64)
    a1 = _angles(k1[:, None] * k1[None, :], n1)
    c1, s1 = np.cos(a1), np.sin(a1)
    f1 = np.concatenate([np.concatenate([c1, -s1], axis=1),
                         np.concatenate([s1, c1], axis=1)], axis=0) / math.sqrt(n1)
    k2 = np.arange(n2, dtype=np.int64)
    kk = k1[:, None, None] + n1 * k2[None, :, None]
    a2 = _angles(kk * k2[None, None, :], s)
    w = np.concatenate([np.cos(a2), -np.sin(a2)], axis=2) / math.sqrt(n2)
    return tuple(jnp.asarray(t, F32) for t in (cs, f1, w))


def _prep_weights(g_pre_mix, w_in, g_q, w_uq, g_kv, w_ukv, w_fnet, w_out, g_post_mix,
                  g_pre_ffn, w_gate, w_up, conv_w, conv_b, w_down, g_post_ffn):
    win = w_in[0]
    f_lo = Q_LORA + KV_LORA + QK_ROPE
    kr_cols = jnp.concatenate([jnp.zeros((D_MODEL, QK_NOPE), F32),
                               win[:, Q_LORA + KV_LORA:f_lo],
                               jnp.zeros((D_MODEL, HEAD_PAD - QK_NOPE - QK_ROPE), F32)], axis=1)
    win_p = jnp.concatenate([win[:, :Q_LORA + KV_LORA], win[:, f_lo:], kr_cols], axis=1)
    wuq = w_uq[0].reshape(Q_LORA, MLA_HEADS, QK_NOPE + QK_ROPE)
    wuq = jnp.pad(wuq, ((0, 0), (0, 0), (0, HEAD_PAD - QK_NOPE - QK_ROPE)))
    wuq = wuq.reshape(Q_LORA, MLA_HEADS * HEAD_PAD)
    wukv = w_ukv[0].reshape(KV_LORA, MLA_HEADS, QK_NOPE + V_HEAD)
    wuk = jnp.pad(wukv[..., :QK_NOPE], ((0, 0), (0, 0), (0, HEAD_PAD - QK_NOPE)))
    wuk = wuk.reshape(KV_LORA, MLA_HEADS * HEAD_PAD)
    wuv = wukv[..., QK_NOPE:].reshape(KV_LORA, MLA_WIDTH)
    return dict(
        gpre=g_pre_mix, win=win_p.astype(BF16), gq=g_q, wuqt=wuq.T.astype(BF16), gkv=g_kv,
        wuk=wuk.astype(BF16), wuvt=wuv.T.astype(BF16), wf=w_fnet[0].astype(BF16),
        woa=w_out[0, :MLA_WIDTH].astype(BF16), wof=w_out[0, MLA_WIDTH:].astype(BF16),
        gpm=g_post_mix, gpf=g_pre_ffn, wg=w_gate[0].astype(BF16), wu=w_up[0].astype(BF16),
        cw=conv_w[0], cb=conv_b, wd=w_down[0].astype(BF16), gpo=g_post_ffn)


def _trunk(x, w):
    b, s, _ = x.shape
    n2 = DFT_N2
    n1 = s // n2
    rope_tabs = _rope_tables(s)
    cs, f1, wtab = _dft_tables(s)
    qt, k, vt, z = _proj(x, w["gpre"], w["win"], w["gq"], w["wuqt"], w["gkv"], w["wuk"], w["wuvt"],
                         cs, rope_tabs, tm=TOKEN_TILE)
    a = _attn(qt, k, vt, tq=MXU_TILE, tk=s // SCORE_SLOTS, qsubs=ATTN_QUERY_BLOCKS)
    y = _dft1(z.reshape(b, 2, FNET_GROUPS, n1, n2, FNET_GROUP_DIM), f1, r=DFT1_ROWS)
    f = _dft2(y, wtab, w["wf"], kb=DFT2_ROWS)
    f = f.reshape(b, FNET_GROUPS, s, FNET_GROUP_DIM)
    return _tail(a, f, x, w["woa"], w["wof"], w["gpm"], w["gpf"], w["wg"], w["wu"], w["cw"],
                 w["cb"], w["wd"], w["gpo"], tm=TOKEN_TILE, ffc=FF_CHUNK)


def kernel(x_prompt, x_sample, g_pre_mix, w_in, g_q, w_uq, g_kv, w_ukv, w_fnet, w_out,
           g_post_mix, g_pre_ffn, w_gate, w_up, conv_w, conv_b, w_down, g_post_ffn):
    w = _prep_weights(g_pre_mix, w_in, g_q, w_uq, g_kv, w_ukv, w_fnet, w_out, g_post_mix,
                      g_pre_ffn, w_gate, w_up, conv_w, conv_b, w_down, g_post_ffn)
    return _trunk(x_prompt, w), _trunk(x_sample, w)
```

```python
import functools
import math

import jax
import jax.numpy as jnp
import numpy as np
from jax.experimental import pallas as pl
from jax.experimental.pallas import tpu as pltpu

D_MODEL = 1024
MLA_HEADS = 8
QK_NOPE = 64
QK_ROPE = 32
V_HEAD = 64
MLA_WIDTH = MLA_HEADS * V_HEAD
Q_LORA = 256
KV_LORA = 256
FNET_GROUPS = 4
FNET_WIDTH = 512
FNET_GROUP_DIM = 128
D_FF = 2816
ROPE_THETA = 10000.0
RMS_EPS = 1e-6
ATTN_SCALE = 1.0 / math.sqrt(QK_NOPE + QK_ROPE)
Q_SCALE = ATTN_SCALE * math.log2(math.e)

LANES = 128
HEAD_PAD = 128
HALF_ROPE = QK_ROPE // 2
DENOM_ROWS = 16
SCORE_PAD = 8
MXU_TILE = 256
SCORE_ACC = (0, 64, 128)
VALUE_ACC = (192, 224)
SCORE_AHEAD = 2
SCORE_SLOTS = 4
ATTN_QUERY_BLOCKS = 4
TOKEN_TILE = 512
FF_CHUNK = 256
DFT1_ROWS = 32
DFT2_ROWS = 16
DFT_N2 = 128
VMEM_LIMIT = 56 * 1024 * 1024

BF16 = jnp.bfloat16
F32 = jnp.float32


def _rms(x, g):
    return x * jax.lax.rsqrt(jnp.mean(x * x, axis=-1, keepdims=True) + RMS_EPS) * g


def _dot(a, b):
    return jnp.dot(a, b, preferred_element_type=F32)


def _params(sem):
    return pltpu.CompilerParams(dimension_semantics=sem, vmem_limit_bytes=VMEM_LIMIT)


def _const_spec(shape):
    zeros = (0,) * len(shape)
    return pl.BlockSpec(shape, lambda *_: zeros)


def _proj_kernel(x_ref, gpre_ref, win_ref, gq_ref, wuqt_ref, gkv_ref, wuk_ref, wuvt_ref, cs_ref,
                 cos_ref, sina_ref, sinb_ref, cost_ref, sint_ref, qt_ref, k_ref, vt_ref, z_ref):
    tm = x_ref.shape[1]
    h = _rms(x_ref[0], gpre_ref[...]).astype(BF16)
    p = _dot(h, win_ref[...])
    cq = _rms(p[:, :Q_LORA], gq_ref[...])
    qt = _dot(wuqt_ref[...], cq.T.astype(BF16))
    ckv = _rms(p[:, Q_LORA:Q_LORA + KV_LORA], gkv_ref[...])
    kn = _dot(ckv.astype(BF16), wuk_ref[...])
    vt_ref[0] = _dot(wuvt_ref[...], ckv.T.astype(BF16)).astype(BF16)
    cos, sina, sinb = cos_ref[...], sina_ref[...], sinb_ref[...]
    t = p[:, 2 * FNET_WIDTH:]
    kr = (t * cos + pltpu.roll(t, HALF_ROPE, 1) * sina
          + pltpu.roll(t, HEAD_PAD - HALF_ROPE, 1) * sinb)
    cos_t, sin_t = cost_ref[...], sint_ref[...]
    lo, mid, hi = QK_NOPE, QK_NOPE + HALF_ROPE, QK_NOPE + QK_ROPE
    for hd in range(MLA_HEADS):
        base = hd * HEAD_PAD
        x1, x2 = qt[base + lo:base + mid], qt[base + mid:base + hi]
        qt_ref[0, base:base + lo, :] = (qt[base:base + lo] * Q_SCALE).astype(BF16)
        qt_ref[0, base + lo:base + mid, :] = ((x1 * cos_t - x2 * sin_t) * Q_SCALE).astype(BF16)
        qt_ref[0, base + mid:base + hi, :] = ((x2 * cos_t + x1 * sin_t) * Q_SCALE).astype(BF16)
        qt_ref[0, base + hi:base + HEAD_PAD, :] = jnp.zeros((HEAD_PAD - hi, tm), BF16)
        sl = slice(base, base + HEAD_PAD)
        k_ref[0, :, sl] = (kn[:, sl] + kr).astype(BF16)
    f = p[:, Q_LORA + KV_LORA:2 * FNET_WIDTH].astype(BF16)
    cs = cs_ref[...].astype(BF16)
    for g in range(FNET_GROUPS):
        sl = slice(g * FNET_GROUP_DIM, (g + 1) * FNET_GROUP_DIM)
        ab = _dot(f[:, sl], cs)
        z_ref[0, 0, g] = ab[:, :FNET_GROUP_DIM]
        z_ref[0, 1, g] = ab[:, FNET_GROUP_DIM:]


def _proj(x, gpre, win, gq, wuqt, gkv, wuk, wuvt, cs, rope_tabs, tm):
    b, s, _ = x.shape
    wide = MLA_HEADS * HEAD_PAD
    tok = lambda w: pl.BlockSpec((1, tm, w), lambda bi, i: (bi, i, 0))
    tab = pl.BlockSpec((tm, HEAD_PAD), lambda bi, i: (i, 0))
    tab_t = pl.BlockSpec((HALF_ROPE, tm), lambda bi, i: (0, i))
    return pl.pallas_call(
        _proj_kernel,
        grid=(b, s // tm),
        in_specs=[tok(D_MODEL), _const_spec(gpre.shape), _const_spec(win.shape),
                  _const_spec(gq.shape), _const_spec(wuqt.shape), _const_spec(gkv.shape),
                  _const_spec(wuk.shape), _const_spec(wuvt.shape), _const_spec(cs.shape),
                  tab, tab, tab, tab_t, tab_t],
        out_specs=[pl.BlockSpec((1, wide, tm), lambda bi, i: (bi, 0, i)),
                   tok(wide),
                   pl.BlockSpec((1, MLA_WIDTH, tm), lambda bi, i: (bi, 0, i)),
                   pl.BlockSpec((1, 2, FNET_GROUPS, tm, FNET_GROUP_DIM),
                                lambda bi, i: (bi, 0, 0, i, 0))],
        out_shape=[jax.ShapeDtypeStruct((b, wide, s), BF16),
                   jax.ShapeDtypeStruct((b, s, wide), BF16),
                   jax.ShapeDtypeStruct((b, MLA_WIDTH, s), BF16),
                   jax.ShapeDtypeStruct((b, 2, FNET_GROUPS, s, FNET_GROUP_DIM), F32)],
        compiler_params=_params(("parallel", "parallel")),
        name="proj",
    )(x, gpre, win, gq, wuqt, gkv, wuk, wuvt, cs, *rope_tabs)


def _attn_kernel(qt_ref, k_ref, vt_ref, o_ref, st_all, mc_all, *, tq, tk, heads, qsubs):
    st_ref = tuple(st_all.at[k] for k in range(SCORE_SLOTS))
    mc_ref = tuple(mc_all.at[k] for k in range(SCORE_SLOTS))
    s = k_ref.shape[1]
    n = s // tk
    nt = tk // MXU_TILE
    i = pl.program_id(2)
    ones = jnp.ones((DENOM_ROWS, MXU_TILE), BF16)
    pad_k = jnp.zeros((MXU_TILE, MXU_TILE - HEAD_PAD), BF16)
    pad_q = jnp.zeros((MXU_TILE - HEAD_PAD, tq), BF16)

    def q_rhs(hd, qi):
        q_off = qi * tq if isinstance(qi, int) else pl.multiple_of(qi * tq, tq)
        qt = qt_ref[0, hd * HEAD_PAD:(hd + 1) * HEAD_PAD, pl.ds(q_off, tq)]
        return jnp.concatenate([qt, pad_q], axis=0)

    def k_lhs(hd, c, j):
        k_lo = c * tk + j * MXU_TILE
        kc = k_ref[0, k_lo:k_lo + MXU_TILE, hd * HEAD_PAD:(hd + 1) * HEAD_PAD]
        return jnp.concatenate([kc, pad_k], axis=1)

    def pop_scores(hd, j, slot, mx):
        st = pltpu.matmul_pop(SCORE_ACC[j % len(SCORE_ACC)], (MXU_TILE, tq), F32, hd)
        st_ref[slot][hd, j * MXU_TILE:(j + 1) * MXU_TILE, :] = st
        return jnp.maximum(mx, jnp.max(st, axis=0, keepdims=True))

    lag = len(SCORE_ACC) - 1

    @pl.when(i == 0)
    def _():
        for c in range(SCORE_AHEAD):
            mx = [jnp.full((1, tq), -jnp.inf, F32) for _ in range(heads)]
            for hd in range(heads):
                pltpu.matmul_push_rhs(q_rhs(hd, 0), 0, hd)
            for j in range(nt + lag):
                for hd in range(heads):
                    if j < nt:
                        pltpu.matmul_acc_lhs(SCORE_ACC[j % len(SCORE_ACC)], k_lhs(hd, c, j), hd,
                                             0 if j == 0 else None)
                    if j >= lag:
                        mx[hd] = pop_scores(hd, j - lag, c, mx[hd])
            for hd in range(heads):
                mc_ref[c][hd] = mx[hd]

    pending = []
    tile = 0
    block_accs = []

    def drain(keep):
        while len(pending) > keep:
            pending.pop(0)()

    for qsub in range(qsubs):
        qb = i * qsubs + qsub
        ms = [jnp.full((1, tq), -jnp.inf, F32) for _ in range(heads)]
        accs = [jnp.zeros((V_HEAD + DENOM_ROWS, tq), F32) for _ in range(heads)]
        block_accs.append(accs)
        for slot in range(n):
            ahead = slot + SCORE_AHEAD
            if ahead < n:
                qi, c_next = qb, ahead
            elif qsub + 1 < qsubs:
                qi, c_next = qb + 1, ahead - n
            else:
                qi, c_next = jnp.minimum(qb + 1, s // tq - 1), ahead - n
            fill = ahead % SCORE_SLOTS
            value_acc = VALUE_ACC[slot % len(VALUE_ACC)]
            m_old = list(ms)
            ms = [jnp.maximum(ms[hd], mc_ref[slot][hd]) for hd in range(heads)]
            mx = [jnp.full((1, tq), -jnp.inf, F32) for _ in range(heads)]
            for j in range(nt):
                score_acc = SCORE_ACC[tile % len(SCORE_ACC)]
                tile += 1
                for hd in range(heads):
                    rows = slice(j * MXU_TILE, (j + 1) * MXU_TILE)
                    p = jnp.exp2(st_ref[slot][hd, rows, :] - ms[hd]).astype(BF16)
                    pltpu.matmul_push_rhs(p, 1, hd)
                    pltpu.matmul_push_rhs(q_rhs(hd, qi), 0, hd)
                    v_lo = slot * tk + j * MXU_TILE
                    vc = vt_ref[0, hd * V_HEAD:(hd + 1) * V_HEAD, v_lo:v_lo + MXU_TILE]
                    pltpu.matmul_acc_lhs(value_acc, jnp.concatenate([vc, ones], axis=0), hd, 1)
                    pltpu.matmul_acc_lhs(score_acc, k_lhs(hd, c_next, j), hd, 0)

                def pop_tile(j=j, fill=fill, mx=mx, score_acc=score_acc, value_acc=value_acc,
                             m_old=m_old, m_new=list(ms), accs=accs):
                    for hd in range(heads):
                        st = pltpu.matmul_pop(score_acc, (MXU_TILE, tq), F32, hd)
                        st_ref[fill][hd, j * MXU_TILE:(j + 1) * MXU_TILE, :] = st
                        mx[hd] = jnp.maximum(mx[hd], jnp.max(st, axis=0, keepdims=True))
                        if j == nt - 1:
                            mc_ref[fill][hd] = mx[hd]
                            pv = pltpu.matmul_pop(value_acc, (V_HEAD + DENOM_ROWS, tq), F32, hd)
                            accs[hd] = jnp.exp2(m_old[hd] - m_new[hd]) * accs[hd] + pv

                drain(lag - 1)
                pending.append(pop_tile)
    drain(0)
    for qsub, accs in enumerate(block_accs):
        outs = [acc[:V_HEAD] / acc[V_HEAD:V_HEAD + 1] for acc in accs]
        o_ref[0, qsub * tq:(qsub + 1) * tq, :] = jnp.concatenate(outs, axis=0).T.astype(BF16)


def _attn(qt, k, vt, tq, tk, qsubs):
    b, _, s = qt.shape
    heads = LANES // V_HEAD
    groups = MLA_HEADS // heads
    assert s % (qsubs * tq) == 0 and s == SCORE_SLOTS * tk and tq == MXU_TILE and tk % MXU_TILE == 0
    return pl.pallas_call(
        functools.partial(_attn_kernel, tq=tq, tk=tk, heads=heads, qsubs=qsubs),
        grid=(b, groups, s // (qsubs * tq)),
        in_specs=[pl.BlockSpec((1, heads * HEAD_PAD, s), lambda bi, j, i: (bi, j, 0)),
                  pl.BlockSpec((1, s, heads * HEAD_PAD), lambda bi, j, i: (bi, 0, j)),
                  pl.BlockSpec((1, heads * V_HEAD, s), lambda bi, j, i: (bi, j, 0))],
        out_specs=pl.BlockSpec((1, qsubs * tq, heads * V_HEAD), lambda bi, j, i: (bi, i, j)),
        out_shape=jax.ShapeDtypeStruct((b, s, MLA_WIDTH), BF16),
        scratch_shapes=[pltpu.VMEM((SCORE_SLOTS, heads, tk + SCORE_PAD, tq), F32),
                        pltpu.VMEM((SCORE_SLOTS, heads, 1, tq), F32)],
        compiler_params=_params(("parallel", "parallel", "arbitrary")),
        name="attn",
    )(qt, k, vt)


def _dft1_kernel(z_ref, f1_ref, y_ref):
    groups, n1, r, c = z_ref.shape[2:]
    f1 = f1_ref[...].astype(BF16)
    flat = lambda ref, part, g: ref.at[0, part, g].reshape(n1 * r, c)
    for j in range(r):
        rows = pl.ds(j, n1, stride=r)
        gather = lambda part: jnp.concatenate(
            [flat(z_ref, part, g)[rows, :] for g in range(groups)], axis=1).astype(BF16)
        y = _dot(f1[:, :n1], gather(0)) + _dot(f1[:, n1:], gather(1))
        for g in range(groups):
            flat(y_ref, 0, g)[rows, :] = y[:n1, g * c:(g + 1) * c]
            flat(y_ref, 1, g)[rows, :] = y[n1:, g * c:(g + 1) * c]


def _dft1(z, f1, r):
    b, _, groups, n1, n2, c = z.shape
    spec = pl.BlockSpec((1, 2, groups, n1, r, c), lambda bi, j: (bi, 0, 0, 0, j, 0))
    return pl.pallas_call(
        _dft1_kernel,
        grid=(b, n2 // r),
        in_specs=[spec, _const_spec(f1.shape)],
        out_specs=spec,
        out_shape=jax.ShapeDtypeStruct(z.shape, F32),
        compiler_params=_params(("parallel", "parallel")),
        name="dft1",
    )(z, f1)


def _dft2_kernel(y_ref, w_ref, wf_ref, o_ref):
    groups, kb, n2, c = y_ref.shape[2:]
    for t in range(kb):
        w = w_ref[t].astype(BF16)
        part = lambda p: jnp.concatenate(
            [y_ref[0, p, g, t] for g in range(groups)], axis=1).astype(BF16)
        xr = (_dot(w[:, :n2], part(0)) + _dot(w[:, n2:], part(1))).astype(BF16)
        for g in range(groups):
            rows = o_ref.at[0, g].reshape(n2 * kb, c)
            rows[pl.ds(t, n2, stride=kb), :] = _dot(xr[:, g * c:(g + 1) * c], wf_ref[g])


def _dft2(y, w, wf, kb):
    b, _, groups, n1, n2, c = y.shape
    return pl.pallas_call(
        _dft2_kernel,
        grid=(b, n1 // kb),
        in_specs=[pl.BlockSpec((1, 2, groups, kb, n2, c), lambda bi, j: (bi, 0, 0, j, 0, 0)),
                  pl.BlockSpec((kb, n2, 2 * n2), lambda bi, j: (j, 0, 0)),
                  _const_spec(wf.shape)],
        out_specs=pl.BlockSpec((1, groups, n2, kb, c), lambda bi, j: (bi, 0, 0, j, 0)),
        out_shape=jax.ShapeDtypeStruct((b, groups, n2, n1, c), F32),
        compiler_params=_params(("parallel", "parallel")),
        name="dft2",
    )(y, w, wf)


HALO = 16


def _tail_kernel(a_ref, ap_ref, an_ref, f_ref, fp_ref, fn_ref, x_ref, xp_ref, xn_ref,
                 woa_ref, wof_ref, gpm_ref, gpf_ref, wg_ref, wu_ref, cw_ref, cb_ref, wd_ref,
                 gpo_ref, y_ref, act_ref, *, ffc):
    i = pl.program_id(1)
    tm = x_ref.shape[1]
    rows = tm + 2 * HALO
    ext = lambda p, c, n: jnp.concatenate([p[0], c[0], n[0]], axis=0)
    f_ext = jnp.concatenate(
        [jnp.concatenate([fp_ref[0, g], f_ref[0, g], fn_ref[0, g]], axis=0)
         for g in range(FNET_GROUPS)], axis=1).astype(BF16)
    mix = _dot(ext(ap_ref, a_ref, an_ref), woa_ref[...]) + _dot(f_ext, wof_ref[...])
    x1 = ext(xp_ref, x_ref, xn_ref) + _rms(mix, gpm_ref[...])
    row = jax.lax.broadcasted_iota(jnp.int32, (rows, 1), 0)
    inside = ((row >= HALO) | (i > 0)) & ((row < HALO + tm) | (i < pl.num_programs(1) - 1))
    hext = jnp.where(inside, _rms(x1, gpf_ref[...]), 0.0).astype(BF16)
    h = hext[HALO:HALO + tm]
    for c in range(D_FF // ffc):
        sl = slice(c * ffc, (c + 1) * ffc)
        g = _dot(hext, wg_ref[:, sl])
        u = _dot(h, wu_ref[:, sl])
        g_prev = pltpu.roll(g, 1, 0)[HALO:HALO + tm]
        g_next = pltpu.roll(g, rows - 1, 0)[HALO:HALO + tm]
        gate = (cb_ref[:, sl] + g_prev * cw_ref[0:1, sl] + g[HALO:HALO + tm] * cw_ref[1:2, sl]
                + g_next * cw_ref[2:3, sl])
        inner = math.sqrt(2.0 / math.pi) * (gate + 0.044715 * (gate * gate * gate))
        act = 0.5 * gate * (1.0 + jnp.tanh(inner)) * u
        act_ref[:, sl] = act.astype(BF16)
    out = _dot(act_ref[...], wd_ref[...])
    y_ref[0] = x1[HALO:HALO + tm] + _rms(out, gpo_ref[...])


def _tail(a, f, x, woa, wof, gpm, gpf, wg, wu, cw, cb, wd, gpo, tm, ffc):
    b, s, _ = x.shape
    per = tm // HALO
    last = s // HALO - 1
    single = lambda shape: pl.BlockSpec(shape, lambda *_: (0,) * len(shape),
                                        pipeline_mode=pl.Buffered(1))

    def with_halo(width):
        return [pl.BlockSpec((1, tm, width), lambda bi, i: (bi, i, 0)),
                pl.BlockSpec((1, HALO, width), lambda bi, i: (bi, jnp.maximum(i * per - 1, 0), 0)),
                pl.BlockSpec((1, HALO, width), lambda bi, i: (bi, jnp.minimum((i + 1) * per, last), 0))]

    grouped = lambda rows, row_block: pl.BlockSpec(
        (1, FNET_GROUPS, rows, FNET_GROUP_DIM), lambda bi, i: (bi, 0, row_block(i), 0))
    f_specs = [grouped(tm, lambda i: i),
               grouped(HALO, lambda i: jnp.maximum(i * per - 1, 0)),
               grouped(HALO, lambda i: jnp.minimum((i + 1) * per, last))]
    return pl.pallas_call(
        functools.partial(_tail_kernel, ffc=ffc),
        grid=(b, s // tm),
        in_specs=with_halo(MLA_WIDTH) + f_specs + with_halo(D_MODEL) + [
            single(woa.shape), single(wof.shape), _const_spec(gpm.shape), _const_spec(gpf.shape),
            single(wg.shape), single(wu.shape), _const_spec(cw.shape), _const_spec(cb.shape),
            single(wd.shape), _const_spec(gpo.shape)],
        out_specs=pl.BlockSpec((1, tm, D_MODEL), lambda bi, i: (bi, i, 0)),
        out_shape=jax.ShapeDtypeStruct(x.shape, F32),
        scratch_shapes=[pltpu.VMEM((tm, D_FF), BF16)],
        compiler_params=_params(("parallel", "arbitrary")),
        name="tail",
    )(a, a, a, f, f, f, x, x, x, woa, wof, gpm, gpf, wg, wu, cw, cb, wd, gpo)


def _rope_tables(s):
    ang = np.arange(s, dtype=np.float64)[:, None] * (
        ROPE_THETA ** (-np.arange(0, QK_ROPE, 2, dtype=np.float64) / QK_ROPE))[None, :]
    cos, sin = np.cos(ang), np.sin(ang)
    zeros = np.zeros((s, QK_NOPE))
    zh = np.zeros((s, HALF_ROPE))
    tail = np.zeros((s, HEAD_PAD - QK_NOPE - QK_ROPE))
    cos_t = np.concatenate([np.ones((s, QK_NOPE)), cos, cos, tail], axis=1)
    sina = np.concatenate([zeros, zh, sin, tail], axis=1)
    sinb = np.concatenate([zeros, -sin, zh, tail], axis=1)
    return tuple(jnp.asarray(t, F32) for t in (cos_t, sina, sinb, cos.T, sin.T))


def _angles(num, den):
    return (2.0 * np.pi / den) * (num % den).astype(np.float64)


def _dft_tables(s):
    n2 = DFT_N2
    n1 = s // n2
    c = np.arange(FNET_GROUP_DIM, dtype=np.int64)
    ang = _angles(c[:, None] * c[None, :], FNET_GROUP_DIM)
    cs = np.concatenate([np.cos(ang), np.sin(ang)], axis=1) / math.sqrt(FNET_GROUP_DIM)
    k1 = np.arange(n1, dtype=np.int64)
    a1 = _angles(k1[:, None] * k1[None, :], n1)
    c1, s1 = np.cos(a1), np.sin(a1)
    f1 = np.concatenate([np.concatenate([c1, -s1], axis=1),
                         np.concatenate([s1, c1], axis=1)], axis=0) / math.sqrt(n1)
    k2 = np.arange(n2, dtype=np.int64)
    kk = k1[:, None, None] + n1 * k2[None, :, None]
    a2 = _angles(kk * k2[None, None, :], s)
    w = np.concatenate([np.cos(a2), -np.sin(a2)], axis=2) / math.sqrt(n2)
    return tuple(jnp.asarray(t, F32) for t in (cs, f1, w))


def _prep_weights(g_pre_mix, w_in, g_q, w_uq, g_kv, w_ukv, w_fnet, w_out, g_post_mix,
                  g_pre_ffn, w_gate, w_up, conv_w, conv_b, w_down, g_post_ffn):
    win = w_in[0]
    f_lo = Q_LORA + KV_LORA + QK_ROPE
    kr_cols = jnp.concatenate([jnp.zeros((D_MODEL, QK_NOPE), F32),
                               win[:, Q_LORA + KV_LORA:f_lo],
                               jnp.zeros((D_MODEL, HEAD_PAD - QK_NOPE - QK_ROPE), F32)], axis=1)
    win_p = jnp.concatenate([win[:, :Q_LORA + KV_LORA], win[:, f_lo:], kr_cols], axis=1)
    wuq = w_uq[0].reshape(Q_LORA, MLA_HEADS, QK_NOPE + QK_ROPE)
    wuq = jnp.pad(wuq, ((0, 0), (0, 0), (0, HEAD_PAD - QK_NOPE - QK_ROPE)))
    wuq = wuq.reshape(Q_LORA, MLA_HEADS * HEAD_PAD)
    wukv = w_ukv[0].reshape(KV_LORA, MLA_HEADS, QK_NOPE + V_HEAD)
    wuk = jnp.pad(wukv[..., :QK_NOPE], ((0, 0), (0, 0), (0, HEAD_PAD - QK_NOPE)))
    wuk = wuk.reshape(KV_LORA, MLA_HEADS * HEAD_PAD)
    wuv = wukv[..., QK_NOPE:].reshape(KV_LORA, MLA_WIDTH)
    return dict(
        gpre=g_pre_mix, win=win_p.astype(BF16), gq=g_q, wuqt=wuq.T.astype(BF16), gkv=g_kv,
        wuk=wuk.astype(BF16), wuvt=wuv.T.astype(BF16), wf=w_fnet[0].astype(BF16),
        woa=w_out[0, :MLA_WIDTH].astype(BF16), wof=w_out[0, MLA_WIDTH:].astype(BF16),
        gpm=g_post_mix, gpf=g_pre_ffn, wg=w_gate[0].astype(BF16), wu=w_up[0].astype(BF16),
        cw=conv_w[0], cb=conv_b, wd=w_down[0].astype(BF16), gpo=g_post_ffn)


def _trunk(x, w):
    b, s, _ = x.shape
    n2 = DFT_N2
    n1 = s // n2
    rope_tabs = _rope_tables(s)
    cs, f1, wtab = _dft_tables(s)
    qt, k, vt, z = _proj(x, w["gpre"], w["win"], w["gq"], w["wuqt"], w["gkv"], w["wuk"], w["wuvt"],
                         cs, rope_tabs, tm=TOKEN_TILE)
    a = _attn(qt, k, vt, tq=MXU_TILE, tk=s // SCORE_SLOTS, qsubs=ATTN_QUERY_BLOCKS)
    y = _dft1(z.reshape(b, 2, FNET_GROUPS, n1, n2, FNET_GROUP_DIM), f1, r=DFT1_ROWS)
    f = _dft2(y, wtab, w["wf"], kb=DFT2_ROWS)
    f = f.reshape(b, FNET_GROUPS, s, FNET_GROUP_DIM)
    return _tail(a, f, x, w["woa"], w["wof"], w["gpm"], w["gpf"], w["wg"], w["wu"], w["cw"],
                 w["cb"], w["wd"], w["gpo"], tm=TOKEN_TILE, ffc=FF_CHUNK)


def kernel(x_prompt, x_sample, g_pre_mix, w_in, g_q, w_uq, g_kv, w_ukv, w_fnet, w_out,
           g_post_mix, g_pre_ffn, w_gate, w_up, conv_w, conv_b, w_down, g_post_ffn):
    w = _prep_weights(g_pre_mix, w_in, g_q, w_uq, g_kv, w_ukv, w_fnet, w_out, g_post_mix,
                      g_pre_ffn, w_gate, w_up, conv_w, conv_b, w_down, g_post_ffn)
    return _trunk(x_prompt, w), _trunk(x_sample, w)
```

```python
import functools
import math

import jax
import jax.numpy as jnp
import numpy as np
from jax.experimental import pallas as pl
from jax.experimental.pallas import tpu as pltpu

D_MODEL = 1024
MLA_HEADS = 8
QK_NOPE = 64
QK_ROPE = 32
V_HEAD = 64
MLA_WIDTH = MLA_HEADS * V_HEAD
Q_LORA = 256
KV_LORA = 256
FNET_GROUPS = 4
FNET_WIDTH = 512
FNET_GROUP_DIM = 128
D_FF = 2816
ROPE_THETA = 10000.0
RMS_EPS = 1e-6
ATTN_SCALE = 1.0 / math.sqrt(QK_NOPE + QK_ROPE)
Q_SCALE = ATTN_SCALE * math.log2(math.e)

LANES = 128
HEAD_PAD = 128
HALF_ROPE = QK_ROPE // 2
DENOM_ROWS = 16
SCORE_PAD = 8
MXU_TILE = 256
SCORE_ACC = (0, 64, 128)
VALUE_ACC = (192, 224)
SCORE_AHEAD = 2
SCORE_SLOTS = 4
ATTN_QUERY_BLOCKS = 4
TOKEN_TILE = 512
FF_CHUNK = 256
DFT1_ROWS = 8
DFT2_ROWS = 8
DFT_N2 = 128
VMEM_LIMIT = 56 * 1024 * 1024

BF16 = jnp.bfloat16
F32 = jnp.float32


def _rms(x, g):
    return x * jax.lax.rsqrt(jnp.mean(x * x, axis=-1, keepdims=True) + RMS_EPS) * g


def _dot(a, b):
    return jnp.dot(a, b, preferred_element_type=F32)


def _params(sem):
    return pltpu.CompilerParams(dimension_semantics=sem, vmem_limit_bytes=VMEM_LIMIT)


def _const_spec(shape):
    zeros = (0,) * len(shape)
    return pl.BlockSpec(shape, lambda *_: zeros)


def _proj_kernel(x_ref, gpre_ref, win_ref, gq_ref, wuqt_ref, gkv_ref, wuk_ref, wuvt_ref, cs_ref,
                 cos_ref, sina_ref, sinb_ref, cost_ref, sint_ref, qt_ref, k_ref, vt_ref, z_ref):
    tm = x_ref.shape[1]
    h = _rms(x_ref[0], gpre_ref[...]).astype(BF16)
    p = _dot(h, win_ref[...])
    cq = _rms(p[:, :Q_LORA], gq_ref[...])
    qt = _dot(wuqt_ref[...], cq.T.astype(BF16))
    ckv = _rms(p[:, Q_LORA:Q_LORA + KV_LORA], gkv_ref[...])
    kn = _dot(ckv.astype(BF16), wuk_ref[...])
    vt_ref[0] = _dot(wuvt_ref[...], ckv.T.astype(BF16)).astype(BF16)
    cos, sina, sinb = cos_ref[...], sina_ref[...], sinb_ref[...]
    t = p[:, 2 * FNET_WIDTH:]
    kr = (t * cos + pltpu.roll(t, HALF_ROPE, 1) * sina
          + pltpu.roll(t, HEAD_PAD - HALF_ROPE, 1) * sinb)
    cos_t, sin_t = cost_ref[...], sint_ref[...]
    lo, mid, hi = QK_NOPE, QK_NOPE + HALF_ROPE, QK_NOPE + QK_ROPE
    for hd in range(MLA_HEADS):
        base = hd * HEAD_PAD
        x1, x2 = qt[base + lo:base + mid], qt[base + mid:base + hi]
        qt_ref[0, base:base + lo, :] = (qt[base:base + lo] * Q_SCALE).astype(BF16)
        qt_ref[0, base + lo:base + mid, :] = ((x1 * cos_t - x2 * sin_t) * Q_SCALE).astype(BF16)
        qt_ref[0, base + mid:base + hi, :] = ((x2 * cos_t + x1 * sin_t) * Q_SCALE).astype(BF16)
        qt_ref[0, base + hi:base + HEAD_PAD, :] = jnp.zeros((HEAD_PAD - hi, tm), BF16)
        sl = slice(base, base + HEAD_PAD)
        k_ref[0, :, sl] = (kn[:, sl] + kr).astype(BF16)
    f = p[:, Q_LORA + KV_LORA:2 * FNET_WIDTH].astype(BF16)
    cs = cs_ref[...].astype(BF16)
    for g in range(FNET_GROUPS):
        sl = slice(g * FNET_GROUP_DIM, (g + 1) * FNET_GROUP_DIM)
        ab = _dot(f[:, sl], cs)
        z_ref[0, 0, g] = ab[:, :FNET_GROUP_DIM]
        z_ref[0, 1, g] = ab[:, FNET_GROUP_DIM:]


def _proj(x, gpre, win, gq, wuqt, gkv, wuk, wuvt, cs, rope_tabs, tm):
    b, s, _ = x.shape
    wide = MLA_HEADS * HEAD_PAD
    tok = lambda w: pl.BlockSpec((1, tm, w), lambda bi, i: (bi, i, 0))
    tab = pl.BlockSpec((tm, HEAD_PAD), lambda bi, i: (i, 0))
    tab_t = pl.BlockSpec((HALF_ROPE, tm), lambda bi, i: (0, i))
    return pl.pallas_call(
        _proj_kernel,
        grid=(b, s // tm),
        in_specs=[tok(D_MODEL), _const_spec(gpre.shape), _const_spec(win.shape),
                  _const_spec(gq.shape), _const_spec(wuqt.shape), _const_spec(gkv.shape),
                  _const_spec(wuk.shape), _const_spec(wuvt.shape), _const_spec(cs.shape),
                  tab, tab, tab, tab_t, tab_t],
        out_specs=[pl.BlockSpec((1, wide, tm), lambda bi, i: (bi, 0, i)),
                   tok(wide),
                   pl.BlockSpec((1, MLA_WIDTH, tm), lambda bi, i: (bi, 0, i)),
                   pl.BlockSpec((1, 2, FNET_GROUPS, tm, FNET_GROUP_DIM),
                                lambda bi, i: (bi, 0, 0, i, 0))],
        out_shape=[jax.ShapeDtypeStruct((b, wide, s), BF16),
                   jax.ShapeDtypeStruct((b, s, wide), BF16),
                   jax.ShapeDtypeStruct((b, MLA_WIDTH, s), BF16),
                   jax.ShapeDtypeStruct((b, 2, FNET_GROUPS, s, FNET_GROUP_DIM), F32)],
        compiler_params=_params(("parallel", "parallel")),
        name="proj",
    )(x, gpre, win, gq, wuqt, gkv, wuk, wuvt, cs, *rope_tabs)


def _attn_kernel(qt_ref, k_ref, vt_ref, o_ref, st_all, mc_all, *, tq, tk, heads, qsubs):
    st_ref = tuple(st_all.at[k] for k in range(SCORE_SLOTS))
    mc_ref = tuple(mc_all.at[k] for k in range(SCORE_SLOTS))
    s = k_ref.shape[1]
    n = s // tk
    nt = tk // MXU_TILE
    i = pl.program_id(2)
    ones = jnp.ones((DENOM_ROWS, MXU_TILE), BF16)
    pad_k = jnp.zeros((MXU_TILE, MXU_TILE - HEAD_PAD), BF16)
    pad_q = jnp.zeros((MXU_TILE - HEAD_PAD, tq), BF16)

    def q_rhs(hd, qi):
        q_off = qi * tq if isinstance(qi, int) else pl.multiple_of(qi * tq, tq)
        qt = qt_ref[0, hd * HEAD_PAD:(hd + 1) * HEAD_PAD, pl.ds(q_off, tq)]
        return jnp.concatenate([qt, pad_q], axis=0)

    def k_lhs(hd, c, j):
        k_lo = c * tk + j * MXU_TILE
        kc = k_ref[0, k_lo:k_lo + MXU_TILE, hd * HEAD_PAD:(hd + 1) * HEAD_PAD]
        return jnp.concatenate([kc, pad_k], axis=1)

    def pop_scores(hd, j, slot, mx):
        st = pltpu.matmul_pop(SCORE_ACC[j % len(SCORE_ACC)], (MXU_TILE, tq), F32, hd)
        st_ref[slot][hd, j * MXU_TILE:(j + 1) * MXU_TILE, :] = st
        return jnp.maximum(mx, jnp.max(st, axis=0, keepdims=True))

    lag = len(SCORE_ACC) - 1

    @pl.when(i == 0)
    def _():
        for c in range(SCORE_AHEAD):
            mx = [jnp.full((1, tq), -jnp.inf, F32) for _ in range(heads)]
            for hd in range(heads):
                pltpu.matmul_push_rhs(q_rhs(hd, 0), 0, hd)
            for j in range(nt + lag):
                for hd in range(heads):
                    if j < nt:
                        pltpu.matmul_acc_lhs(SCORE_ACC[j % len(SCORE_ACC)], k_lhs(hd, c, j), hd,
                                             0 if j == 0 else None)
                    if j >= lag:
                        mx[hd] = pop_scores(hd, j - lag, c, mx[hd])
            for hd in range(heads):
                mc_ref[c][hd] = mx[hd]

    pending = []
    tile = 0
    block_accs = []

    def drain(keep):
        while len(pending) > keep:
            pending.pop(0)()

    for qsub in range(qsubs):
        qb = i * qsubs + qsub
        ms = [jnp.full((1, tq), -jnp.inf, F32) for _ in range(heads)]
        accs = [jnp.zeros((V_HEAD + DENOM_ROWS, tq), F32) for _ in range(heads)]
        block_accs.append(accs)
        for slot in range(n):
            ahead = slot + SCORE_AHEAD
            if ahead < n:
                qi, c_next = qb, ahead
            elif qsub + 1 < qsubs:
                qi, c_next = qb + 1, ahead - n
            else:
                qi, c_next = jnp.minimum(qb + 1, s // tq - 1), ahead - n
            fill = ahead % SCORE_SLOTS
            value_acc = VALUE_ACC[slot % len(VALUE_ACC)]
            m_old = list(ms)
            ms = [jnp.maximum(ms[hd], mc_ref[slot][hd]) for hd in range(heads)]
            mx = [jnp.full((1, tq), -jnp.inf, F32) for _ in range(heads)]
            for j in range(nt):
                score_acc = SCORE_ACC[tile % len(SCORE_ACC)]
                tile += 1
                for hd in range(heads):
                    rows = slice(j * MXU_TILE, (j + 1) * MXU_TILE)
                    p = jnp.exp2(st_ref[slot][hd, rows, :] - ms[hd]).astype(BF16)
                    pltpu.matmul_push_rhs(p, 1, hd)
                    pltpu.matmul_push_rhs(q_rhs(hd, qi), 0, hd)
                    v_lo = slot * tk + j * MXU_TILE
                    vc = vt_ref[0, hd * V_HEAD:(hd + 1) * V_HEAD, v_lo:v_lo + MXU_TILE]
                    pltpu.matmul_acc_lhs(value_acc, jnp.concatenate([vc, ones], axis=0), hd, 1)
                    pltpu.matmul_acc_lhs(score_acc, k_lhs(hd, c_next, j), hd, 0)

                def pop_tile(j=j, fill=fill, mx=mx, score_acc=score_acc, value_acc=value_acc,
                             m_old=m_old, m_new=list(ms), accs=accs):
                    for hd in range(heads):
                        st = pltpu.matmul_pop(score_acc, (MXU_TILE, tq), F32, hd)
                        st_ref[fill][hd, j * MXU_TILE:(j + 1) * MXU_TILE, :] = st
                        mx[hd] = jnp.maximum(mx[hd], jnp.max(st, axis=0, keepdims=True))
                        if j == nt - 1:
                            mc_ref[fill][hd] = mx[hd]
                            pv = pltpu.matmul_pop(value_acc, (V_HEAD + DENOM_ROWS, tq), F32, hd)
                            accs[hd] = jnp.exp2(m_old[hd] - m_new[hd]) * accs[hd] + pv

                drain(lag - 1)
                pending.append(pop_tile)
    drain(0)
    for qsub, accs in enumerate(block_accs):
        outs = [acc[:V_HEAD] / acc[V_HEAD:V_HEAD + 1] for acc in accs]
        o_ref[0, qsub * tq:(qsub + 1) * tq, :] = jnp.concatenate(outs, axis=0).T.astype(BF16)


def _attn(qt, k, vt, tq, tk, qsubs):
    b, _, s = qt.shape
    heads = LANES // V_HEAD
    groups = MLA_HEADS // heads
    assert s % (qsubs * tq) == 0 and s == SCORE_SLOTS * tk and tq == MXU_TILE and tk % MXU_TILE == 0
    return pl.pallas_call(
        functools.partial(_attn_kernel, tq=tq, tk=tk, heads=heads, qsubs=qsubs),
        grid=(b, groups, s // (qsubs * tq)),
        in_specs=[pl.BlockSpec((1, heads * HEAD_PAD, s), lambda bi, j, i: (bi, j, 0)),
                  pl.BlockSpec((1, s, heads * HEAD_PAD), lambda bi, j, i: (bi, 0, j)),
                  pl.BlockSpec((1, heads * V_HEAD, s), lambda bi, j, i: (bi, j, 0))],
        out_specs=pl.BlockSpec((1, qsubs * tq, heads * V_HEAD), lambda bi, j, i: (bi, i, j)),
        out_shape=jax.ShapeDtypeStruct((b, s, MLA_WIDTH), BF16),
        scratch_shapes=[pltpu.VMEM((SCORE_SLOTS, heads, tk + SCORE_PAD, tq), F32),
                        pltpu.VMEM((SCORE_SLOTS, heads, 1, tq), F32)],
        compiler_params=_params(("parallel", "parallel", "arbitrary")),
        name="attn",
    )(qt, k, vt)


def _dft1_kernel(z_ref, f1_ref, y_ref):
    groups, n1, r, c = z_ref.shape[2:]
    f1 = f1_ref[...].astype(BF16)
    flat = lambda ref, part, g: ref.at[0, part, g].reshape(n1 * r, c)
    for j in range(r):
        rows = pl.ds(j, n1, stride=r)
        gather = lambda part: jnp.concatenate(
            [flat(z_ref, part, g)[rows, :] for g in range(groups)], axis=1).astype(BF16)
        y = _dot(f1[:, :n1], gather(0)) + _dot(f1[:, n1:], gather(1))
        for g in range(groups):
            flat(y_ref, 0, g)[rows, :] = y[:n1, g * c:(g + 1) * c]
            flat(y_ref, 1, g)[rows, :] = y[n1:, g * c:(g + 1) * c]


def _dft1(z, f1, r):
    b, _, groups, n1, n2, c = z.shape
    spec = pl.BlockSpec((1, 2, groups, n1, r, c), lambda bi, j: (bi, 0, 0, 0, j, 0))
    return pl.pallas_call(
        _dft1_kernel,
        grid=(b, n2 // r),
        in_specs=[spec, _const_spec(f1.shape)],
        out_specs=spec,
        out_shape=jax.ShapeDtypeStruct(z.shape, F32),
        compiler_params=_params(("parallel", "parallel")),
        name="dft1",
    )(z, f1)


def _dft2_kernel(y_ref, w_ref, wf_ref, o_ref):
    groups, kb, n2, c = y_ref.shape[2:]
    for t in range(kb):
        w = w_ref[t].astype(BF16)
        part = lambda p: jnp.concatenate(
            [y_ref[0, p, g, t] for g in range(groups)], axis=1).astype(BF16)
        xr = (_dot(w[:, :n2], part(0)) + _dot(w[:, n2:], part(1))).astype(BF16)
        for g in range(groups):
            rows = o_ref.at[0, g].reshape(n2 * kb, c)
            rows[pl.ds(t, n2, stride=kb), :] = _dot(xr[:, g * c:(g + 1) * c], wf_ref[g])


def _dft2(y, w, wf, kb):
    b, _, groups, n1, n2, c = y.shape
    return pl.pallas_call(
        _dft2_kernel,
        grid=(b, n1 // kb),
        in_specs=[pl.BlockSpec((1, 2, groups, kb, n2, c), lambda bi, j: (bi, 0, 0, j, 0, 0)),
                  pl.BlockSpec((kb, n2, 2 * n2), lambda bi, j: (j, 0, 0)),
                  _const_spec(wf.shape)],
        out_specs=pl.BlockSpec((1, groups, n2, kb, c), lambda bi, j: (bi, 0, 0, j, 0)),
        out_shape=jax.ShapeDtypeStruct((b, groups, n2, n1, c), F32),
        compiler_params=_params(("parallel", "parallel")),
        name="dft2",
    )(y, w, wf)


HALO = 16


def _tail_kernel(a_ref, ap_ref, an_ref, f_ref, fp_ref, fn_ref, x_ref, xp_ref, xn_ref,
                 woa_ref, wof_ref, gpm_ref, gpf_ref, wg_ref, wu_ref, cw_ref, cb_ref, wd_ref,
                 gpo_ref, y_ref, act_ref, *, ffc):
    i = pl.program_id(1)
    tm = x_ref.shape[1]
    rows = tm + 2 * HALO
    ext = lambda p, c, n: jnp.concatenate([p[0], c[0], n[0]], axis=0)
    f_ext = jnp.concatenate(
        [jnp.concatenate([fp_ref[0, g], f_ref[0, g], fn_ref[0, g]], axis=0)
         for g in range(FNET_GROUPS)], axis=1).astype(BF16)
    mix = _dot(ext(ap_ref, a_ref, an_ref), woa_ref[...]) + _dot(f_ext, wof_ref[...])
    x1 = ext(xp_ref, x_ref, xn_ref) + _rms(mix, gpm_ref[...])
    row = jax.lax.broadcasted_iota(jnp.int32, (rows, 1), 0)
    inside = ((row >= HALO) | (i > 0)) & ((row < HALO + tm) | (i < pl.num_programs(1) - 1))
    hext = jnp.where(inside, _rms(x1, gpf_ref[...]), 0.0).astype(BF16)
    h = hext[HALO:HALO + tm]
    for c in range(D_FF // ffc):
        sl = slice(c * ffc, (c + 1) * ffc)
        g = _dot(hext, wg_ref[:, sl])
        u = _dot(h, wu_ref[:, sl])
        g_prev = pltpu.roll(g, 1, 0)[HALO:HALO + tm]
        g_next = pltpu.roll(g, rows - 1, 0)[HALO:HALO + tm]
        gate = (cb_ref[:, sl] + g_prev * cw_ref[0:1, sl] + g[HALO:HALO + tm] * cw_ref[1:2, sl]
                + g_next * cw_ref[2:3, sl])
        inner = math.sqrt(2.0 / math.pi) * (gate + 0.044715 * (gate * gate * gate))
        act = 0.5 * gate * (1.0 + jnp.tanh(inner)) * u
        act_ref[:, sl] = act.astype(BF16)
    out = _dot(act_ref[...], wd_ref[...])
    y_ref[0] = x1[HALO:HALO + tm] + _rms(out, gpo_ref[...])


def _tail(a, f, x, woa, wof, gpm, gpf, wg, wu, cw, cb, wd, gpo, tm, ffc):
    b, s, _ = x.shape
    per = tm // HALO
    last = s // HALO - 1
    single = lambda shape: pl.BlockSpec(shape, lambda *_: (0,) * len(shape),
                                        pipeline_mode=pl.Buffered(1))

    def with_halo(width):
        return [pl.BlockSpec((1, tm, width), lambda bi, i: (bi, i, 0)),
                pl.BlockSpec((1, HALO, width), lambda bi, i: (bi, jnp.maximum(i * per - 1, 0), 0)),
                pl.BlockSpec((1, HALO, width), lambda bi, i: (bi, jnp.minimum((i + 1) * per, last), 0))]

    grouped = lambda rows, row_block: pl.BlockSpec(
        (1, FNET_GROUPS, rows, FNET_GROUP_DIM), lambda bi, i: (bi, 0, row_block(i), 0))
    f_specs = [grouped(tm, lambda i: i),
               grouped(HALO, lambda i: jnp.maximum(i * per - 1, 0)),
               grouped(HALO, lambda i: jnp.minimum((i + 1) * per, last))]
    return pl.pallas_call(
        functools.partial(_tail_kernel, ffc=ffc),
        grid=(b, s // tm),
        in_specs=with_halo(MLA_WIDTH) + f_specs + with_halo(D_MODEL) + [
            single(woa.shape), single(wof.shape), _const_spec(gpm.shape), _const_spec(gpf.shape),
            single(wg.shape), single(wu.shape), _const_spec(cw.shape), _const_spec(cb.shape),
            single(wd.shape), _const_spec(gpo.shape)],
        out_specs=pl.BlockSpec((1, tm, D_MODEL), lambda bi, i: (bi, i, 0)),
        out_shape=jax.ShapeDtypeStruct(x.shape, F32),
        scratch_shapes=[pltpu.VMEM((tm, D_FF), BF16)],
        compiler_params=_params(("parallel", "arbitrary")),
        name="tail",
    )(a, a, a, f, f, f, x, x, x, woa, wof, gpm, gpf, wg, wu, cw, cb, wd, gpo)


def _rope_tables(s):
    ang = np.arange(s, dtype=np.float64)[:, None] * (
        ROPE_THETA ** (-np.arange(0, QK_ROPE, 2, dtype=np.float64) / QK_ROPE))[None, :]
    cos, sin = np.cos(ang), np.sin(ang)
    zeros = np.zeros((s, QK_NOPE))
    zh = np.zeros((s, HALF_ROPE))
    tail = np.zeros((s, HEAD_PAD - QK_NOPE - QK_ROPE))
    cos_t = np.concatenate([np.ones((s, QK_NOPE)), cos, cos, tail], axis=1)
    sina = np.concatenate([zeros, zh, sin, tail], axis=1)
    sinb = np.concatenate([zeros, -sin, zh, tail], axis=1)
    return tuple(jnp.asarray(t, F32) for t in (cos_t, sina, sinb, cos.T, sin.T))


def _angles(num, den):
    return (2.0 * np.pi / den) * (num % den).astype(np.float64)


def _dft_tables(s):
    n2 = DFT_N2
    n1 = s // n2
    c = np.arange(FNET_GROUP_DIM, dtype=np.int64)
    ang = _angles(c[:, None] * c[None, :], FNET_GROUP_DIM)
    cs = np.concatenate([np.cos(ang), np.sin(ang)], axis=1) / math.sqrt(FNET_GROUP_DIM)
    k1 = np.arange(n1, dtype=np.int64)
    a1 = _angles(k1[:, None] * k1[None, :], n1)
    c1, s1 = np.cos(a1), np.sin(a1)
    f1 = np.concatenate([np.concatenate([c1, -s1], axis=1),
                         np.concatenate([s1, c1], axis=1)], axis=0) / math.sqrt(n1)
    k2 = np.arange(n2, dtype=np.int64)
    kk = k1[:, None, None] + n1 * k2[None, :, None]
    a2 = _angles(kk * k2[None, None, :], s)
    w = np.concatenate([np.cos(a2), -np.sin(a2)], axis=2) / math.sqrt(n2)
    return tuple(jnp.asarray(t, F32) for t in (cs, f1, w))


def _prep_weights(g_pre_mix, w_in, g_q, w_uq, g_kv, w_ukv, w_fnet, w_out, g_post_mix,
                  g_pre_ffn, w_gate, w_up, conv_w, conv_b, w_down, g_post_ffn):
    win = w_in[0]
    f_lo = Q_LORA + KV_LORA + QK_ROPE
    kr_cols = jnp.concatenate([jnp.zeros((D_MODEL, QK_NOPE), F32),
                               win[:, Q_LORA + KV_LORA:f_lo],
                               jnp.zeros((D_MODEL, HEAD_PAD - QK_NOPE - QK_ROPE), F32)], axis=1)
    win_p = jnp.concatenate([win[:, :Q_LORA + KV_LORA], win[:, f_lo:], kr_cols], axis=1)
    wuq = w_uq[0].reshape(Q_LORA, MLA_HEADS, QK_NOPE + QK_ROPE)
    wuq = jnp.pad(wuq, ((0, 0), (0, 0), (0, HEAD_PAD - QK_NOPE - QK_ROPE)))
    wuq = wuq.reshape(Q_LORA, MLA_HEADS * HEAD_PAD)
    wukv = w_ukv[0].reshape(KV_LORA, MLA_HEADS, QK_NOPE + V_HEAD)
    wuk = jnp.pad(wukv[..., :QK_NOPE], ((0, 0), (0, 0), (0, HEAD_PAD - QK_NOPE)))
    wuk = wuk.reshape(KV_LORA, MLA_HEADS * HEAD_PAD)
    wuv = wukv[..., QK_NOPE:].reshape(KV_LORA, MLA_WIDTH)
    return dict(
        gpre=g_pre_mix, win=win_p.astype(BF16), gq=g_q, wuqt=wuq.T.astype(BF16), gkv=g_kv,
        wuk=wuk.astype(BF16), wuvt=wuv.T.astype(BF16), wf=w_fnet[0].astype(BF16),
        woa=w_out[0, :MLA_WIDTH].astype(BF16), wof=w_out[0, MLA_WIDTH:].astype(BF16),
        gpm=g_post_mix, gpf=g_pre_ffn, wg=w_gate[0].astype(BF16), wu=w_up[0].astype(BF16),
        cw=conv_w[0], cb=conv_b, wd=w_down[0].astype(BF16), gpo=g_post_ffn)


def _trunk(x, w):
    b, s, _ = x.shape
    n2 = DFT_N2
    n1 = s // n2
    rope_tabs = _rope_tables(s)
    cs, f1, wtab = _dft_tables(s)
    qt, k, vt, z = _proj(x, w["gpre"], w["win"], w["gq"], w["wuqt"], w["gkv"], w["wuk"], w["wuvt"],
                         cs, rope_tabs, tm=TOKEN_TILE)
    a = _attn(qt, k, vt, tq=MXU_TILE, tk=s // SCORE_SLOTS, qsubs=ATTN_QUERY_BLOCKS)
    y = _dft1(z.reshape(b, 2, FNET_GROUPS, n1, n2, FNET_GROUP_DIM), f1, r=DFT1_ROWS)
    f = _dft2(y, wtab, w["wf"], kb=DFT2_ROWS)
    f = f.reshape(b, FNET_GROUPS, s, FNET_GROUP_DIM)
    return _tail(a, f, x, w["woa"], w["wof"], w["gpm"], w["gpf"], w["wg"], w["wu"], w["cw"],
                 w["cb"], w["wd"], w["gpo"], tm=TOKEN_TILE, ffc=FF_CHUNK)


def kernel(x_prompt, x_sample, g_pre_mix, w_in, g_q, w_uq, g_kv, w_ukv, w_fnet, w_out,
           g_post_mix, g_pre_ffn, w_gate, w_up, conv_w, conv_b, w_down, g_post_ffn):
    w = _prep_weights(g_pre_mix, w_in, g_q, w_uq, g_kv, w_ukv, w_fnet, w_out, g_post_mix,
                      g_pre_ffn, w_gate, w_up, conv_w, conv_b, w_down, g_post_ffn)
    return _trunk(x_prompt, w), _trunk(x_sample, w)
```

```python
import functools
import math

import jax
import jax.numpy as jnp
import numpy as np
from jax.experimental import pallas as pl
from jax.experimental.pallas import tpu as pltpu

D_MODEL = 1024
MLA_HEADS = 8
QK_NOPE = 64
QK_ROPE = 32
V_HEAD = 64
MLA_WIDTH = MLA_HEADS * V_HEAD
Q_LORA = 256
KV_LORA = 256
FNET_GROUPS = 4
FNET_WIDTH = 512
FNET_GROUP_DIM = 128
D_FF = 2816
ROPE_THETA = 10000.0
RMS_EPS = 1e-6
ATTN_SCALE = 1.0 / math.sqrt(QK_NOPE + QK_ROPE)
Q_SCALE = ATTN_SCALE * math.log2(math.e)

LANES = 128
HEAD_PAD = 128
HALF_ROPE = QK_ROPE // 2
DENOM_ROWS = 16
SCORE_PAD = 8
MXU_TILE = 256
SCORE_ACC = (0, 64, 128)
VALUE_ACC = (192, 224)
SCORE_AHEAD = 2
SCORE_SLOTS = 4
ATTN_STEP_TILES = 128
TOKEN_TILE = 512
FF_CHUNK = 256
DFT1_TOKENS = 512
DFT2_ROWS = 8
DFT_N2 = 128
VMEM_LIMIT = 56 * 1024 * 1024

BF16 = jnp.bfloat16
F32 = jnp.float32


def _rms(x, g):
    return x * jax.lax.rsqrt(jnp.mean(x * x, axis=-1, keepdims=True) + RMS_EPS) * g


def _dot(a, b):
    return jnp.dot(a, b, preferred_element_type=F32)


def _params(sem):
    return pltpu.CompilerParams(dimension_semantics=sem, vmem_limit_bytes=VMEM_LIMIT)


def _const_spec(shape):
    zeros = (0,) * len(shape)
    return pl.BlockSpec(shape, lambda *_: zeros)


def _proj_kernel(x_ref, gpre_ref, win_ref, gq_ref, wuqt_ref, gkv_ref, wuk_ref, wuvt_ref, cs_ref,
                 cos_ref, sina_ref, sinb_ref, cost_ref, sint_ref, qt_ref, k_ref, vt_ref, z_ref):
    tm = x_ref.shape[1]
    h = _rms(x_ref[0], gpre_ref[...]).astype(BF16)
    p = _dot(h, win_ref[...])
    cq = _rms(p[:, :Q_LORA], gq_ref[...])
    qt = _dot(wuqt_ref[...], cq.T.astype(BF16))
    ckv = _rms(p[:, Q_LORA:Q_LORA + KV_LORA], gkv_ref[...])
    kn = _dot(ckv.astype(BF16), wuk_ref[...])
    vt_ref[0] = _dot(wuvt_ref[...], ckv.T.astype(BF16)).astype(BF16)
    cos, sina, sinb = cos_ref[...], sina_ref[...], sinb_ref[...]
    t = p[:, 2 * FNET_WIDTH:]
    kr = (t * cos + pltpu.roll(t, HALF_ROPE, 1) * sina
          + pltpu.roll(t, HEAD_PAD - HALF_ROPE, 1) * sinb)
    cos_t, sin_t = cost_ref[...], sint_ref[...]
    lo, mid, hi = QK_NOPE, QK_NOPE + HALF_ROPE, QK_NOPE + QK_ROPE
    for hd in range(MLA_HEADS):
        base = hd * HEAD_PAD
        x1, x2 = qt[base + lo:base + mid], qt[base + mid:base + hi]
        qt_ref[0, base:base + lo, :] = (qt[base:base + lo] * Q_SCALE).astype(BF16)
        qt_ref[0, base + lo:base + mid, :] = ((x1 * cos_t - x2 * sin_t) * Q_SCALE).astype(BF16)
        qt_ref[0, base + mid:base + hi, :] = ((x2 * cos_t + x1 * sin_t) * Q_SCALE).astype(BF16)
        qt_ref[0, base + hi:base + HEAD_PAD, :] = jnp.zeros((HEAD_PAD - hi, tm), BF16)
        sl = slice(base, base + HEAD_PAD)
        k_ref[0, :, sl] = (kn[:, sl] + kr).astype(BF16)
    f = p[:, Q_LORA + KV_LORA:2 * FNET_WIDTH].astype(BF16)
    cs = cs_ref[...].astype(BF16)
    for g in range(FNET_GROUPS):
        sl = slice(g * FNET_GROUP_DIM, (g + 1) * FNET_GROUP_DIM)
        ab = _dot(f[:, sl], cs)
        z_ref[0, 0, g] = ab[:, :FNET_GROUP_DIM]
        z_ref[0, 1, g] = ab[:, FNET_GROUP_DIM:]


def _proj(x, gpre, win, gq, wuqt, gkv, wuk, wuvt, cs, rope_tabs, tm):
    b, s, _ = x.shape
    wide = MLA_HEADS * HEAD_PAD
    tok = lambda w: pl.BlockSpec((1, tm, w), lambda bi, i: (bi, i, 0))
    tab = pl.BlockSpec((tm, HEAD_PAD), lambda bi, i: (i, 0))
    tab_t = pl.BlockSpec((HALF_ROPE, tm), lambda bi, i: (0, i))
    return pl.pallas_call(
        _proj_kernel,
        grid=(b, s // tm),
        in_specs=[tok(D_MODEL), _const_spec(gpre.shape), _const_spec(win.shape),
                  _const_spec(gq.shape), _const_spec(wuqt.shape), _const_spec(gkv.shape),
                  _const_spec(wuk.shape), _const_spec(wuvt.shape), _const_spec(cs.shape),
                  tab, tab, tab, tab_t, tab_t],
        out_specs=[pl.BlockSpec((1, wide, tm), lambda bi, i: (bi, 0, i)),
                   tok(wide),
                   pl.BlockSpec((1, MLA_WIDTH, tm), lambda bi, i: (bi, 0, i)),
                   pl.BlockSpec((1, 2, FNET_GROUPS, tm, FNET_GROUP_DIM),
                                lambda bi, i: (bi, 0, 0, i, 0))],
        out_shape=[jax.ShapeDtypeStruct((b, wide, s), BF16),
                   jax.ShapeDtypeStruct((b, s, wide), BF16),
                   jax.ShapeDtypeStruct((b, MLA_WIDTH, s), BF16),
                   jax.ShapeDtypeStruct((b, 2, FNET_GROUPS, s, FNET_GROUP_DIM), F32)],
        compiler_params=_params(("parallel", "parallel")),
        name="proj",
    )(x, gpre, win, gq, wuqt, gkv, wuk, wuvt, cs, *rope_tabs)


def _attn_kernel(qt_ref, k_ref, vt_ref, o_ref, st_all, mc_all, *, tq, tk, heads, qsubs):
    st_ref = tuple(st_all.at[k] for k in range(SCORE_SLOTS))
    mc_ref = tuple(mc_all.at[k] for k in range(SCORE_SLOTS))
    s = k_ref.shape[1]
    n = s // tk
    nt = tk // MXU_TILE
    i = pl.program_id(2)
    ones = jnp.ones((DENOM_ROWS, MXU_TILE), BF16)
    pad_k = jnp.zeros((MXU_TILE, MXU_TILE - HEAD_PAD), BF16)
    pad_q = jnp.zeros((MXU_TILE - HEAD_PAD, tq), BF16)

    def q_rhs(hd, qi):
        q_off = qi * tq if isinstance(qi, int) else pl.multiple_of(qi * tq, tq)
        qt = qt_ref[0, hd * HEAD_PAD:(hd + 1) * HEAD_PAD, pl.ds(q_off, tq)]
        return jnp.concatenate([qt, pad_q], axis=0)

    def k_lhs(hd, c, j):
        k_lo = c * tk + j * MXU_TILE
        kc = k_ref[0, k_lo:k_lo + MXU_TILE, hd * HEAD_PAD:(hd + 1) * HEAD_PAD]
        return jnp.concatenate([kc, pad_k], axis=1)

    def pop_scores(hd, j, slot, mx):
        st = pltpu.matmul_pop(SCORE_ACC[j % len(SCORE_ACC)], (MXU_TILE, tq), F32, hd)
        st_ref[slot][hd, j * MXU_TILE:(j + 1) * MXU_TILE, :] = st
        return jnp.maximum(mx, jnp.max(st, axis=0, keepdims=True))

    lag = len(SCORE_ACC) - 1

    @pl.when(i == 0)
    def _():
        for c in range(SCORE_AHEAD):
            mx = [jnp.full((1, tq), -jnp.inf, F32) for _ in range(heads)]
            for hd in range(heads):
                pltpu.matmul_push_rhs(q_rhs(hd, 0), 0, hd)
            for j in range(nt + lag):
                for hd in range(heads):
                    if j < nt:
                        pltpu.matmul_acc_lhs(SCORE_ACC[j % len(SCORE_ACC)], k_lhs(hd, c, j), hd,
                                             0 if j == 0 else None)
                    if j >= lag:
                        mx[hd] = pop_scores(hd, j - lag, c, mx[hd])
            for hd in range(heads):
                mc_ref[c][hd] = mx[hd]

    pending = []
    tile = 0
    block_accs = []

    def drain(keep):
        while len(pending) > keep:
            pending.pop(0)()

    for qsub in range(qsubs):
        qb = i * qsubs + qsub
        ms = [jnp.full((1, tq), -jnp.inf, F32) for _ in range(heads)]
        accs = [jnp.zeros((V_HEAD + DENOM_ROWS, tq), F32) for _ in range(heads)]
        block_accs.append(accs)
        for slot in range(n):
            ahead = slot + SCORE_AHEAD
            if ahead < n:
                qi, c_next = qb, ahead
            elif qsub + 1 < qsubs:
                qi, c_next = qb + 1, ahead - n
            else:
                qi, c_next = jnp.minimum(qb + 1, s // tq - 1), ahead - n
            fill = ahead % SCORE_SLOTS
            value_acc = VALUE_ACC[slot % len(VALUE_ACC)]
            m_old = list(ms)
            ms = [jnp.maximum(ms[hd], mc_ref[slot][hd]) for hd in range(heads)]
            mx = [jnp.full((1, tq), -jnp.inf, F32) for _ in range(heads)]
            for j in range(nt):
                score_acc = SCORE_ACC[tile % len(SCORE_ACC)]
                tile += 1
                for hd in range(heads):
                    rows = slice(j * MXU_TILE, (j + 1) * MXU_TILE)
                    p = jnp.exp2(st_ref[slot][hd, rows, :] - ms[hd]).astype(BF16)
                    pltpu.matmul_push_rhs(p, 1, hd)
                    pltpu.matmul_push_rhs(q_rhs(hd, qi), 0, hd)
                    v_lo = slot * tk + j * MXU_TILE
                    vc = vt_ref[0, hd * V_HEAD:(hd + 1) * V_HEAD, v_lo:v_lo + MXU_TILE]
                    pltpu.matmul_acc_lhs(value_acc, jnp.concatenate([vc, ones], axis=0), hd, 1)
                    pltpu.matmul_acc_lhs(score_acc, k_lhs(hd, c_next, j), hd, 0)

                def pop_tile(j=j, fill=fill, mx=mx, score_acc=score_acc, value_acc=value_acc,
                             m_old=m_old, m_new=list(ms), accs=accs):
                    for hd in range(heads):
                        st = pltpu.matmul_pop(score_acc, (MXU_TILE, tq), F32, hd)
                        st_ref[fill][hd, j * MXU_TILE:(j + 1) * MXU_TILE, :] = st
                        mx[hd] = jnp.maximum(mx[hd], jnp.max(st, axis=0, keepdims=True))
                        if j == nt - 1:
                            mc_ref[fill][hd] = mx[hd]
                            pv = pltpu.matmul_pop(value_acc, (V_HEAD + DENOM_ROWS, tq), F32, hd)
                            accs[hd] = jnp.exp2(m_old[hd] - m_new[hd]) * accs[hd] + pv

                drain(lag - 1)
                pending.append(pop_tile)
    drain(0)
    for qsub, accs in enumerate(block_accs):
        outs = [acc[:V_HEAD] / acc[V_HEAD:V_HEAD + 1] for acc in accs]
        o_ref[0, qsub * tq:(qsub + 1) * tq, :] = jnp.concatenate(outs, axis=0).T.astype(BF16)


def _attn(qt, k, vt, tq, tk, qsubs):
    b, _, s = qt.shape
    heads = LANES // V_HEAD
    groups = MLA_HEADS // heads
    assert s % (qsubs * tq) == 0 and s == SCORE_SLOTS * tk and tq == MXU_TILE and tk % MXU_TILE == 0
    return pl.pallas_call(
        functools.partial(_attn_kernel, tq=tq, tk=tk, heads=heads, qsubs=qsubs),
        grid=(b, groups, s // (qsubs * tq)),
        in_specs=[pl.BlockSpec((1, heads * HEAD_PAD, s), lambda bi, j, i: (bi, j, 0)),
                  pl.BlockSpec((1, s, heads * HEAD_PAD), lambda bi, j, i: (bi, 0, j)),
                  pl.BlockSpec((1, heads * V_HEAD, s), lambda bi, j, i: (bi, j, 0))],
        out_specs=pl.BlockSpec((1, qsubs * tq, heads * V_HEAD), lambda bi, j, i: (bi, i, j)),
        out_shape=jax.ShapeDtypeStruct((b, s, MLA_WIDTH), BF16),
        scratch_shapes=[pltpu.VMEM((SCORE_SLOTS, heads, tk + SCORE_PAD, tq), F32),
                        pltpu.VMEM((SCORE_SLOTS, heads, 1, tq), F32)],
        compiler_params=_params(("parallel", "parallel", "arbitrary")),
        name="attn",
    )(qt, k, vt)


def _dft1_kernel(z_ref, f1_ref, y_ref):
    groups, n1, r, c = z_ref.shape[2:]
    f1 = f1_ref[...].astype(BF16)
    flat = lambda ref, part, g: ref.at[0, part, g].reshape(n1 * r, c)
    for j in range(r):
        rows = pl.ds(j, n1, stride=r)
        gather = lambda part: jnp.concatenate(
            [flat(z_ref, part, g)[rows, :] for g in range(groups)], axis=1).astype(BF16)
        y = _dot(f1[:, :n1], gather(0)) + _dot(f1[:, n1:], gather(1))
        for g in range(groups):
            flat(y_ref, 0, g)[rows, :] = y[:n1, g * c:(g + 1) * c]
            flat(y_ref, 1, g)[rows, :] = y[n1:, g * c:(g + 1) * c]


def _dft1(z, f1, r):
    b, _, groups, n1, n2, c = z.shape
    spec = pl.BlockSpec((1, 2, groups, n1, r, c), lambda bi, j: (bi, 0, 0, 0, j, 0))
    return pl.pallas_call(
        _dft1_kernel,
        grid=(b, n2 // r),
        in_specs=[spec, _const_spec(f1.shape)],
        out_specs=spec,
        out_shape=jax.ShapeDtypeStruct(z.shape, F32),
        compiler_params=_params(("parallel", "parallel")),
        name="dft1",
    )(z, f1)


def _dft2_kernel(y_ref, w_ref, wf_ref, o_ref):
    groups, kb, n2, c = y_ref.shape[2:]
    for t in range(kb):
        w = w_ref[t].astype(BF16)
        part = lambda p: jnp.concatenate(
            [y_ref[0, p, g, t] for g in range(groups)], axis=1).astype(BF16)
        xr = (_dot(w[:, :n2], part(0)) + _dot(w[:, n2:], part(1))).astype(BF16)
        for g in range(groups):
            rows = o_ref.at[0, g].reshape(n2 * kb, c)
            rows[pl.ds(t, n2, stride=kb), :] = _dot(xr[:, g * c:(g + 1) * c], wf_ref[g])


def _dft2(y, w, wf, kb):
    b, _, groups, n1, n2, c = y.shape
    return pl.pallas_call(
        _dft2_kernel,
        grid=(b, n1 // kb),
        in_specs=[pl.BlockSpec((1, 2, groups, kb, n2, c), lambda bi, j: (bi, 0, 0, j, 0, 0)),
                  pl.BlockSpec((kb, n2, 2 * n2), lambda bi, j: (j, 0, 0)),
                  _const_spec(wf.shape)],
        out_specs=pl.BlockSpec((1, groups, n2, kb, c), lambda bi, j: (bi, 0, 0, j, 0)),
        out_shape=jax.ShapeDtypeStruct((b, groups, n2, n1, c), F32),
        compiler_params=_params(("parallel", "parallel")),
        name="dft2",
    )(y, w, wf)


HALO = 16


def _tail_kernel(a_ref, ap_ref, an_ref, f_ref, fp_ref, fn_ref, x_ref, xp_ref, xn_ref,
                 woa_ref, wof_ref, gpm_ref, gpf_ref, wg_ref, wu_ref, cw_ref, cb_ref, wd_ref,
                 gpo_ref, y_ref, act_ref, *, ffc):
    i = pl.program_id(1)
    tm = x_ref.shape[1]
    rows = tm + 2 * HALO
    ext = lambda p, c, n: jnp.concatenate([p[0], c[0], n[0]], axis=0)
    f_ext = jnp.concatenate(
        [jnp.concatenate([fp_ref[0, g], f_ref[0, g], fn_ref[0, g]], axis=0)
         for g in range(FNET_GROUPS)], axis=1).astype(BF16)
    mix = _dot(ext(ap_ref, a_ref, an_ref), woa_ref[...]) + _dot(f_ext, wof_ref[...])
    x1 = ext(xp_ref, x_ref, xn_ref) + _rms(mix, gpm_ref[...])
    row = jax.lax.broadcasted_iota(jnp.int32, (rows, 1), 0)
    inside = ((row >= HALO) | (i > 0)) & ((row < HALO + tm) | (i < pl.num_programs(1) - 1))
    hext = jnp.where(inside, _rms(x1, gpf_ref[...]), 0.0).astype(BF16)
    h = hext[HALO:HALO + tm]
    for c in range(D_FF // ffc):
        sl = slice(c * ffc, (c + 1) * ffc)
        g = _dot(hext, wg_ref[:, sl])
        u = _dot(h, wu_ref[:, sl])
        g_prev = pltpu.roll(g, 1, 0)[HALO:HALO + tm]
        g_next = pltpu.roll(g, rows - 1, 0)[HALO:HALO + tm]
        gate = (cb_ref[:, sl] + g_prev * cw_ref[0:1, sl] + g[HALO:HALO + tm] * cw_ref[1:2, sl]
                + g_next * cw_ref[2:3, sl])
        inner = math.sqrt(2.0 / math.pi) * (gate + 0.044715 * (gate * gate * gate))
        act = 0.5 * gate * (1.0 + jnp.tanh(inner)) * u
        act_ref[:, sl] = act.astype(BF16)
    out = _dot(act_ref[...], wd_ref[...])
    y_ref[0] = x1[HALO:HALO + tm] + _rms(out, gpo_ref[...])


def _tail(a, f, x, woa, wof, gpm, gpf, wg, wu, cw, cb, wd, gpo, tm, ffc):
    b, s, _ = x.shape
    per = tm // HALO
    last = s // HALO - 1
    single = lambda shape: pl.BlockSpec(shape, lambda *_: (0,) * len(shape),
                                        pipeline_mode=pl.Buffered(1))

    def with_halo(width):
        return [pl.BlockSpec((1, tm, width), lambda bi, i: (bi, i, 0)),
                pl.BlockSpec((1, HALO, width), lambda bi, i: (bi, jnp.maximum(i * per - 1, 0), 0)),
                pl.BlockSpec((1, HALO, width), lambda bi, i: (bi, jnp.minimum((i + 1) * per, last), 0))]

    grouped = lambda rows, row_block: pl.BlockSpec(
        (1, FNET_GROUPS, rows, FNET_GROUP_DIM), lambda bi, i: (bi, 0, row_block(i), 0))
    f_specs = [grouped(tm, lambda i: i),
               grouped(HALO, lambda i: jnp.maximum(i * per - 1, 0)),
               grouped(HALO, lambda i: jnp.minimum((i + 1) * per, last))]
    return pl.pallas_call(
        functools.partial(_tail_kernel, ffc=ffc),
        grid=(b, s // tm),
        in_specs=with_halo(MLA_WIDTH) + f_specs + with_halo(D_MODEL) + [
            single(woa.shape), single(wof.shape), _const_spec(gpm.shape), _const_spec(gpf.shape),
            single(wg.shape), single(wu.shape), _const_spec(cw.shape), _const_spec(cb.shape),
            single(wd.shape), _const_spec(gpo.shape)],
        out_specs=pl.BlockSpec((1, tm, D_MODEL), lambda bi, i: (bi, i, 0)),
        out_shape=jax.ShapeDtypeStruct(x.shape, F32),
        scratch_shapes=[pltpu.VMEM((tm, D_FF), BF16)],
        compiler_params=_params(("parallel", "arbitrary")),
        name="tail",
    )(a, a, a, f, f, f, x, x, x, woa, wof, gpm, gpf, wg, wu, cw, cb, wd, gpo)


def _rope_tables(s):
    ang = np.arange(s, dtype=np.float64)[:, None] * (
        ROPE_THETA ** (-np.arange(0, QK_ROPE, 2, dtype=np.float64) / QK_ROPE))[None, :]
    cos, sin = np.cos(ang), np.sin(ang)
    zeros = np.zeros((s, QK_NOPE))
    zh = np.zeros((s, HALF_ROPE))
    tail = np.zeros((s, HEAD_PAD - QK_NOPE - QK_ROPE))
    cos_t = np.concatenate([np.ones((s, QK_NOPE)), cos, cos, tail], axis=1)
    sina = np.concatenate([zeros, zh, sin, tail], axis=1)
    sinb = np.concatenate([zeros, -sin, zh, tail], axis=1)
    return tuple(jnp.asarray(t, F32) for t in (cos_t, sina, sinb, cos.T, sin.T))


def _angles(num, den):
    return (2.0 * np.pi / den) * (num % den).astype(np.float64)


def _dft_tables(s):
    n2 = DFT_N2
    n1 = s // n2
    c = np.arange(FNET_GROUP_DIM, dtype=np.int64)
    ang = _angles(c[:, None] * c[None, :], FNET_GROUP_DIM)
    cs = np.concatenate([np.cos(ang), np.sin(ang)], axis=1) / math.sqrt(FNET_GROUP_DIM)
    k1 = np.arange(n1, dtype=np.int64)
    a1 = _angles(k1[:, None] * k1[None, :], n1)
    c1, s1 = np.cos(a1), np.sin(a1)
    f1 = np.concatenate([np.concatenate([c1, -s1], axis=1),
                         np.concatenate([s1, c1], axis=1)], axis=0) / math.sqrt(n1)
    k2 = np.arange(n2, dtype=np.int64)
    kk = k1[:, None, None] + n1 * k2[None, :, None]
    a2 = _angles(kk * k2[None, None, :], s)
    w = np.concatenate([np.cos(a2), -np.sin(a2)], axis=2) / math.sqrt(n2)
    return tuple(jnp.asarray(t, F32) for t in (cs, f1, w))


def _prep_weights(g_pre_mix, w_in, g_q, w_uq, g_kv, w_ukv, w_fnet, w_out, g_post_mix,
                  g_pre_ffn, w_gate, w_up, conv_w, conv_b, w_down, g_post_ffn):
    win = w_in[0]
    f_lo = Q_LORA + KV_LORA + QK_ROPE
    kr_cols = jnp.concatenate([jnp.zeros((D_MODEL, QK_NOPE), F32),
                               win[:, Q_LORA + KV_LORA:f_lo],
                               jnp.zeros((D_MODEL, HEAD_PAD - QK_NOPE - QK_ROPE), F32)], axis=1)
    win_p = jnp.concatenate([win[:, :Q_LORA + KV_LORA], win[:, f_lo:], kr_cols], axis=1)
    wuq = w_uq[0].reshape(Q_LORA, MLA_HEADS, QK_NOPE + QK_ROPE)
    wuq = jnp.pad(wuq, ((0, 0), (0, 0), (0, HEAD_PAD - QK_NOPE - QK_ROPE)))
    wuq = wuq.reshape(Q_LORA, MLA_HEADS * HEAD_PAD)
    wukv = w_ukv[0].reshape(KV_LORA, MLA_HEADS, QK_NOPE + V_HEAD)
    wuk = jnp.pad(wukv[..., :QK_NOPE], ((0, 0), (0, 0), (0, HEAD_PAD - QK_NOPE)))
    wuk = wuk.reshape(KV_LORA, MLA_HEADS * HEAD_PAD)
    wuv = wukv[..., QK_NOPE:].reshape(KV_LORA, MLA_WIDTH)
    return dict(
        gpre=g_pre_mix, win=win_p.astype(BF16), gq=g_q, wuqt=wuq.T.astype(BF16), gkv=g_kv,
        wuk=wuk.astype(BF16), wuvt=wuv.T.astype(BF16), wf=w_fnet[0].astype(BF16),
        woa=w_out[0, :MLA_WIDTH].astype(BF16), wof=w_out[0, MLA_WIDTH:].astype(BF16),
        gpm=g_post_mix, gpf=g_pre_ffn, wg=w_gate[0].astype(BF16), wu=w_up[0].astype(BF16),
        cw=conv_w[0], cb=conv_b, wd=w_down[0].astype(BF16), gpo=g_post_ffn)


def _trunk(x, w):
    b, s, _ = x.shape
    n2 = DFT_N2
    n1 = s // n2
    rope_tabs = _rope_tables(s)
    cs, f1, wtab = _dft_tables(s)
    qt, k, vt, z = _proj(x, w["gpre"], w["win"], w["gq"], w["wuqt"], w["gkv"], w["wuk"], w["wuvt"],
                         cs, rope_tabs, tm=TOKEN_TILE)
    a = _attn(qt, k, vt, tq=MXU_TILE, tk=s // SCORE_SLOTS, qsubs=ATTN_STEP_TILES // (s // MXU_TILE))
    y = _dft1(z.reshape(b, 2, FNET_GROUPS, n1, n2, FNET_GROUP_DIM), f1, r=DFT1_TOKENS // n1)
    f = _dft2(y, wtab, w["wf"], kb=DFT2_ROWS)
    f = f.reshape(b, FNET_GROUPS, s, FNET_GROUP_DIM)
    return _tail(a, f, x, w["woa"], w["wof"], w["gpm"], w["gpf"], w["wg"], w["wu"], w["cw"],
                 w["cb"], w["wd"], w["gpo"], tm=TOKEN_TILE, ffc=FF_CHUNK)


def kernel(x_prompt, x_sample, g_pre_mix, w_in, g_q, w_uq, g_kv, w_ukv, w_fnet, w_out,
           g_post_mix, g_pre_ffn, w_gate, w_up, conv_w, conv_b, w_down, g_post_ffn):
    w = _prep_weights(g_pre_mix, w_in, g_q, w_uq, g_kv, w_ukv, w_fnet, w_out, g_post_mix,
                      g_pre_ffn, w_gate, w_up, conv_w, conv_b, w_down, g_post_ffn)
    return _trunk(x_prompt, w), _trunk(x_sample, w)
```

```python
import functools
import math

import jax
import jax.numpy as jnp
import numpy as np
from jax.experimental import pallas as pl
from jax.experimental.pallas import tpu as pltpu

D_MODEL = 1024
MLA_HEADS = 8
QK_NOPE = 64
QK_ROPE = 32
V_HEAD = 64
MLA_WIDTH = MLA_HEADS * V_HEAD
Q_LORA = 256
KV_LORA = 256
FNET_GROUPS = 4
FNET_WIDTH = 512
FNET_GROUP_DIM = 128
D_FF = 2816
ROPE_THETA = 10000.0
RMS_EPS = 1e-6
ATTN_SCALE = 1.0 / math.sqrt(QK_NOPE + QK_ROPE)
Q_SCALE = ATTN_SCALE * math.log2(math.e)

LANES = 128
HEAD_PAD = 128
HALF_ROPE = QK_ROPE // 2
DENOM_ROWS = 16
SCORE_PAD = 8
MXU_TILE = 256
SCORE_ACC = (0, 64, 128)
VALUE_ACC = (192, 224)
SCORE_AHEAD = 2
SCORE_SLOTS = 4
ATTN_STEP_TILES = 128
TOKEN_TILE = 512
FF_CHUNK = 256
DFT1_TOKENS = 512
DFT2_ROWS = 8
DFT_N2 = 128
VMEM_LIMIT = 56 * 1024 * 1024

BF16 = jnp.bfloat16
F32 = jnp.float32


def _rms(x, g):
    return x * jax.lax.rsqrt(jnp.mean(x * x, axis=-1, keepdims=True) + RMS_EPS) * g


def _dot(a, b):
    return jnp.dot(a, b, preferred_element_type=F32)


def _params(sem):
    return pltpu.CompilerParams(dimension_semantics=sem, vmem_limit_bytes=VMEM_LIMIT)


def _const_spec(shape):
    zeros = (0,) * len(shape)
    return pl.BlockSpec(shape, lambda *_: zeros)


def _proj_kernel(x_ref, gpre_ref, win_ref, gq_ref, wuqt_ref, gkv_ref, wuk_ref, wuvt_ref,
                 cos_ref, sina_ref, sinb_ref, cost_ref, sint_ref, qt_ref, k_ref, vt_ref, f_ref):
    tm = x_ref.shape[1]
    h = _rms(x_ref[0], gpre_ref[...]).astype(BF16)
    p = _dot(h, win_ref[...])
    cq = _rms(p[:, :Q_LORA], gq_ref[...])
    qt = _dot(wuqt_ref[...], cq.T.astype(BF16))
    ckv = _rms(p[:, Q_LORA:Q_LORA + KV_LORA], gkv_ref[...])
    kn = _dot(ckv.astype(BF16), wuk_ref[...])
    vt_ref[0] = _dot(wuvt_ref[...], ckv.T.astype(BF16)).astype(BF16)
    cos, sina, sinb = cos_ref[...], sina_ref[...], sinb_ref[...]
    t = p[:, 2 * FNET_WIDTH:]
    kr = (t * cos + pltpu.roll(t, HALF_ROPE, 1) * sina
          + pltpu.roll(t, HEAD_PAD - HALF_ROPE, 1) * sinb)
    cos_t, sin_t = cost_ref[...], sint_ref[...]
    lo, mid, hi = QK_NOPE, QK_NOPE + HALF_ROPE, QK_NOPE + QK_ROPE
    for hd in range(MLA_HEADS):
        base = hd * HEAD_PAD
        x1, x2 = qt[base + lo:base + mid], qt[base + mid:base + hi]
        qt_ref[0, base:base + lo, :] = (qt[base:base + lo] * Q_SCALE).astype(BF16)
        qt_ref[0, base + lo:base + mid, :] = ((x1 * cos_t - x2 * sin_t) * Q_SCALE).astype(BF16)
        qt_ref[0, base + mid:base + hi, :] = ((x2 * cos_t + x1 * sin_t) * Q_SCALE).astype(BF16)
        qt_ref[0, base + hi:base + HEAD_PAD, :] = jnp.zeros((HEAD_PAD - hi, tm), BF16)
        sl = slice(base, base + HEAD_PAD)
        k_ref[0, :, sl] = (kn[:, sl] + kr).astype(BF16)
    for g in range(FNET_GROUPS):
        lo = Q_LORA + KV_LORA + g * FNET_GROUP_DIM
        f_ref[0, g] = p[:, lo:lo + FNET_GROUP_DIM]


def _proj(x, gpre, win, gq, wuqt, gkv, wuk, wuvt, rope_tabs, tm):
    b, s, _ = x.shape
    wide = MLA_HEADS * HEAD_PAD
    tok = lambda w: pl.BlockSpec((1, tm, w), lambda bi, i: (bi, i, 0))
    tab = pl.BlockSpec((tm, HEAD_PAD), lambda bi, i: (i, 0))
    tab_t = pl.BlockSpec((HALF_ROPE, tm), lambda bi, i: (0, i))
    return pl.pallas_call(
        _proj_kernel,
        grid=(b, s // tm),
        in_specs=[tok(D_MODEL), _const_spec(gpre.shape), _const_spec(win.shape),
                  _const_spec(gq.shape), _const_spec(wuqt.shape), _const_spec(gkv.shape),
                  _const_spec(wuk.shape), _const_spec(wuvt.shape),
                  tab, tab, tab, tab_t, tab_t],
        out_specs=[pl.BlockSpec((1, wide, tm), lambda bi, i: (bi, 0, i)),
                   tok(wide),
                   pl.BlockSpec((1, MLA_WIDTH, tm), lambda bi, i: (bi, 0, i)),
                   pl.BlockSpec((1, FNET_GROUPS, tm, FNET_GROUP_DIM),
                                lambda bi, i: (bi, 0, i, 0))],
        out_shape=[jax.ShapeDtypeStruct((b, wide, s), BF16),
                   jax.ShapeDtypeStruct((b, s, wide), BF16),
                   jax.ShapeDtypeStruct((b, MLA_WIDTH, s), BF16),
                   jax.ShapeDtypeStruct((b, FNET_GROUPS, s, FNET_GROUP_DIM), F32)],
        compiler_params=_params(("parallel", "parallel")),
        name="proj",
    )(x, gpre, win, gq, wuqt, gkv, wuk, wuvt, *rope_tabs)


def _attn_kernel(qt_ref, k_ref, vt_ref, o_ref, st_all, mc_all, *, tq, tk, heads, qsubs):
    st_ref = tuple(st_all.at[k] for k in range(SCORE_SLOTS))
    mc_ref = tuple(mc_all.at[k] for k in range(SCORE_SLOTS))
    s = k_ref.shape[1]
    n = s // tk
    nt = tk // MXU_TILE
    i = pl.program_id(2)
    ones = jnp.ones((DENOM_ROWS, MXU_TILE), BF16)
    pad_k = jnp.zeros((MXU_TILE, MXU_TILE - HEAD_PAD), BF16)
    pad_q = jnp.zeros((MXU_TILE - HEAD_PAD, tq), BF16)

    def q_rhs(hd, qi):
        q_off = qi * tq if isinstance(qi, int) else pl.multiple_of(qi * tq, tq)
        qt = qt_ref[0, hd * HEAD_PAD:(hd + 1) * HEAD_PAD, pl.ds(q_off, tq)]
        return jnp.concatenate([qt, pad_q], axis=0)

    def k_lhs(hd, c, j):
        k_lo = c * tk + j * MXU_TILE
        kc = k_ref[0, k_lo:k_lo + MXU_TILE, hd * HEAD_PAD:(hd + 1) * HEAD_PAD]
        return jnp.concatenate([kc, pad_k], axis=1)

    def pop_scores(hd, j, slot, mx):
        st = pltpu.matmul_pop(SCORE_ACC[j % len(SCORE_ACC)], (MXU_TILE, tq), F32, hd)
        st_ref[slot][hd, j * MXU_TILE:(j + 1) * MXU_TILE, :] = st
        return jnp.maximum(mx, jnp.max(st, axis=0, keepdims=True))

    lag = len(SCORE_ACC) - 1

    @pl.when(i == 0)
    def _():
        for c in range(SCORE_AHEAD):
            mx = [jnp.full((1, tq), -jnp.inf, F32) for _ in range(heads)]
            for hd in range(heads):
                pltpu.matmul_push_rhs(q_rhs(hd, 0), 0, hd)
            for j in range(nt + lag):
                for hd in range(heads):
                    if j < nt:
                        pltpu.matmul_acc_lhs(SCORE_ACC[j % len(SCORE_ACC)], k_lhs(hd, c, j), hd,
                                             0 if j == 0 else None)
                    if j >= lag:
                        mx[hd] = pop_scores(hd, j - lag, c, mx[hd])
            for hd in range(heads):
                mc_ref[c][hd] = mx[hd]

    pending = []
    tile = 0
    block_accs = []

    def drain(keep):
        while len(pending) > keep:
            pending.pop(0)()

    for qsub in range(qsubs):
        qb = i * qsubs + qsub
        ms = [jnp.full((1, tq), -jnp.inf, F32) for _ in range(heads)]
        accs = [jnp.zeros((V_HEAD + DENOM_ROWS, tq), F32) for _ in range(heads)]
        block_accs.append(accs)
        for slot in range(n):
            ahead = slot + SCORE_AHEAD
            if ahead < n:
                qi, c_next = qb, ahead
            elif qsub + 1 < qsubs:
                qi, c_next = qb + 1, ahead - n
            else:
                qi, c_next = jnp.minimum(qb + 1, s // tq - 1), ahead - n
            fill = ahead % SCORE_SLOTS
            value_acc = VALUE_ACC[slot % len(VALUE_ACC)]
            m_old = list(ms)
            ms = [jnp.maximum(ms[hd], mc_ref[slot][hd]) for hd in range(heads)]
            mx = [jnp.full((1, tq), -jnp.inf, F32) for _ in range(heads)]
            for j in range(nt):
                score_acc = SCORE_ACC[tile % len(SCORE_ACC)]
                tile += 1
                for hd in range(heads):
                    rows = slice(j * MXU_TILE, (j + 1) * MXU_TILE)
                    p = jnp.exp2(st_ref[slot][hd, rows, :] - ms[hd]).astype(BF16)
                    pltpu.matmul_push_rhs(p, 1, hd)
                    pltpu.matmul_push_rhs(q_rhs(hd, qi), 0, hd)
                    v_lo = slot * tk + j * MXU_TILE
                    vc = vt_ref[0, hd * V_HEAD:(hd + 1) * V_HEAD, v_lo:v_lo + MXU_TILE]
                    pltpu.matmul_acc_lhs(value_acc, jnp.concatenate([vc, ones], axis=0), hd, 1)
                    pltpu.matmul_acc_lhs(score_acc, k_lhs(hd, c_next, j), hd, 0)

                def pop_tile(j=j, fill=fill, mx=mx, score_acc=score_acc, value_acc=value_acc,
                             m_old=m_old, m_new=list(ms), accs=accs):
                    for hd in range(heads):
                        st = pltpu.matmul_pop(score_acc, (MXU_TILE, tq), F32, hd)
                        st_ref[fill][hd, j * MXU_TILE:(j + 1) * MXU_TILE, :] = st
                        mx[hd] = jnp.maximum(mx[hd], jnp.max(st, axis=0, keepdims=True))
                        if j == nt - 1:
                            mc_ref[fill][hd] = mx[hd]
                            pv = pltpu.matmul_pop(value_acc, (V_HEAD + DENOM_ROWS, tq), F32, hd)
                            accs[hd] = jnp.exp2(m_old[hd] - m_new[hd]) * accs[hd] + pv

                drain(lag - 1)
                pending.append(pop_tile)
    drain(0)
    for qsub, accs in enumerate(block_accs):
        outs = [acc[:V_HEAD] / acc[V_HEAD:V_HEAD + 1] for acc in accs]
        o_ref[0, qsub * tq:(qsub + 1) * tq, :] = jnp.concatenate(outs, axis=0).T.astype(BF16)


def _attn(qt, k, vt, tq, tk, qsubs):
    b, _, s = qt.shape
    heads = LANES // V_HEAD
    groups = MLA_HEADS // heads
    assert s % (qsubs * tq) == 0 and s == SCORE_SLOTS * tk and tq == MXU_TILE and tk % MXU_TILE == 0
    return pl.pallas_call(
        functools.partial(_attn_kernel, tq=tq, tk=tk, heads=heads, qsubs=qsubs),
        grid=(b, groups, s // (qsubs * tq)),
        in_specs=[pl.BlockSpec((1, heads * HEAD_PAD, s), lambda bi, j, i: (bi, j, 0)),
                  pl.BlockSpec((1, s, heads * HEAD_PAD), lambda bi, j, i: (bi, 0, j)),
                  pl.BlockSpec((1, heads * V_HEAD, s), lambda bi, j, i: (bi, j, 0))],
        out_specs=pl.BlockSpec((1, qsubs * tq, heads * V_HEAD), lambda bi, j, i: (bi, i, j)),
        out_shape=jax.ShapeDtypeStruct((b, s, MLA_WIDTH), BF16),
        scratch_shapes=[pltpu.VMEM((SCORE_SLOTS, heads, tk + SCORE_PAD, tq), F32),
                        pltpu.VMEM((SCORE_SLOTS, heads, 1, tq), F32)],
        compiler_params=_params(("parallel", "parallel", "arbitrary")),
        name="attn",
    )(qt, k, vt)


def _dft1_kernel(x_ref, cs_ref, f1_ref, y_ref):
    groups, n1, r, c = x_ref.shape[1:]
    f1 = f1_ref[...].astype(BF16)
    rows = [pl.ds(j, n1, stride=r) for j in range(r)]
    flat_x = lambda g: x_ref.at[0, g].reshape(n1 * r, c)
    flat_y = lambda part, g: y_ref.at[0, part, g].reshape(n1 * r, c)
    x = jnp.concatenate([flat_x(g)[rows[j], :] for j in range(r) for g in range(groups)], axis=0)
    z = _dot(x.astype(BF16), cs_ref[...].astype(BF16)).astype(BF16)
    for j in range(r):
        blk = [z[(j * groups + g) * n1:(j * groups + g + 1) * n1] for g in range(groups)]
        zr = jnp.concatenate([t[:, :c] for t in blk], axis=1)
        zi = jnp.concatenate([t[:, c:] for t in blk], axis=1)
        y = _dot(f1[:, :n1], zr) + _dot(f1[:, n1:], zi)
        for g in range(groups):
            flat_y(0, g)[rows[j], :] = y[:n1, g * c:(g + 1) * c]
            flat_y(1, g)[rows[j], :] = y[n1:, g * c:(g + 1) * c]


def _dft1(x, cs, f1, r):
    b, groups, n1, n2, c = x.shape
    return pl.pallas_call(
        _dft1_kernel,
        grid=(b, n2 // r),
        in_specs=[pl.BlockSpec((1, groups, n1, r, c), lambda bi, j: (bi, 0, 0, j, 0)),
                  _const_spec(cs.shape), _const_spec(f1.shape)],
        out_specs=pl.BlockSpec((1, 2, groups, n1, r, c), lambda bi, j: (bi, 0, 0, 0, j, 0)),
        out_shape=jax.ShapeDtypeStruct((b, 2, groups, n1, n2, c), F32),
        compiler_params=_params(("parallel", "parallel")),
        name="dft1",
    )(x, cs, f1)


def _dft2_kernel(y_ref, w_ref, wf_ref, o_ref):
    groups, kb, n2, c = y_ref.shape[2:]
    for t in range(kb):
        w = w_ref[t].astype(BF16)
        part = lambda p: jnp.concatenate(
            [y_ref[0, p, g, t] for g in range(groups)], axis=1).astype(BF16)
        xr = (_dot(w[:, :n2], part(0)) + _dot(w[:, n2:], part(1))).astype(BF16)
        for g in range(groups):
            rows = o_ref.at[0, g].reshape(n2 * kb, c)
            rows[pl.ds(t, n2, stride=kb), :] = _dot(xr[:, g * c:(g + 1) * c], wf_ref[g])


def _dft2(y, w, wf, kb):
    b, _, groups, n1, n2, c = y.shape
    return pl.pallas_call(
        _dft2_kernel,
        grid=(b, n1 // kb),
        in_specs=[pl.BlockSpec((1, 2, groups, kb, n2, c), lambda bi, j: (bi, 0, 0, j, 0, 0)),
                  pl.BlockSpec((kb, n2, 2 * n2), lambda bi, j: (j, 0, 0)),
                  _const_spec(wf.shape)],
        out_specs=pl.BlockSpec((1, groups, n2, kb, c), lambda bi, j: (bi, 0, 0, j, 0)),
        out_shape=jax.ShapeDtypeStruct((b, groups, n2, n1, c), F32),
        compiler_params=_params(("parallel", "parallel")),
        name="dft2",
    )(y, w, wf)


HALO = 16


def _tail_kernel(a_ref, ap_ref, an_ref, f_ref, fp_ref, fn_ref, x_ref, xp_ref, xn_ref,
                 woa_ref, wof_ref, gpm_ref, gpf_ref, wg_ref, wu_ref, cw_ref, cb_ref, wd_ref,
                 gpo_ref, y_ref, act_ref, *, ffc):
    i = pl.program_id(1)
    tm = x_ref.shape[1]
    rows = tm + 2 * HALO
    ext = lambda p, c, n: jnp.concatenate([p[0], c[0], n[0]], axis=0)
    f_ext = jnp.concatenate(
        [jnp.concatenate([fp_ref[0, g], f_ref[0, g], fn_ref[0, g]], axis=0)
         for g in range(FNET_GROUPS)], axis=1).astype(BF16)
    mix = _dot(ext(ap_ref, a_ref, an_ref), woa_ref[...]) + _dot(f_ext, wof_ref[...])
    x1 = ext(xp_ref, x_ref, xn_ref) + _rms(mix, gpm_ref[...])
    row = jax.lax.broadcasted_iota(jnp.int32, (rows, 1), 0)
    inside = ((row >= HALO) | (i > 0)) & ((row < HALO + tm) | (i < pl.num_programs(1) - 1))
    hext = jnp.where(inside, _rms(x1, gpf_ref[...]), 0.0).astype(BF16)
    h = hext[HALO:HALO + tm]
    for c in range(D_FF // ffc):
        sl = slice(c * ffc, (c + 1) * ffc)
        g = _dot(hext, wg_ref[:, sl])
        u = _dot(h, wu_ref[:, sl])
        g_prev = pltpu.roll(g, 1, 0)[HALO:HALO + tm]
        g_next = pltpu.roll(g, rows - 1, 0)[HALO:HALO + tm]
        gate = (cb_ref[:, sl] + g_prev * cw_ref[0:1, sl] + g[HALO:HALO + tm] * cw_ref[1:2, sl]
                + g_next * cw_ref[2:3, sl])
        inner = math.sqrt(2.0 / math.pi) * (gate + 0.044715 * (gate * gate * gate))
        act = 0.5 * gate * (1.0 + jnp.tanh(inner)) * u
        act_ref[:, sl] = act.astype(BF16)
    out = _dot(act_ref[...], wd_ref[...])
    y_ref[0] = x1[HALO:HALO + tm] + _rms(out, gpo_ref[...])


def _tail(a, f, x, woa, wof, gpm, gpf, wg, wu, cw, cb, wd, gpo, tm, ffc):
    b, s, _ = x.shape
    per = tm // HALO
    last = s // HALO - 1
    single = lambda shape: pl.BlockSpec(shape, lambda *_: (0,) * len(shape),
                                        pipeline_mode=pl.Buffered(1))

    def with_halo(width):
        return [pl.BlockSpec((1, tm, width), lambda bi, i: (bi, i, 0)),
                pl.BlockSpec((1, HALO, width), lambda bi, i: (bi, jnp.maximum(i * per - 1, 0), 0)),
                pl.BlockSpec((1, HALO, width), lambda bi, i: (bi, jnp.minimum((i + 1) * per, last), 0))]

    grouped = lambda rows, row_block: pl.BlockSpec(
        (1, FNET_GROUPS, rows, FNET_GROUP_DIM), lambda bi, i: (bi, 0, row_block(i), 0))
    f_specs = [grouped(tm, lambda i: i),
               grouped(HALO, lambda i: jnp.maximum(i * per - 1, 0)),
               grouped(HALO, lambda i: jnp.minimum((i + 1) * per, last))]
    return pl.pallas_call(
        functools.partial(_tail_kernel, ffc=ffc),
        grid=(b, s // tm),
        in_specs=with_halo(MLA_WIDTH) + f_specs + with_halo(D_MODEL) + [
            single(woa.shape), single(wof.shape), _const_spec(gpm.shape), _const_spec(gpf.shape),
            single(wg.shape), single(wu.shape), _const_spec(cw.shape), _const_spec(cb.shape),
            single(wd.shape), _const_spec(gpo.shape)],
        out_specs=pl.BlockSpec((1, tm, D_MODEL), lambda bi, i: (bi, i, 0)),
        out_shape=jax.ShapeDtypeStruct(x.shape, F32),
        scratch_shapes=[pltpu.VMEM((tm, D_FF), BF16)],
        compiler_params=_params(("parallel", "arbitrary")),
        name="tail",
    )(a, a, a, f, f, f, x, x, x, woa, wof, gpm, gpf, wg, wu, cw, cb, wd, gpo)


def _rope_tables(s):
    ang = np.arange(s, dtype=np.float64)[:, None] * (
        ROPE_THETA ** (-np.arange(0, QK_ROPE, 2, dtype=np.float64) / QK_ROPE))[None, :]
    cos, sin = np.cos(ang), np.sin(ang)
    zeros = np.zeros((s, QK_NOPE))
    zh = np.zeros((s, HALF_ROPE))
    tail = np.zeros((s, HEAD_PAD - QK_NOPE - QK_ROPE))
    cos_t = np.concatenate([np.ones((s, QK_NOPE)), cos, cos, tail], axis=1)
    sina = np.concatenate([zeros, zh, sin, tail], axis=1)
    sinb = np.concatenate([zeros, -sin, zh, tail], axis=1)
    return tuple(jnp.asarray(t, F32) for t in (cos_t, sina, sinb, cos.T, sin.T))


def _angles(num, den):
    return (2.0 * np.pi / den) * (num % den).astype(np.float64)


def _dft_tables(s):
    n2 = DFT_N2
    n1 = s // n2
    c = np.arange(FNET_GROUP_DIM, dtype=np.int64)
    ang = _angles(c[:, None] * c[None, :], FNET_GROUP_DIM)
    cs = np.concatenate([np.cos(ang), np.sin(ang)], axis=1) / math.sqrt(FNET_GROUP_DIM)
    k1 = np.arange(n1, dtype=np.int64)
    a1 = _angles(k1[:, None] * k1[None, :], n1)
    c1, s1 = np.cos(a1), np.sin(a1)
    f1 = np.concatenate([np.concatenate([c1, -s1], axis=1),
                         np.concatenate([s1, c1], axis=1)], axis=0) / math.sqrt(n1)
    k2 = np.arange(n2, dtype=np.int64)
    kk = k1[:, None, None] + n1 * k2[None, :, None]
    a2 = _angles(kk * k2[None, None, :], s)
    w = np.concatenate([np.cos(a2), -np.sin(a2)], axis=2) / math.sqrt(n2)
    return tuple(jnp.asarray(t, F32) for t in (cs, f1, w))


def _prep_weights(g_pre_mix, w_in, g_q, w_uq, g_kv, w_ukv, w_fnet, w_out, g_post_mix,
                  g_pre_ffn, w_gate, w_up, conv_w, conv_b, w_down, g_post_ffn):
    win = w_in[0]
    f_lo = Q_LORA + KV_LORA + QK_ROPE
    kr_cols = jnp.concatenate([jnp.zeros((D_MODEL, QK_NOPE), F32),
                               win[:, Q_LORA + KV_LORA:f_lo],
                               jnp.zeros((D_MODEL, HEAD_PAD - QK_NOPE - QK_ROPE), F32)], axis=1)
    win_p = jnp.concatenate([win[:, :Q_LORA + KV_LORA], win[:, f_lo:], kr_cols], axis=1)
    wuq = w_uq[0].reshape(Q_LORA, MLA_HEADS, QK_NOPE + QK_ROPE)
    wuq = jnp.pad(wuq, ((0, 0), (0, 0), (0, HEAD_PAD - QK_NOPE - QK_ROPE)))
    wuq = wuq.reshape(Q_LORA, MLA_HEADS * HEAD_PAD)
    wukv = w_ukv[0].reshape(KV_LORA, MLA_HEADS, QK_NOPE + V_HEAD)
    wuk = jnp.pad(wukv[..., :QK_NOPE], ((0, 0), (0, 0), (0, HEAD_PAD - QK_NOPE)))
    wuk = wuk.reshape(KV_LORA, MLA_HEADS * HEAD_PAD)
    wuv = wukv[..., QK_NOPE:].reshape(KV_LORA, MLA_WIDTH)
    return dict(
        gpre=g_pre_mix, win=win_p.astype(BF16), gq=g_q, wuqt=wuq.T.astype(BF16), gkv=g_kv,
        wuk=wuk.astype(BF16), wuvt=wuv.T.astype(BF16), wf=w_fnet[0].astype(BF16),
        woa=w_out[0, :MLA_WIDTH].astype(BF16), wof=w_out[0, MLA_WIDTH:].astype(BF16),
        gpm=g_post_mix, gpf=g_pre_ffn, wg=w_gate[0].astype(BF16), wu=w_up[0].astype(BF16),
        cw=conv_w[0], cb=conv_b, wd=w_down[0].astype(BF16), gpo=g_post_ffn)


def _trunk(x, w):
    b, s, _ = x.shape
    n2 = DFT_N2
    n1 = s // n2
    rope_tabs = _rope_tables(s)
    cs, f1, wtab = _dft_tables(s)
    qt, k, vt, fin = _proj(x, w["gpre"], w["win"], w["gq"], w["wuqt"], w["gkv"], w["wuk"],
                           w["wuvt"], rope_tabs, tm=TOKEN_TILE)
    a = _attn(qt, k, vt, tq=MXU_TILE, tk=s // SCORE_SLOTS, qsubs=ATTN_STEP_TILES // (s // MXU_TILE))
    y = _dft1(fin.reshape(b, FNET_GROUPS, n1, n2, FNET_GROUP_DIM), cs, f1, r=DFT1_TOKENS // n1)
    f = _dft2(y, wtab, w["wf"], kb=DFT2_ROWS)
    f = f.reshape(b, FNET_GROUPS, s, FNET_GROUP_DIM)
    return _tail(a, f, x, w["woa"], w["wof"], w["gpm"], w["gpf"], w["wg"], w["wu"], w["cw"],
                 w["cb"], w["wd"], w["gpo"], tm=TOKEN_TILE, ffc=FF_CHUNK)


def kernel(x_prompt, x_sample, g_pre_mix, w_in, g_q, w_uq, g_kv, w_ukv, w_fnet, w_out,
           g_post_mix, g_pre_ffn, w_gate, w_up, conv_w, conv_b, w_down, g_post_ffn):
    w = _prep_weights(g_pre_mix, w_in, g_q, w_uq, g_kv, w_ukv, w_fnet, w_out, g_post_mix,
                      g_pre_ffn, w_gate, w_up, conv_w, conv_b, w_down, g_post_ffn)
    return _trunk(x_prompt, w), _trunk(x_sample, w)
```

```python
import functools
import math

import jax
import jax.numpy as jnp
import numpy as np
from jax.experimental import pallas as pl
from jax.experimental.pallas import tpu as pltpu

D_MODEL = 1024
MLA_HEADS = 8
QK_NOPE = 64
QK_ROPE = 32
V_HEAD = 64
MLA_WIDTH = MLA_HEADS * V_HEAD
Q_LORA = 256
KV_LORA = 256
FNET_GROUPS = 4
FNET_WIDTH = 512
FNET_GROUP_DIM = 128
D_FF = 2816
ROPE_THETA = 10000.0
RMS_EPS = 1e-6
ATTN_SCALE = 1.0 / math.sqrt(QK_NOPE + QK_ROPE)
Q_SCALE = ATTN_SCALE * math.log2(math.e)

MXUS = 2
HEAD_PAD = 128
HALF_ROPE = QK_ROPE // 2
DENOM_ROWS = 16
MXU_TILE = 256
SCORE_ACC = (0, 64, 128)
VALUE_ACC = (192, 224)
SCORE_AHEAD = 2
SCORE_SLOTS = 4
ATTN_STEP_TILES = 128
TOKEN_TILE = 512
FF_CHUNK = 256
DFT1_TOKENS = 512
DFT2_ROWS = 8
DFT_N2 = 128
VMEM_LIMIT = 56 * 1024 * 1024

BF16 = jnp.bfloat16
F32 = jnp.float32


def _rms(x, g):
    return x * jax.lax.rsqrt(jnp.mean(x * x, axis=-1, keepdims=True) + RMS_EPS) * g


def _dot(a, b):
    return jnp.dot(a, b, preferred_element_type=F32)


def _params(sem):
    return pltpu.CompilerParams(dimension_semantics=sem, vmem_limit_bytes=VMEM_LIMIT)


def _const_spec(shape):
    zeros = (0,) * len(shape)
    return pl.BlockSpec(shape, lambda *_: zeros)


def _proj_kernel(x_ref, gpre_ref, win_ref, gq_ref, wuqt_ref, gkv_ref, wuk_ref, wuvt_ref,
                 cos_ref, sina_ref, sinb_ref, cost_ref, sint_ref, qt_ref, k_ref, vt_ref, f_ref):
    tm = x_ref.shape[1]
    h = _rms(x_ref[0], gpre_ref[...]).astype(BF16)
    p = _dot(h, win_ref[...])
    cq = _rms(p[:, :Q_LORA], gq_ref[...])
    qt = _dot(wuqt_ref[...], cq.T.astype(BF16))
    ckv = _rms(p[:, Q_LORA:Q_LORA + KV_LORA], gkv_ref[...])
    kn = _dot(ckv.astype(BF16), wuk_ref[...])
    vt_ref[0] = _dot(wuvt_ref[...], ckv.T.astype(BF16)).astype(BF16)
    cos, sina, sinb = cos_ref[...], sina_ref[...], sinb_ref[...]
    t = p[:, 2 * FNET_WIDTH:]
    kr = (t * cos + pltpu.roll(t, HALF_ROPE, 1) * sina
          + pltpu.roll(t, HEAD_PAD - HALF_ROPE, 1) * sinb)
    cos_t, sin_t = cost_ref[...], sint_ref[...]
    lo, mid, hi = QK_NOPE, QK_NOPE + HALF_ROPE, QK_NOPE + QK_ROPE
    for hd in range(MLA_HEADS):
        base = hd * HEAD_PAD
        x1, x2 = qt[base + lo:base + mid], qt[base + mid:base + hi]
        qt_ref[0, base:base + lo, :] = (qt[base:base + lo] * Q_SCALE).astype(BF16)
        qt_ref[0, base + lo:base + mid, :] = ((x1 * cos_t - x2 * sin_t) * Q_SCALE).astype(BF16)
        qt_ref[0, base + mid:base + hi, :] = ((x2 * cos_t + x1 * sin_t) * Q_SCALE).astype(BF16)
        qt_ref[0, base + hi:base + HEAD_PAD, :] = jnp.zeros((HEAD_PAD - hi, tm), BF16)
        sl = slice(base, base + HEAD_PAD)
        k_ref[0, :, sl] = (kn[:, sl] + kr).astype(BF16)
    for g in range(FNET_GROUPS):
        lo = Q_LORA + KV_LORA + g * FNET_GROUP_DIM
        f_ref[0, g] = p[:, lo:lo + FNET_GROUP_DIM]


def _proj(x, gpre, win, gq, wuqt, gkv, wuk, wuvt, rope_tabs, tm):
    b, s, _ = x.shape
    wide = MLA_HEADS * HEAD_PAD
    tok = lambda w: pl.BlockSpec((1, tm, w), lambda bi, i: (bi, i, 0))
    tab = pl.BlockSpec((tm, HEAD_PAD), lambda bi, i: (i, 0))
    tab_t = pl.BlockSpec((HALF_ROPE, tm), lambda bi, i: (0, i))
    return pl.pallas_call(
        _proj_kernel,
        grid=(b, s // tm),
        in_specs=[tok(D_MODEL), _const_spec(gpre.shape), _const_spec(win.shape),
                  _const_spec(gq.shape), _const_spec(wuqt.shape), _const_spec(gkv.shape),
                  _const_spec(wuk.shape), _const_spec(wuvt.shape),
                  tab, tab, tab, tab_t, tab_t],
        out_specs=[pl.BlockSpec((1, wide, tm), lambda bi, i: (bi, 0, i)),
                   tok(wide),
                   pl.BlockSpec((1, MLA_WIDTH, tm), lambda bi, i: (bi, 0, i)),
                   pl.BlockSpec((1, FNET_GROUPS, tm, FNET_GROUP_DIM),
                                lambda bi, i: (bi, 0, i, 0))],
        out_shape=[jax.ShapeDtypeStruct((b, wide, s), BF16),
                   jax.ShapeDtypeStruct((b, s, wide), BF16),
                   jax.ShapeDtypeStruct((b, MLA_WIDTH, s), BF16),
                   jax.ShapeDtypeStruct((b, FNET_GROUPS, s, FNET_GROUP_DIM), F32)],
        compiler_params=_params(("parallel", "parallel")),
        name="proj",
    )(x, gpre, win, gq, wuqt, gkv, wuk, wuvt, *rope_tabs)


def _attn_kernel(qt_ref, k_ref, vt_ref, o_ref, st_all, mc_all, *, tq, tk, heads, qsubs):
    st_ref = tuple(st_all.at[k] for k in range(SCORE_SLOTS))
    mc_ref = tuple(mc_all.at[k] for k in range(SCORE_SLOTS))
    s = k_ref.shape[1]
    n = s // tk
    nt = tk // MXU_TILE
    i = pl.program_id(2)
    ones = jnp.ones((DENOM_ROWS, MXU_TILE), BF16)
    pad_k = jnp.zeros((MXU_TILE, MXU_TILE - HEAD_PAD), BF16)
    pad_q = jnp.zeros((MXU_TILE - HEAD_PAD, tq), BF16)

    def q_rhs(hd, qi):
        q_off = qi * tq if isinstance(qi, int) else pl.multiple_of(qi * tq, tq)
        qt = qt_ref[0, hd * HEAD_PAD:(hd + 1) * HEAD_PAD, pl.ds(q_off, tq)]
        return jnp.concatenate([qt, pad_q], axis=0)

    def k_lhs(hd, c, j):
        k_lo = c * tk + j * MXU_TILE
        kc = k_ref[0, k_lo:k_lo + MXU_TILE, hd * HEAD_PAD:(hd + 1) * HEAD_PAD]
        return jnp.concatenate([kc, pad_k], axis=1)

    def pop_scores(hd, j, slot, mx):
        st = pltpu.matmul_pop(SCORE_ACC[j % len(SCORE_ACC)], (MXU_TILE, tq), F32, hd)
        st_ref[slot][hd, j * MXU_TILE:(j + 1) * MXU_TILE, :] = st
        return jnp.maximum(mx, jnp.max(st, axis=0, keepdims=True))

    lag = len(SCORE_ACC) - 1

    @pl.when(i == 0)
    def _():
        for c in range(SCORE_AHEAD):
            mx = [jnp.full((1, tq), -jnp.inf, F32) for _ in range(heads)]
            for hd in range(heads):
                pltpu.matmul_push_rhs(q_rhs(hd, 0), 0, hd)
            for j in range(nt + lag):
                for hd in range(heads):
                    if j < nt:
                        pltpu.matmul_acc_lhs(SCORE_ACC[j % len(SCORE_ACC)], k_lhs(hd, c, j), hd,
                                             0 if j == 0 else None)
                    if j >= lag:
                        mx[hd] = pop_scores(hd, j - lag, c, mx[hd])
            for hd in range(heads):
                mc_ref[c][hd] = mx[hd]

    pending = []
    tile = 0
    block_accs = []

    def drain(keep):
        while len(pending) > keep:
            pending.pop(0)()

    for qsub in range(qsubs):
        qb = i * qsubs + qsub
        ms = [jnp.full((1, tq), -jnp.inf, F32) for _ in range(heads)]
        accs = [jnp.zeros((V_HEAD + DENOM_ROWS, tq), F32) for _ in range(heads)]
        block_accs.append(accs)
        for slot in range(n):
            ahead = slot + SCORE_AHEAD
            if ahead < n:
                qi, c_next = qb, ahead
            elif qsub + 1 < qsubs:
                qi, c_next = qb + 1, ahead - n
            else:
                qi, c_next = jnp.minimum(qb + 1, s // tq - 1), ahead - n
            fill = ahead % SCORE_SLOTS
            value_acc = VALUE_ACC[slot % len(VALUE_ACC)]
            m_old = list(ms)
            ms = [jnp.maximum(ms[hd], mc_ref[slot][hd]) for hd in range(heads)]
            mx = [jnp.full((1, tq), -jnp.inf, F32) for _ in range(heads)]
            for j in range(nt):
                score_acc = SCORE_ACC[tile % len(SCORE_ACC)]
                tile += 1
                for hd in range(heads):
                    rows = slice(j * MXU_TILE, (j + 1) * MXU_TILE)
                    p = jnp.exp2(st_ref[slot][hd, rows, :] - ms[hd]).astype(BF16)
                    pltpu.matmul_push_rhs(p, 1, hd)
                    pltpu.matmul_push_rhs(q_rhs(hd, qi), 0, hd)
                    v_lo = slot * tk + j * MXU_TILE
                    vc = vt_ref[0, hd * V_HEAD:(hd + 1) * V_HEAD, v_lo:v_lo + MXU_TILE]
                    pltpu.matmul_acc_lhs(value_acc, jnp.concatenate([vc, ones], axis=0), hd, 1)
                    pltpu.matmul_acc_lhs(score_acc, k_lhs(hd, c_next, j), hd, 0)

                def pop_tile(j=j, fill=fill, mx=mx, score_acc=score_acc, value_acc=value_acc,
                             m_old=m_old, m_new=list(ms), accs=accs):
                    for hd in range(heads):
                        st = pltpu.matmul_pop(score_acc, (MXU_TILE, tq), F32, hd)
                        st_ref[fill][hd, j * MXU_TILE:(j + 1) * MXU_TILE, :] = st
                        mx[hd] = jnp.maximum(mx[hd], jnp.max(st, axis=0, keepdims=True))
                        if j == nt - 1:
                            mc_ref[fill][hd] = mx[hd]
                            pv = pltpu.matmul_pop(value_acc, (V_HEAD + DENOM_ROWS, tq), F32, hd)
                            accs[hd] = jnp.exp2(m_old[hd] - m_new[hd]) * accs[hd] + pv

                drain(lag - 1)
                pending.append(pop_tile)
    drain(0)
    for qsub, accs in enumerate(block_accs):
        outs = [acc[:V_HEAD] / acc[V_HEAD:V_HEAD + 1] for acc in accs]
        o_ref[0, qsub * tq:(qsub + 1) * tq, :] = jnp.concatenate(outs, axis=0).T.astype(BF16)


def _attn(qt, k, vt, tq, tk, qsubs):
    b, _, s = qt.shape
    heads = MXUS
    groups = MLA_HEADS // heads
    assert s % (qsubs * tq) == 0 and s == SCORE_SLOTS * tk and tq == MXU_TILE and tk % MXU_TILE == 0
    return pl.pallas_call(
        functools.partial(_attn_kernel, tq=tq, tk=tk, heads=heads, qsubs=qsubs),
        grid=(b, groups, s // (qsubs * tq)),
        in_specs=[pl.BlockSpec((1, heads * HEAD_PAD, s), lambda bi, j, i: (bi, j, 0)),
                  pl.BlockSpec((1, s, heads * HEAD_PAD), lambda bi, j, i: (bi, 0, j)),
                  pl.BlockSpec((1, heads * V_HEAD, s), lambda bi, j, i: (bi, j, 0))],
        out_specs=pl.BlockSpec((1, qsubs * tq, heads * V_HEAD), lambda bi, j, i: (bi, i, j)),
        out_shape=jax.ShapeDtypeStruct((b, s, MLA_WIDTH), BF16),
        scratch_shapes=[pltpu.VMEM((SCORE_SLOTS, heads, tk, tq), F32),
                        pltpu.VMEM((SCORE_SLOTS, heads, 1, tq), F32)],
        compiler_params=_params(("parallel", "parallel", "arbitrary")),
        name="attn",
    )(qt, k, vt)


def _dft1_kernel(x_ref, cs_ref, f1_ref, y_ref):
    groups, n1, r, c = x_ref.shape[1:]
    f1 = f1_ref[...].astype(BF16)
    rows = [pl.ds(j, n1, stride=r) for j in range(r)]
    flat_x = lambda g: x_ref.at[0, g].reshape(n1 * r, c)
    flat_y = lambda part, g: y_ref.at[0, part, g].reshape(n1 * r, c)
    x = jnp.concatenate([flat_x(g)[rows[j], :] for j in range(r) for g in range(groups)], axis=0)
    z = _dot(x.astype(BF16), cs_ref[...].astype(BF16)).astype(BF16)
    for j in range(r):
        blk = [z[(j * groups + g) * n1:(j * groups + g + 1) * n1] for g in range(groups)]
        zr = jnp.concatenate([t[:, :c] for t in blk], axis=1)
        zi = jnp.concatenate([t[:, c:] for t in blk], axis=1)
        y = _dot(f1[:, :n1], zr) + _dot(f1[:, n1:], zi)
        for g in range(groups):
            flat_y(0, g)[rows[j], :] = y[:n1, g * c:(g + 1) * c]
            flat_y(1, g)[rows[j], :] = y[n1:, g * c:(g + 1) * c]


def _dft1(x, cs, f1, r):
    b, groups, n1, n2, c = x.shape
    return pl.pallas_call(
        _dft1_kernel,
        grid=(b, n2 // r),
        in_specs=[pl.BlockSpec((1, groups, n1, r, c), lambda bi, j: (bi, 0, 0, j, 0)),
                  _const_spec(cs.shape), _const_spec(f1.shape)],
        out_specs=pl.BlockSpec((1, 2, groups, n1, r, c), lambda bi, j: (bi, 0, 0, 0, j, 0)),
        out_shape=jax.ShapeDtypeStruct((b, 2, groups, n1, n2, c), F32),
        compiler_params=_params(("parallel", "parallel")),
        name="dft1",
    )(x, cs, f1)


def _dft2_kernel(y_ref, w_ref, wf_ref, o_ref):
    groups, kb, n2, c = y_ref.shape[2:]
    for t in range(kb):
        w = w_ref[t].astype(BF16)
        part = lambda p: jnp.concatenate(
            [y_ref[0, p, g, t] for g in range(groups)], axis=1).astype(BF16)
        xr = (_dot(w[:, :n2], part(0)) + _dot(w[:, n2:], part(1))).astype(BF16)
        for g in range(groups):
            rows = o_ref.at[0, g].reshape(n2 * kb, c)
            rows[pl.ds(t, n2, stride=kb), :] = _dot(xr[:, g * c:(g + 1) * c], wf_ref[g])


def _dft2(y, w, wf, kb):
    b, _, groups, n1, n2, c = y.shape
    return pl.pallas_call(
        _dft2_kernel,
        grid=(b, n1 // kb),
        in_specs=[pl.BlockSpec((1, 2, groups, kb, n2, c), lambda bi, j: (bi, 0, 0, j, 0, 0)),
                  pl.BlockSpec((kb, n2, 2 * n2), lambda bi, j: (j, 0, 0)),
                  _const_spec(wf.shape)],
        out_specs=pl.BlockSpec((1, groups, n2, kb, c), lambda bi, j: (bi, 0, 0, j, 0)),
        out_shape=jax.ShapeDtypeStruct((b, groups, n2, n1, c), F32),
        compiler_params=_params(("parallel", "parallel")),
        name="dft2",
    )(y, w, wf)


HALO = 16


def _tail_kernel(a_ref, ap_ref, an_ref, f_ref, fp_ref, fn_ref, x_ref, xp_ref, xn_ref,
                 woa_ref, wof_ref, gpm_ref, gpf_ref, wg_ref, wu_ref, cw_ref, cb_ref, wd_ref,
                 gpo_ref, y_ref, act_ref, *, ffc):
    i = pl.program_id(1)
    tm = x_ref.shape[1]
    rows = tm + 2 * HALO
    ext = lambda p, c, n: jnp.concatenate([p[0], c[0], n[0]], axis=0)
    f_ext = jnp.concatenate(
        [jnp.concatenate([fp_ref[0, g], f_ref[0, g], fn_ref[0, g]], axis=0)
         for g in range(FNET_GROUPS)], axis=1).astype(BF16)
    mix = _dot(ext(ap_ref, a_ref, an_ref), woa_ref[...]) + _dot(f_ext, wof_ref[...])
    x1 = ext(xp_ref, x_ref, xn_ref) + _rms(mix, gpm_ref[...])
    row = jax.lax.broadcasted_iota(jnp.int32, (rows, 1), 0)
    inside = ((row >= HALO) | (i > 0)) & ((row < HALO + tm) | (i < pl.num_programs(1) - 1))
    hext = jnp.where(inside, _rms(x1, gpf_ref[...]), 0.0).astype(BF16)
    h = hext[HALO:HALO + tm]
    for c in range(D_FF // ffc):
        sl = slice(c * ffc, (c + 1) * ffc)
        g = _dot(hext, wg_ref[:, sl])
        u = _dot(h, wu_ref[:, sl])
        g_prev = pltpu.roll(g, 1, 0)[HALO:HALO + tm]
        g_next = pltpu.roll(g, rows - 1, 0)[HALO:HALO + tm]
        gate = (cb_ref[:, sl] + g_prev * cw_ref[0:1, sl] + g[HALO:HALO + tm] * cw_ref[1:2, sl]
                + g_next * cw_ref[2:3, sl])
        inner = math.sqrt(2.0 / math.pi) * (gate + 0.044715 * (gate * gate * gate))
        act = 0.5 * gate * (1.0 + jnp.tanh(inner)) * u
        act_ref[:, sl] = act.astype(BF16)
    out = _dot(act_ref[...], wd_ref[...])
    y_ref[0] = x1[HALO:HALO + tm] + _rms(out, gpo_ref[...])


def _tail(a, f, x, woa, wof, gpm, gpf, wg, wu, cw, cb, wd, gpo, tm, ffc):
    b, s, _ = x.shape
    per = tm // HALO
    last = s // HALO - 1
    single = lambda shape: pl.BlockSpec(shape, lambda *_: (0,) * len(shape),
                                        pipeline_mode=pl.Buffered(1))

    def with_halo(width):
        return [pl.BlockSpec((1, tm, width), lambda bi, i: (bi, i, 0)),
                pl.BlockSpec((1, HALO, width), lambda bi, i: (bi, jnp.maximum(i * per - 1, 0), 0)),
                pl.BlockSpec((1, HALO, width), lambda bi, i: (bi, jnp.minimum((i + 1) * per, last), 0))]

    grouped = lambda rows, row_block: pl.BlockSpec(
        (1, FNET_GROUPS, rows, FNET_GROUP_DIM), lambda bi, i: (bi, 0, row_block(i), 0))
    f_specs = [grouped(tm, lambda i: i),
               grouped(HALO, lambda i: jnp.maximum(i * per - 1, 0)),
               grouped(HALO, lambda i: jnp.minimum((i + 1) * per, last))]
    return pl.pallas_call(
        functools.partial(_tail_kernel, ffc=ffc),
        grid=(b, s // tm),
        in_specs=with_halo(MLA_WIDTH) + f_specs + with_halo(D_MODEL) + [
            single(woa.shape), single(wof.shape), _const_spec(gpm.shape), _const_spec(gpf.shape),
            single(wg.shape), single(wu.shape), _const_spec(cw.shape), _const_spec(cb.shape),
            single(wd.shape), _const_spec(gpo.shape)],
        out_specs=pl.BlockSpec((1, tm, D_MODEL), lambda bi, i: (bi, i, 0)),
        out_shape=jax.ShapeDtypeStruct(x.shape, F32),
        scratch_shapes=[pltpu.VMEM((tm, D_FF), BF16)],
        compiler_params=_params(("parallel", "arbitrary")),
        name="tail",
    )(a, a, a, f, f, f, x, x, x, woa, wof, gpm, gpf, wg, wu, cw, cb, wd, gpo)


def _rope_tables(s):
    ang = np.arange(s, dtype=np.float64)[:, None] * (
        ROPE_THETA ** (-np.arange(0, QK_ROPE, 2, dtype=np.float64) / QK_ROPE))[None, :]
    cos, sin = np.cos(ang), np.sin(ang)
    zeros = np.zeros((s, QK_NOPE))
    zh = np.zeros((s, HALF_ROPE))
    tail = np.zeros((s, HEAD_PAD - QK_NOPE - QK_ROPE))
    cos_t = np.concatenate([np.ones((s, QK_NOPE)), cos, cos, tail], axis=1)
    sina = np.concatenate([zeros, zh, sin, tail], axis=1)
    sinb = np.concatenate([zeros, -sin, zh, tail], axis=1)
    return tuple(jnp.asarray(t, F32) for t in (cos_t, sina, sinb, cos.T, sin.T))


def _angles(num, den):
    return (2.0 * np.pi / den) * (num % den).astype(np.float64)


def _dft_tables(s):
    n2 = DFT_N2
    n1 = s // n2
    c = np.arange(FNET_GROUP_DIM, dtype=np.int64)
    ang = _angles(c[:, None] * c[None, :], FNET_GROUP_DIM)
    cs = np.concatenate([np.cos(ang), np.sin(ang)], axis=1) / math.sqrt(FNET_GROUP_DIM)
    k1 = np.arange(n1, dtype=np.int64)
    a1 = _angles(k1[:, None] * k1[None, :], n1)
    c1, s1 = np.cos(a1), np.sin(a1)
    f1 = np.concatenate([np.concatenate([c1, -s1], axis=1),
                         np.concatenate([s1, c1], axis=1)], axis=0) / math.sqrt(n1)
    k2 = np.arange(n2, dtype=np.int64)
    kk = k1[:, None, None] + n1 * k2[None, :, None]
    a2 = _angles(kk * k2[None, None, :], s)
    w = np.concatenate([np.cos(a2), -np.sin(a2)], axis=2) / math.sqrt(n2)
    return tuple(jnp.asarray(t, F32) for t in (cs, f1, w))


def _prep_weights(g_pre_mix, w_in, g_q, w_uq, g_kv, w_ukv, w_fnet, w_out, g_post_mix,
                  g_pre_ffn, w_gate, w_up, conv_w, conv_b, w_down, g_post_ffn):
    win = w_in[0]
    f_lo = Q_LORA + KV_LORA + QK_ROPE
    kr_cols = jnp.concatenate([jnp.zeros((D_MODEL, QK_NOPE), F32),
                               win[:, Q_LORA + KV_LORA:f_lo],
                               jnp.zeros((D_MODEL, HEAD_PAD - QK_NOPE - QK_ROPE), F32)], axis=1)
    win_p = jnp.concatenate([win[:, :Q_LORA + KV_LORA], win[:, f_lo:], kr_cols], axis=1)
    wuq = w_uq[0].reshape(Q_LORA, MLA_HEADS, QK_NOPE + QK_ROPE)
    wuq = jnp.pad(wuq, ((0, 0), (0, 0), (0, HEAD_PAD - QK_NOPE - QK_ROPE)))
    wuq = wuq.reshape(Q_LORA, MLA_HEADS * HEAD_PAD)
    wukv = w_ukv[0].reshape(KV_LORA, MLA_HEADS, QK_NOPE + V_HEAD)
    wuk = jnp.pad(wukv[..., :QK_NOPE], ((0, 0), (0, 0), (0, HEAD_PAD - QK_NOPE)))
    wuk = wuk.reshape(KV_LORA, MLA_HEADS * HEAD_PAD)
    wuv = wukv[..., QK_NOPE:].reshape(KV_LORA, MLA_WIDTH)
    return dict(
        gpre=g_pre_mix, win=win_p.astype(BF16), gq=g_q, wuqt=wuq.T.astype(BF16), gkv=g_kv,
        wuk=wuk.astype(BF16), wuvt=wuv.T.astype(BF16), wf=w_fnet[0].astype(BF16),
        woa=w_out[0, :MLA_WIDTH].astype(BF16), wof=w_out[0, MLA_WIDTH:].astype(BF16),
        gpm=g_post_mix, gpf=g_pre_ffn, wg=w_gate[0].astype(BF16), wu=w_up[0].astype(BF16),
        cw=conv_w[0], cb=conv_b, wd=w_down[0].astype(BF16), gpo=g_post_ffn)


def _trunk(x, w):
    b, s, _ = x.shape
    n2 = DFT_N2
    n1 = s // n2
    rope_tabs = _rope_tables(s)
    cs, f1, wtab = _dft_tables(s)
    qt, k, vt, fin = _proj(x, w["gpre"], w["win"], w["gq"], w["wuqt"], w["gkv"], w["wuk"],
                           w["wuvt"], rope_tabs, tm=TOKEN_TILE)
    a = _attn(qt, k, vt, tq=MXU_TILE, tk=s // SCORE_SLOTS, qsubs=ATTN_STEP_TILES // (s // MXU_TILE))
    y = _dft1(fin.reshape(b, FNET_GROUPS, n1, n2, FNET_GROUP_DIM), cs, f1, r=DFT1_TOKENS // n1)
    f = _dft2(y, wtab, w["wf"], kb=DFT2_ROWS)
    f = f.reshape(b, FNET_GROUPS, s, FNET_GROUP_DIM)
    return _tail(a, f, x, w["woa"], w["wof"], w["gpm"], w["gpf"], w["wg"], w["wu"], w["cw"],
                 w["cb"], w["wd"], w["gpo"], tm=TOKEN_TILE, ffc=FF_CHUNK)


def kernel(x_prompt, x_sample, g_pre_mix, w_in, g_q, w_uq, g_kv, w_ukv, w_fnet, w_out,
           g_post_mix, g_pre_ffn, w_gate, w_up, conv_w, conv_b, w_down, g_post_ffn):
    w = _prep_weights(g_pre_mix, w_in, g_q, w_uq, g_kv, w_ukv, w_fnet, w_out, g_post_mix,
                      g_pre_ffn, w_gate, w_up, conv_w, conv_b, w_down, g_post_ffn)
    return _trunk(x_prompt, w), _trunk(x_sample, w)
```

```python
import functools
import math

import jax
import jax.numpy as jnp
import numpy as np
from jax.experimental import pallas as pl
from jax.experimental.pallas import tpu as pltpu

D_MODEL = 1024
MLA_HEADS = 8
QK_NOPE = 64
QK_ROPE = 32
V_HEAD = 64
MLA_WIDTH = MLA_HEADS * V_HEAD
Q_LORA = 256
KV_LORA = 256
FNET_GROUPS = 4
FNET_WIDTH = 512
FNET_GROUP_DIM = 128
D_FF = 2816
ROPE_THETA = 10000.0
RMS_EPS = 1e-6
ATTN_SCALE = 1.0 / math.sqrt(QK_NOPE + QK_ROPE)
Q_SCALE = ATTN_SCALE * math.log2(math.e)

MXUS = 2
HEAD_PAD = 128
HALF_ROPE = QK_ROPE // 2
DENOM_ROWS = 16
MXU_TILE = 256
SCORE_ACC = (0, 64, 128)
VALUE_ACC = (192, 224)
SCORE_AHEAD = 2
SCORE_SLOTS = 4
ATTN_STEP_TILES = 128
PROJ_TILE = 1024
TOKEN_TILE = 512
FF_CHUNK = 256
DFT1_TOKENS = 512
DFT2_ROWS = 8
DFT_N2 = 128
VMEM_LIMIT = 56 * 1024 * 1024

BF16 = jnp.bfloat16
F32 = jnp.float32


def _rms(x, g):
    return x * jax.lax.rsqrt(jnp.mean(x * x, axis=-1, keepdims=True) + RMS_EPS) * g


def _dot(a, b):
    return jnp.dot(a, b, preferred_element_type=F32)


def _params(sem):
    return pltpu.CompilerParams(dimension_semantics=sem, vmem_limit_bytes=VMEM_LIMIT)


def _const_spec(shape):
    zeros = (0,) * len(shape)
    return pl.BlockSpec(shape, lambda *_: zeros)


def _proj_kernel(x_ref, gpre_ref, win_ref, gq_ref, wuqt_ref, gkv_ref, wuk_ref, wuvt_ref,
                 cos_ref, sina_ref, sinb_ref, cost_ref, sint_ref, qt_ref, k_ref, vt_ref, f_ref):
    tm = x_ref.shape[1]
    h = _rms(x_ref[0], gpre_ref[...]).astype(BF16)
    p = _dot(h, win_ref[...])
    cq = _rms(p[:, :Q_LORA], gq_ref[...])
    qt = _dot(wuqt_ref[...], cq.T.astype(BF16))
    ckv = _rms(p[:, Q_LORA:Q_LORA + KV_LORA], gkv_ref[...])
    kn = _dot(ckv.astype(BF16), wuk_ref[...])
    vt_ref[0] = _dot(wuvt_ref[...], ckv.T.astype(BF16)).astype(BF16)
    cos, sina, sinb = cos_ref[...], sina_ref[...], sinb_ref[...]
    t = p[:, 2 * FNET_WIDTH:]
    kr = (t * cos + pltpu.roll(t, HALF_ROPE, 1) * sina
          + pltpu.roll(t, HEAD_PAD - HALF_ROPE, 1) * sinb)
    cos_t, sin_t = cost_ref[...], sint_ref[...]
    lo, mid, hi = QK_NOPE, QK_NOPE + HALF_ROPE, QK_NOPE + QK_ROPE
    for hd in range(MLA_HEADS):
        base = hd * HEAD_PAD
        x1, x2 = qt[base + lo:base + mid], qt[base + mid:base + hi]
        qt_ref[0, base:base + lo, :] = (qt[base:base + lo] * Q_SCALE).astype(BF16)
        qt_ref[0, base + lo:base + mid, :] = ((x1 * cos_t - x2 * sin_t) * Q_SCALE).astype(BF16)
        qt_ref[0, base + mid:base + hi, :] = ((x2 * cos_t + x1 * sin_t) * Q_SCALE).astype(BF16)
        qt_ref[0, base + hi:base + HEAD_PAD, :] = jnp.zeros((HEAD_PAD - hi, tm), BF16)
        sl = slice(base, base + HEAD_PAD)
        k_ref[0, :, sl] = (kn[:, sl] + kr).astype(BF16)
    for g in range(FNET_GROUPS):
        lo = Q_LORA + KV_LORA + g * FNET_GROUP_DIM
        f_ref[0, g] = p[:, lo:lo + FNET_GROUP_DIM]


def _proj(x, gpre, win, gq, wuqt, gkv, wuk, wuvt, rope_tabs, tm):
    b, s, _ = x.shape
    wide = MLA_HEADS * HEAD_PAD
    tok = lambda w: pl.BlockSpec((1, tm, w), lambda bi, i: (bi, i, 0))
    tab = pl.BlockSpec((tm, HEAD_PAD), lambda bi, i: (i, 0))
    tab_t = pl.BlockSpec((HALF_ROPE, tm), lambda bi, i: (0, i))
    return pl.pallas_call(
        _proj_kernel,
        grid=(b, s // tm),
        in_specs=[tok(D_MODEL), _const_spec(gpre.shape), _const_spec(win.shape),
                  _const_spec(gq.shape), _const_spec(wuqt.shape), _const_spec(gkv.shape),
                  _const_spec(wuk.shape), _const_spec(wuvt.shape),
                  tab, tab, tab, tab_t, tab_t],
        out_specs=[pl.BlockSpec((1, wide, tm), lambda bi, i: (bi, 0, i)),
                   tok(wide),
                   pl.BlockSpec((1, MLA_WIDTH, tm), lambda bi, i: (bi, 0, i)),
                   pl.BlockSpec((1, FNET_GROUPS, tm, FNET_GROUP_DIM),
                                lambda bi, i: (bi, 0, i, 0))],
        out_shape=[jax.ShapeDtypeStruct((b, wide, s), BF16),
                   jax.ShapeDtypeStruct((b, s, wide), BF16),
                   jax.ShapeDtypeStruct((b, MLA_WIDTH, s), BF16),
                   jax.ShapeDtypeStruct((b, FNET_GROUPS, s, FNET_GROUP_DIM), F32)],
        compiler_params=_params(("parallel", "parallel")),
        name="proj",
    )(x, gpre, win, gq, wuqt, gkv, wuk, wuvt, *rope_tabs)


def _attn_kernel(qt_ref, k_ref, vt_ref, o_ref, st_all, mc_all, *, tq, tk, heads, qsubs):
    st_ref = tuple(st_all.at[k] for k in range(SCORE_SLOTS))
    mc_ref = tuple(mc_all.at[k] for k in range(SCORE_SLOTS))
    s = k_ref.shape[1]
    n = s // tk
    nt = tk // MXU_TILE
    i = pl.program_id(2)
    ones = jnp.ones((DENOM_ROWS, MXU_TILE), BF16)
    pad_k = jnp.zeros((MXU_TILE, MXU_TILE - HEAD_PAD), BF16)
    pad_q = jnp.zeros((MXU_TILE - HEAD_PAD, tq), BF16)

    def q_rhs(hd, qi):
        q_off = qi * tq if isinstance(qi, int) else pl.multiple_of(qi * tq, tq)
        qt = qt_ref[0, hd * HEAD_PAD:(hd + 1) * HEAD_PAD, pl.ds(q_off, tq)]
        return jnp.concatenate([qt, pad_q], axis=0)

    def k_lhs(hd, c, j):
        k_lo = c * tk + j * MXU_TILE
        kc = k_ref[0, k_lo:k_lo + MXU_TILE, hd * HEAD_PAD:(hd + 1) * HEAD_PAD]
        return jnp.concatenate([kc, pad_k], axis=1)

    def pop_scores(hd, j, slot, mx):
        st = pltpu.matmul_pop(SCORE_ACC[j % len(SCORE_ACC)], (MXU_TILE, tq), F32, hd)
        st_ref[slot][hd, j * MXU_TILE:(j + 1) * MXU_TILE, :] = st
        return jnp.maximum(mx, jnp.max(st, axis=0, keepdims=True))

    lag = len(SCORE_ACC) - 1

    @pl.when(i == 0)
    def _():
        for c in range(SCORE_AHEAD):
            mx = [jnp.full((1, tq), -jnp.inf, F32) for _ in range(heads)]
            for hd in range(heads):
                pltpu.matmul_push_rhs(q_rhs(hd, 0), 0, hd)
            for j in range(nt + lag):
                for hd in range(heads):
                    if j < nt:
                        pltpu.matmul_acc_lhs(SCORE_ACC[j % len(SCORE_ACC)], k_lhs(hd, c, j), hd,
                                             0 if j == 0 else None)
                    if j >= lag:
                        mx[hd] = pop_scores(hd, j - lag, c, mx[hd])
            for hd in range(heads):
                mc_ref[c][hd] = mx[hd]

    pending = []
    tile = 0
    block_accs = []

    def drain(keep):
        while len(pending) > keep:
            pending.pop(0)()

    for qsub in range(qsubs):
        qb = i * qsubs + qsub
        ms = [jnp.full((1, tq), -jnp.inf, F32) for _ in range(heads)]
        accs = [jnp.zeros((V_HEAD + DENOM_ROWS, tq), F32) for _ in range(heads)]
        block_accs.append(accs)
        for slot in range(n):
            ahead = slot + SCORE_AHEAD
            if ahead < n:
                qi, c_next = qb, ahead
            elif qsub + 1 < qsubs:
                qi, c_next = qb + 1, ahead - n
            else:
                qi, c_next = jnp.minimum(qb + 1, s // tq - 1), ahead - n
            fill = ahead % SCORE_SLOTS
            value_acc = VALUE_ACC[slot % len(VALUE_ACC)]
            m_old = list(ms)
            ms = [jnp.maximum(ms[hd], mc_ref[slot][hd]) for hd in range(heads)]
            mx = [jnp.full((1, tq), -jnp.inf, F32) for _ in range(heads)]
            for j in range(nt):
                score_acc = SCORE_ACC[tile % len(SCORE_ACC)]
                tile += 1
                for hd in range(heads):
                    rows = slice(j * MXU_TILE, (j + 1) * MXU_TILE)
                    p = jnp.exp2(st_ref[slot][hd, rows, :] - ms[hd]).astype(BF16)
                    pltpu.matmul_push_rhs(p, 1, hd)
                    pltpu.matmul_push_rhs(q_rhs(hd, qi), 0, hd)
                    v_lo = slot * tk + j * MXU_TILE
                    vc = vt_ref[0, hd * V_HEAD:(hd + 1) * V_HEAD, v_lo:v_lo + MXU_TILE]
                    pltpu.matmul_acc_lhs(value_acc, jnp.concatenate([vc, ones], axis=0), hd, 1)
                    pltpu.matmul_acc_lhs(score_acc, k_lhs(hd, c_next, j), hd, 0)

                def pop_tile(j=j, fill=fill, mx=mx, score_acc=score_acc, value_acc=value_acc,
                             m_old=m_old, m_new=list(ms), accs=accs):
                    for hd in range(heads):
                        st = pltpu.matmul_pop(score_acc, (MXU_TILE, tq), F32, hd)
                        st_ref[fill][hd, j * MXU_TILE:(j + 1) * MXU_TILE, :] = st
                        mx[hd] = jnp.maximum(mx[hd], jnp.max(st, axis=0, keepdims=True))
                        if j == nt - 1:
                            mc_ref[fill][hd] = mx[hd]
                            pv = pltpu.matmul_pop(value_acc, (V_HEAD + DENOM_ROWS, tq), F32, hd)
                            accs[hd] = jnp.exp2(m_old[hd] - m_new[hd]) * accs[hd] + pv

                drain(lag - 1)
                pending.append(pop_tile)
    drain(0)
    for qsub, accs in enumerate(block_accs):
        outs = [acc[:V_HEAD] / acc[V_HEAD:V_HEAD + 1] for acc in accs]
        o_ref[0, qsub * tq:(qsub + 1) * tq, :] = jnp.concatenate(outs, axis=0).T.astype(BF16)


def _attn(qt, k, vt, tq, tk, qsubs):
    b, _, s = qt.shape
    heads = MXUS
    groups = MLA_HEADS // heads
    assert s % (qsubs * tq) == 0 and s == SCORE_SLOTS * tk and tq == MXU_TILE and tk % MXU_TILE == 0
    return pl.pallas_call(
        functools.partial(_attn_kernel, tq=tq, tk=tk, heads=heads, qsubs=qsubs),
        grid=(b, groups, s // (qsubs * tq)),
        in_specs=[pl.BlockSpec((1, heads * HEAD_PAD, s), lambda bi, j, i: (bi, j, 0)),
                  pl.BlockSpec((1, s, heads * HEAD_PAD), lambda bi, j, i: (bi, 0, j)),
                  pl.BlockSpec((1, heads * V_HEAD, s), lambda bi, j, i: (bi, j, 0))],
        out_specs=pl.BlockSpec((1, qsubs * tq, heads * V_HEAD), lambda bi, j, i: (bi, i, j)),
        out_shape=jax.ShapeDtypeStruct((b, s, MLA_WIDTH), BF16),
        scratch_shapes=[pltpu.VMEM((SCORE_SLOTS, heads, tk, tq), F32),
                        pltpu.VMEM((SCORE_SLOTS, heads, 1, tq), F32)],
        compiler_params=_params(("parallel", "parallel", "arbitrary")),
        name="attn",
    )(qt, k, vt)


def _dft1_kernel(x_ref, cs_ref, f1_ref, y_ref):
    groups, n1, r, c = x_ref.shape[1:]
    f1 = f1_ref[...].astype(BF16)
    rows = [pl.ds(j, n1, stride=r) for j in range(r)]
    flat_x = lambda g: x_ref.at[0, g].reshape(n1 * r, c)
    flat_y = lambda part, g: y_ref.at[0, part, g].reshape(n1 * r, c)
    x = jnp.concatenate([flat_x(g)[rows[j], :] for j in range(r) for g in range(groups)], axis=0)
    z = _dot(x.astype(BF16), cs_ref[...].astype(BF16)).astype(BF16)
    for j in range(r):
        blk = [z[(j * groups + g) * n1:(j * groups + g + 1) * n1] for g in range(groups)]
        zr = jnp.concatenate([t[:, :c] for t in blk], axis=1)
        zi = jnp.concatenate([t[:, c:] for t in blk], axis=1)
        y = _dot(f1[:, :n1], zr) + _dot(f1[:, n1:], zi)
        for g in range(groups):
            flat_y(0, g)[rows[j], :] = y[:n1, g * c:(g + 1) * c]
            flat_y(1, g)[rows[j], :] = y[n1:, g * c:(g + 1) * c]


def _dft1(x, cs, f1, r):
    b, groups, n1, n2, c = x.shape
    return pl.pallas_call(
        _dft1_kernel,
        grid=(b, n2 // r),
        in_specs=[pl.BlockSpec((1, groups, n1, r, c), lambda bi, j: (bi, 0, 0, j, 0)),
                  _const_spec(cs.shape), _const_spec(f1.shape)],
        out_specs=pl.BlockSpec((1, 2, groups, n1, r, c), lambda bi, j: (bi, 0, 0, 0, j, 0)),
        out_shape=jax.ShapeDtypeStruct((b, 2, groups, n1, n2, c), F32),
        compiler_params=_params(("parallel", "parallel")),
        name="dft1",
    )(x, cs, f1)


def _dft2_kernel(y_ref, w_ref, wf_ref, o_ref):
    groups, kb, n2, c = y_ref.shape[2:]
    for t in range(kb):
        w = w_ref[t].astype(BF16)
        part = lambda p: jnp.concatenate(
            [y_ref[0, p, g, t] for g in range(groups)], axis=1).astype(BF16)
        xr = (_dot(w[:, :n2], part(0)) + _dot(w[:, n2:], part(1))).astype(BF16)
        for g in range(groups):
            rows = o_ref.at[0, g].reshape(n2 * kb, c)
            rows[pl.ds(t, n2, stride=kb), :] = _dot(xr[:, g * c:(g + 1) * c], wf_ref[g])


def _dft2(y, w, wf, kb):
    b, _, groups, n1, n2, c = y.shape
    return pl.pallas_call(
        _dft2_kernel,
        grid=(b, n1 // kb),
        in_specs=[pl.BlockSpec((1, 2, groups, kb, n2, c), lambda bi, j: (bi, 0, 0, j, 0, 0)),
                  pl.BlockSpec((kb, n2, 2 * n2), lambda bi, j: (j, 0, 0)),
                  _const_spec(wf.shape)],
        out_specs=pl.BlockSpec((1, groups, n2, kb, c), lambda bi, j: (bi, 0, 0, j, 0)),
        out_shape=jax.ShapeDtypeStruct((b, groups, n2, n1, c), F32),
        compiler_params=_params(("parallel", "parallel")),
        name="dft2",
    )(y, w, wf)


HALO = 16


def _tail_kernel(a_ref, ap_ref, an_ref, f_ref, fp_ref, fn_ref, x_ref, xp_ref, xn_ref,
                 woa_ref, wof_ref, gpm_ref, gpf_ref, wg_ref, wu_ref, cw_ref, cb_ref, wd_ref,
                 gpo_ref, y_ref, act_ref, *, ffc):
    i = pl.program_id(1)
    tm = x_ref.shape[1]
    rows = tm + 2 * HALO
    ext = lambda p, c, n: jnp.concatenate([p[0], c[0], n[0]], axis=0)
    f_ext = jnp.concatenate(
        [jnp.concatenate([fp_ref[0, g], f_ref[0, g], fn_ref[0, g]], axis=0)
         for g in range(FNET_GROUPS)], axis=1).astype(BF16)
    mix = _dot(ext(ap_ref, a_ref, an_ref), woa_ref[...]) + _dot(f_ext, wof_ref[...])
    x1 = ext(xp_ref, x_ref, xn_ref) + _rms(mix, gpm_ref[...])
    row = jax.lax.broadcasted_iota(jnp.int32, (rows, 1), 0)
    inside = ((row >= HALO) | (i > 0)) & ((row < HALO + tm) | (i < pl.num_programs(1) - 1))
    hext = jnp.where(inside, _rms(x1, gpf_ref[...]), 0.0).astype(BF16)
    h = hext[HALO:HALO + tm]
    for c in range(D_FF // ffc):
        sl = slice(c * ffc, (c + 1) * ffc)
        g = _dot(hext, wg_ref[:, sl])
        u = _dot(h, wu_ref[:, sl])
        g_prev = pltpu.roll(g, 1, 0)[HALO:HALO + tm]
        g_next = pltpu.roll(g, rows - 1, 0)[HALO:HALO + tm]
        gate = (cb_ref[:, sl] + g_prev * cw_ref[0:1, sl] + g[HALO:HALO + tm] * cw_ref[1:2, sl]
                + g_next * cw_ref[2:3, sl])
        inner = math.sqrt(2.0 / math.pi) * (gate + 0.044715 * (gate * gate * gate))
        act = 0.5 * gate * (1.0 + jnp.tanh(inner)) * u
        act_ref[:, sl] = act.astype(BF16)
    out = _dot(act_ref[...], wd_ref[...])
    y_ref[0] = x1[HALO:HALO + tm] + _rms(out, gpo_ref[...])


def _tail(a, f, x, woa, wof, gpm, gpf, wg, wu, cw, cb, wd, gpo, tm, ffc):
    b, s, _ = x.shape
    per = tm // HALO
    last = s // HALO - 1
    single = lambda shape: pl.BlockSpec(shape, lambda *_: (0,) * len(shape),
                                        pipeline_mode=pl.Buffered(1))

    def with_halo(width):
        return [pl.BlockSpec((1, tm, width), lambda bi, i: (bi, i, 0)),
                pl.BlockSpec((1, HALO, width), lambda bi, i: (bi, jnp.maximum(i * per - 1, 0), 0)),
                pl.BlockSpec((1, HALO, width), lambda bi, i: (bi, jnp.minimum((i + 1) * per, last), 0))]

    grouped = lambda rows, row_block: pl.BlockSpec(
        (1, FNET_GROUPS, rows, FNET_GROUP_DIM), lambda bi, i: (bi, 0, row_block(i), 0))
    f_specs = [grouped(tm, lambda i: i),
               grouped(HALO, lambda i: jnp.maximum(i * per - 1, 0)),
               grouped(HALO, lambda i: jnp.minimum((i + 1) * per, last))]
    return pl.pallas_call(
        functools.partial(_tail_kernel, ffc=ffc),
        grid=(b, s // tm),
        in_specs=with_halo(MLA_WIDTH) + f_specs + with_halo(D_MODEL) + [
            single(woa.shape), single(wof.shape), _const_spec(gpm.shape), _const_spec(gpf.shape),
            single(wg.shape), single(wu.shape), _const_spec(cw.shape), _const_spec(cb.shape),
            single(wd.shape), _const_spec(gpo.shape)],
        out_specs=pl.BlockSpec((1, tm, D_MODEL), lambda bi, i: (bi, i, 0)),
        out_shape=jax.ShapeDtypeStruct(x.shape, F32),
        scratch_shapes=[pltpu.VMEM((tm, D_FF), BF16)],
        compiler_params=_params(("parallel", "arbitrary")),
        name="tail",
    )(a, a, a, f, f, f, x, x, x, woa, wof, gpm, gpf, wg, wu, cw, cb, wd, gpo)


def _rope_tables(s):
    ang = np.arange(s, dtype=np.float64)[:, None] * (
        ROPE_THETA ** (-np.arange(0, QK_ROPE, 2, dtype=np.float64) / QK_ROPE))[None, :]
    cos, sin = np.cos(ang), np.sin(ang)
    zeros = np.zeros((s, QK_NOPE))
    zh = np.zeros((s, HALF_ROPE))
    tail = np.zeros((s, HEAD_PAD - QK_NOPE - QK_ROPE))
    cos_t = np.concatenate([np.ones((s, QK_NOPE)), cos, cos, tail], axis=1)
    sina = np.concatenate([zeros, zh, sin, tail], axis=1)
    sinb = np.concatenate([zeros, -sin, zh, tail], axis=1)
    return tuple(jnp.asarray(t, F32) for t in (cos_t, sina, sinb, cos.T, sin.T))


def _angles(num, den):
    return (2.0 * np.pi / den) * (num % den).astype(np.float64)


def _dft_tables(s):
    n2 = DFT_N2
    n1 = s // n2
    c = np.arange(FNET_GROUP_DIM, dtype=np.int64)
    ang = _angles(c[:, None] * c[None, :], FNET_GROUP_DIM)
    cs = np.concatenate([np.cos(ang), np.sin(ang)], axis=1) / math.sqrt(FNET_GROUP_DIM)
    k1 = np.arange(n1, dtype=np.int64)
    a1 = _angles(k1[:, None] * k1[None, :], n1)
    c1, s1 = np.cos(a1), np.sin(a1)
    f1 = np.concatenate([np.concatenate([c1, -s1], axis=1),
                         np.concatenate([s1, c1], axis=1)], axis=0) / math.sqrt(n1)
    k2 = np.arange(n2, dtype=np.int64)
    kk = k1[:, None, None] + n1 * k2[None, :, None]
    a2 = _angles(kk * k2[None, None, :], s)
    w = np.concatenate([np.cos(a2), -np.sin(a2)], axis=2) / math.sqrt(n2)
    return tuple(jnp.asarray(t, F32) for t in (cs, f1, w))


def _prep_weights(g_pre_mix, w_in, g_q, w_uq, g_kv, w_ukv, w_fnet, w_out, g_post_mix,
                  g_pre_ffn, w_gate, w_up, conv_w, conv_b, w_down, g_post_ffn):
    win = w_in[0]
    f_lo = Q_LORA + KV_LORA + QK_ROPE
    kr_cols = jnp.concatenate([jnp.zeros((D_MODEL, QK_NOPE), F32),
                               win[:, Q_LORA + KV_LORA:f_lo],
                               jnp.zeros((D_MODEL, HEAD_PAD - QK_NOPE - QK_ROPE), F32)], axis=1)
    win_p = jnp.concatenate([win[:, :Q_LORA + KV_LORA], win[:, f_lo:], kr_cols], axis=1)
    wuq = w_uq[0].reshape(Q_LORA, MLA_HEADS, QK_NOPE + QK_ROPE)
    wuq = jnp.pad(wuq, ((0, 0), (0, 0), (0, HEAD_PAD - QK_NOPE - QK_ROPE)))
    wuq = wuq.reshape(Q_LORA, MLA_HEADS * HEAD_PAD)
    wukv = w_ukv[0].reshape(KV_LORA, MLA_HEADS, QK_NOPE + V_HEAD)
    wuk = jnp.pad(wukv[..., :QK_NOPE], ((0, 0), (0, 0), (0, HEAD_PAD - QK_NOPE)))
    wuk = wuk.reshape(KV_LORA, MLA_HEADS * HEAD_PAD)
    wuv = wukv[..., QK_NOPE:].reshape(KV_LORA, MLA_WIDTH)
    return dict(
        gpre=g_pre_mix, win=win_p.astype(BF16), gq=g_q, wuqt=wuq.T.astype(BF16), gkv=g_kv,
        wuk=wuk.astype(BF16), wuvt=wuv.T.astype(BF16), wf=w_fnet[0].astype(BF16),
        woa=w_out[0, :MLA_WIDTH].astype(BF16), wof=w_out[0, MLA_WIDTH:].astype(BF16),
        gpm=g_post_mix, gpf=g_pre_ffn, wg=w_gate[0].astype(BF16), wu=w_up[0].astype(BF16),
        cw=conv_w[0], cb=conv_b, wd=w_down[0].astype(BF16), gpo=g_post_ffn)


def _trunk(x, w):
    b, s, _ = x.shape
    n2 = DFT_N2
    n1 = s // n2
    rope_tabs = _rope_tables(s)
    cs, f1, wtab = _dft_tables(s)
    qt, k, vt, fin = _proj(x, w["gpre"], w["win"], w["gq"], w["wuqt"], w["gkv"], w["wuk"],
                           w["wuvt"], rope_tabs, tm=PROJ_TILE)
    a = _attn(qt, k, vt, tq=MXU_TILE, tk=s // SCORE_SLOTS, qsubs=ATTN_STEP_TILES // (s // MXU_TILE))
    y = _dft1(fin.reshape(b, FNET_GROUPS, n1, n2, FNET_GROUP_DIM), cs, f1, r=DFT1_TOKENS // n1)
    f = _dft2(y, wtab, w["wf"], kb=DFT2_ROWS)
    f = f.reshape(b, FNET_GROUPS, s, FNET_GROUP_DIM)
    return _tail(a, f, x, w["woa"], w["wof"], w["gpm"], w["gpf"], w["wg"], w["wu"], w["cw"],
                 w["cb"], w["wd"], w["gpo"], tm=TOKEN_TILE, ffc=FF_CHUNK)


def kernel(x_prompt, x_sample, g_pre_mix, w_in, g_q, w_uq, g_kv, w_ukv, w_fnet, w_out,
           g_post_mix, g_pre_ffn, w_gate, w_up, conv_w, conv_b, w_down, g_post_ffn):
    w = _prep_weights(g_pre_mix, w_in, g_q, w_uq, g_kv, w_ukv, w_fnet, w_out, g_post_mix,
                      g_pre_ffn, w_gate, w_up, conv_w, conv_b, w_down, g_post_ffn)
    return _trunk(x_prompt, w), _trunk(x_sample, w)
```

```python
import functools
import math

import jax
import jax.numpy as jnp
import numpy as np
from jax.experimental import pallas as pl
from jax.experimental.pallas import tpu as pltpu

D_MODEL = 1024
MLA_HEADS = 8
QK_NOPE = 64
QK_ROPE = 32
V_HEAD = 64
MLA_WIDTH = MLA_HEADS * V_HEAD
Q_LORA = 256
KV_LORA = 256
FNET_GROUPS = 4
FNET_WIDTH = 512
FNET_GROUP_DIM = 128
D_FF = 2816
ROPE_THETA = 10000.0
RMS_EPS = 1e-6
ATTN_SCALE = 1.0 / math.sqrt(QK_NOPE + QK_ROPE)
Q_SCALE = ATTN_SCALE * math.log2(math.e)

MXUS = 2
HEAD_PAD = 128
HALF_ROPE = QK_ROPE // 2
DENOM_ROWS = 16
MXU_TILE = 256
SCORE_ACC = (0, 64, 128)
VALUE_ACC = (192, 224)
SCORE_AHEAD = 2
SCORE_SLOTS = 4
ATTN_STEP_TILES = 128
PROJ_TILE = 1024
TOKEN_TILE = 512
FF_CHUNK = 256
DFT1_ROWS = 8
DFT2_ROWS = 8
DFT_N2 = 128
VMEM_LIMIT = 56 * 1024 * 1024

BF16 = jnp.bfloat16
F32 = jnp.float32


def _rms(x, g):
    return x * jax.lax.rsqrt(jnp.mean(x * x, axis=-1, keepdims=True) + RMS_EPS) * g


def _dot(a, b):
    return jnp.dot(a, b, preferred_element_type=F32)


def _params(sem):
    return pltpu.CompilerParams(dimension_semantics=sem, vmem_limit_bytes=VMEM_LIMIT)


def _const_spec(shape):
    zeros = (0,) * len(shape)
    return pl.BlockSpec(shape, lambda *_: zeros)


def _proj_kernel(x_ref, gpre_ref, win_ref, gq_ref, wuqt_ref, gkv_ref, wuk_ref, wuvt_ref,
                 cos_ref, sina_ref, sinb_ref, cost_ref, sint_ref, qt_ref, k_ref, vt_ref, f_ref):
    tm = x_ref.shape[1]
    h = _rms(x_ref[0], gpre_ref[...]).astype(BF16)
    p = _dot(h, win_ref[...])
    cq = _rms(p[:, :Q_LORA], gq_ref[...])
    qt = _dot(wuqt_ref[...], cq.T.astype(BF16))
    ckv = _rms(p[:, Q_LORA:Q_LORA + KV_LORA], gkv_ref[...])
    kn = _dot(ckv.astype(BF16), wuk_ref[...])
    vt_ref[0] = _dot(wuvt_ref[...], ckv.T.astype(BF16)).astype(BF16)
    cos, sina, sinb = cos_ref[...], sina_ref[...], sinb_ref[...]
    t = p[:, 2 * FNET_WIDTH:]
    kr = (t * cos + pltpu.roll(t, HALF_ROPE, 1) * sina
          + pltpu.roll(t, HEAD_PAD - HALF_ROPE, 1) * sinb)
    cos_t, sin_t = cost_ref[...], sint_ref[...]
    lo, mid, hi = QK_NOPE, QK_NOPE + HALF_ROPE, QK_NOPE + QK_ROPE
    for hd in range(MLA_HEADS):
        base = hd * HEAD_PAD
        x1, x2 = qt[base + lo:base + mid], qt[base + mid:base + hi]
        qt_ref[0, base:base + lo, :] = (qt[base:base + lo] * Q_SCALE).astype(BF16)
        qt_ref[0, base + lo:base + mid, :] = ((x1 * cos_t - x2 * sin_t) * Q_SCALE).astype(BF16)
        qt_ref[0, base + mid:base + hi, :] = ((x2 * cos_t + x1 * sin_t) * Q_SCALE).astype(BF16)
        qt_ref[0, base + hi:base + HEAD_PAD, :] = jnp.zeros((HEAD_PAD - hi, tm), BF16)
        sl = slice(base, base + HEAD_PAD)
        k_ref[0, :, sl] = (kn[:, sl] + kr).astype(BF16)
    for g in range(FNET_GROUPS):
        lo = Q_LORA + KV_LORA + g * FNET_GROUP_DIM
        f_ref[0, g] = p[:, lo:lo + FNET_GROUP_DIM]


def _proj(x, gpre, win, gq, wuqt, gkv, wuk, wuvt, rope_tabs, tm):
    b, s, _ = x.shape
    wide = MLA_HEADS * HEAD_PAD
    tok = lambda w: pl.BlockSpec((1, tm, w), lambda bi, i: (bi, i, 0))
    tab = pl.BlockSpec((tm, HEAD_PAD), lambda bi, i: (i, 0))
    tab_t = pl.BlockSpec((HALF_ROPE, tm), lambda bi, i: (0, i))
    return pl.pallas_call(
        _proj_kernel,
        grid=(b, s // tm),
        in_specs=[tok(D_MODEL), _const_spec(gpre.shape), _const_spec(win.shape),
                  _const_spec(gq.shape), _const_spec(wuqt.shape), _const_spec(gkv.shape),
                  _const_spec(wuk.shape), _const_spec(wuvt.shape),
                  tab, tab, tab, tab_t, tab_t],
        out_specs=[pl.BlockSpec((1, wide, tm), lambda bi, i: (bi, 0, i)),
                   tok(wide),
                   pl.BlockSpec((1, MLA_WIDTH, tm), lambda bi, i: (bi, 0, i)),
                   pl.BlockSpec((1, FNET_GROUPS, tm, FNET_GROUP_DIM),
                                lambda bi, i: (bi, 0, i, 0))],
        out_shape=[jax.ShapeDtypeStruct((b, wide, s), BF16),
                   jax.ShapeDtypeStruct((b, s, wide), BF16),
                   jax.ShapeDtypeStruct((b, MLA_WIDTH, s), BF16),
                   jax.ShapeDtypeStruct((b, FNET_GROUPS, s, FNET_GROUP_DIM), F32)],
        compiler_params=_params(("parallel", "parallel")),
        name="proj",
    )(x, gpre, win, gq, wuqt, gkv, wuk, wuvt, *rope_tabs)


def _attn_kernel(qt_ref, k_ref, vt_ref, o_ref, st_all, mc_all, *, tq, tk, heads, qsubs):
    st_ref = tuple(st_all.at[k] for k in range(SCORE_SLOTS))
    mc_ref = tuple(mc_all.at[k] for k in range(SCORE_SLOTS))
    s = k_ref.shape[1]
    n = s // tk
    nt = tk // MXU_TILE
    i = pl.program_id(2)
    ones = jnp.ones((DENOM_ROWS, MXU_TILE), BF16)
    pad_k = jnp.zeros((MXU_TILE, MXU_TILE - HEAD_PAD), BF16)
    pad_q = jnp.zeros((MXU_TILE - HEAD_PAD, tq), BF16)

    def q_rhs(hd, qi):
        q_off = qi * tq if isinstance(qi, int) else pl.multiple_of(qi * tq, tq)
        qt = qt_ref[0, hd * HEAD_PAD:(hd + 1) * HEAD_PAD, pl.ds(q_off, tq)]
        return jnp.concatenate([qt, pad_q], axis=0)

    def k_lhs(hd, c, j):
        k_lo = c * tk + j * MXU_TILE
        kc = k_ref[0, k_lo:k_lo + MXU_TILE, hd * HEAD_PAD:(hd + 1) * HEAD_PAD]
        return jnp.concatenate([kc, pad_k], axis=1)

    def pop_scores(hd, j, slot, mx):
        st = pltpu.matmul_pop(SCORE_ACC[j % len(SCORE_ACC)], (MXU_TILE, tq), F32, hd)
        st_ref[slot][hd, j * MXU_TILE:(j + 1) * MXU_TILE, :] = st
        return jnp.maximum(mx, jnp.max(st, axis=0, keepdims=True))

    lag = len(SCORE_ACC) - 1

    @pl.when(i == 0)
    def _():
        for c in range(SCORE_AHEAD):
            mx = [jnp.full((1, tq), -jnp.inf, F32) for _ in range(heads)]
            for hd in range(heads):
                pltpu.matmul_push_rhs(q_rhs(hd, 0), 0, hd)
            for j in range(nt + lag):
                for hd in range(heads):
                    if j < nt:
                        pltpu.matmul_acc_lhs(SCORE_ACC[j % len(SCORE_ACC)], k_lhs(hd, c, j), hd,
                                             0 if j == 0 else None)
                    if j >= lag:
                        mx[hd] = pop_scores(hd, j - lag, c, mx[hd])
            for hd in range(heads):
                mc_ref[c][hd] = mx[hd]

    pending = []
    tile = 0
    block_accs = []

    def drain(keep):
        while len(pending) > keep:
            pending.pop(0)()

    for qsub in range(qsubs):
        qb = i * qsubs + qsub
        ms = [jnp.full((1, tq), -jnp.inf, F32) for _ in range(heads)]
        accs = [jnp.zeros((V_HEAD + DENOM_ROWS, tq), F32) for _ in range(heads)]
        block_accs.append(accs)
        for slot in range(n):
            ahead = slot + SCORE_AHEAD
            if ahead < n:
                qi, c_next = qb, ahead
            elif qsub + 1 < qsubs:
                qi, c_next = qb + 1, ahead - n
            else:
                qi, c_next = jnp.minimum(qb + 1, s // tq - 1), ahead - n
            fill = ahead % SCORE_SLOTS
            value_acc = VALUE_ACC[slot % len(VALUE_ACC)]
            m_old = list(ms)
            ms = [jnp.maximum(ms[hd], mc_ref[slot][hd]) for hd in range(heads)]
            mx = [jnp.full((1, tq), -jnp.inf, F32) for _ in range(heads)]
            for j in range(nt):
                score_acc = SCORE_ACC[tile % len(SCORE_ACC)]
                tile += 1
                for hd in range(heads):
                    rows = slice(j * MXU_TILE, (j + 1) * MXU_TILE)
                    p = jnp.exp2(st_ref[slot][hd, rows, :] - ms[hd]).astype(BF16)
                    pltpu.matmul_push_rhs(p, 1, hd)
                    pltpu.matmul_push_rhs(q_rhs(hd, qi), 0, hd)
                    v_lo = slot * tk + j * MXU_TILE
                    vc = vt_ref[0, hd * V_HEAD:(hd + 1) * V_HEAD, v_lo:v_lo + MXU_TILE]
                    pltpu.matmul_acc_lhs(value_acc, jnp.concatenate([vc, ones], axis=0), hd, 1)
                    pltpu.matmul_acc_lhs(score_acc, k_lhs(hd, c_next, j), hd, 0)

                def pop_tile(j=j, fill=fill, mx=mx, score_acc=score_acc, value_acc=value_acc,
                             m_old=m_old, m_new=list(ms), accs=accs):
                    for hd in range(heads):
                        st = pltpu.matmul_pop(score_acc, (MXU_TILE, tq), F32, hd)
                        st_ref[fill][hd, j * MXU_TILE:(j + 1) * MXU_TILE, :] = st
                        mx[hd] = jnp.maximum(mx[hd], jnp.max(st, axis=0, keepdims=True))
                        if j == nt - 1:
                            mc_ref[fill][hd] = mx[hd]
                            pv = pltpu.matmul_pop(value_acc, (V_HEAD + DENOM_ROWS, tq), F32, hd)
                            accs[hd] = jnp.exp2(m_old[hd] - m_new[hd]) * accs[hd] + pv

                drain(lag - 1)
                pending.append(pop_tile)
    drain(0)
    for qsub, accs in enumerate(block_accs):
        outs = [acc[:V_HEAD] / acc[V_HEAD:V_HEAD + 1] for acc in accs]
        o_ref[0, qsub * tq:(qsub + 1) * tq, :] = jnp.concatenate(outs, axis=0).T.astype(BF16)


def _attn(qt, k, vt, tq, tk, qsubs):
    b, _, s = qt.shape
    heads = MXUS
    groups = MLA_HEADS // heads
    assert s % (qsubs * tq) == 0 and s == SCORE_SLOTS * tk and tq == MXU_TILE and tk % MXU_TILE == 0
    return pl.pallas_call(
        functools.partial(_attn_kernel, tq=tq, tk=tk, heads=heads, qsubs=qsubs),
        grid=(b, groups, s // (qsubs * tq)),
        in_specs=[pl.BlockSpec((1, heads * HEAD_PAD, s), lambda bi, j, i: (bi, j, 0)),
                  pl.BlockSpec((1, s, heads * HEAD_PAD), lambda bi, j, i: (bi, 0, j)),
                  pl.BlockSpec((1, heads * V_HEAD, s), lambda bi, j, i: (bi, j, 0))],
        out_specs=pl.BlockSpec((1, qsubs * tq, heads * V_HEAD), lambda bi, j, i: (bi, i, j)),
        out_shape=jax.ShapeDtypeStruct((b, s, MLA_WIDTH), BF16),
        scratch_shapes=[pltpu.VMEM((SCORE_SLOTS, heads, tk, tq), F32),
                        pltpu.VMEM((SCORE_SLOTS, heads, 1, tq), F32)],
        compiler_params=_params(("parallel", "parallel", "arbitrary")),
        name="attn",
    )(qt, k, vt)


def _dft1_kernel(x_ref, cs_ref, f1_ref, y_ref):
    groups, n1, r, c = x_ref.shape[1:]
    f1 = f1_ref[...].astype(BF16)
    rows = [pl.ds(j, n1, stride=r) for j in range(r)]
    flat_x = lambda g: x_ref.at[0, g].reshape(n1 * r, c)
    flat_y = lambda part, g: y_ref.at[0, part, g].reshape(n1 * r, c)
    x = jnp.concatenate([flat_x(g)[rows[j], :] for j in range(r) for g in range(groups)], axis=0)
    z = _dot(x.astype(BF16), cs_ref[...].astype(BF16)).astype(BF16)
    for j in range(r):
        blk = [z[(j * groups + g) * n1:(j * groups + g + 1) * n1] for g in range(groups)]
        zr = jnp.concatenate([t[:, :c] for t in blk], axis=1)
        zi = jnp.concatenate([t[:, c:] for t in blk], axis=1)
        y = _dot(f1[:, :n1], zr) + _dot(f1[:, n1:], zi)
        for g in range(groups):
            flat_y(0, g)[rows[j], :] = y[:n1, g * c:(g + 1) * c]
            flat_y(1, g)[rows[j], :] = y[n1:, g * c:(g + 1) * c]


def _dft1(x, cs, f1, r):
    b, groups, n1, n2, c = x.shape
    return pl.pallas_call(
        _dft1_kernel,
        grid=(b, n2 // r),
        in_specs=[pl.BlockSpec((1, groups, n1, r, c), lambda bi, j: (bi, 0, 0, j, 0)),
                  _const_spec(cs.shape), _const_spec(f1.shape)],
        out_specs=pl.BlockSpec((1, 2, groups, n1, r, c), lambda bi, j: (bi, 0, 0, 0, j, 0)),
        out_shape=jax.ShapeDtypeStruct((b, 2, groups, n1, n2, c), F32),
        compiler_params=_params(("parallel", "parallel")),
        name="dft1",
    )(x, cs, f1)


def _dft2_kernel(y_ref, w_ref, wf_ref, o_ref):
    groups, kb, n2, c = y_ref.shape[2:]
    for t in range(kb):
        w = w_ref[t].astype(BF16)
        part = lambda p: jnp.concatenate(
            [y_ref[0, p, g, t] for g in range(groups)], axis=1).astype(BF16)
        xr = (_dot(w[:, :n2], part(0)) + _dot(w[:, n2:], part(1))).astype(BF16)
        for g in range(groups):
            rows = o_ref.at[0, g].reshape(n2 * kb, c)
            rows[pl.ds(t, n2, stride=kb), :] = _dot(xr[:, g * c:(g + 1) * c], wf_ref[g])


def _dft2(y, w, wf, kb):
    b, _, groups, n1, n2, c = y.shape
    return pl.pallas_call(
        _dft2_kernel,
        grid=(b, n1 // kb),
        in_specs=[pl.BlockSpec((1, 2, groups, kb, n2, c), lambda bi, j: (bi, 0, 0, j, 0, 0)),
                  pl.BlockSpec((kb, n2, 2 * n2), lambda bi, j: (j, 0, 0)),
                  _const_spec(wf.shape)],
        out_specs=pl.BlockSpec((1, groups, n2, kb, c), lambda bi, j: (bi, 0, 0, j, 0)),
        out_shape=jax.ShapeDtypeStruct((b, groups, n2, n1, c), F32),
        compiler_params=_params(("parallel", "parallel")),
        name="dft2",
    )(y, w, wf)


HALO = 16


def _tail_kernel(a_ref, ap_ref, an_ref, f_ref, fp_ref, fn_ref, x_ref, xp_ref, xn_ref,
                 woa_ref, wof_ref, gpm_ref, gpf_ref, wg_ref, wu_ref, cw_ref, cb_ref, wd_ref,
                 gpo_ref, y_ref, act_ref, *, ffc):
    i = pl.program_id(1)
    tm = x_ref.shape[1]
    rows = tm + 2 * HALO
    ext = lambda p, c, n: jnp.concatenate([p[0], c[0], n[0]], axis=0)
    f_ext = jnp.concatenate(
        [jnp.concatenate([fp_ref[0, g], f_ref[0, g], fn_ref[0, g]], axis=0)
         for g in range(FNET_GROUPS)], axis=1).astype(BF16)
    mix = _dot(ext(ap_ref, a_ref, an_ref), woa_ref[...]) + _dot(f_ext, wof_ref[...])
    x1 = ext(xp_ref, x_ref, xn_ref) + _rms(mix, gpm_ref[...])
    row = jax.lax.broadcasted_iota(jnp.int32, (rows, 1), 0)
    inside = ((row >= HALO) | (i > 0)) & ((row < HALO + tm) | (i < pl.num_programs(1) - 1))
    hext = jnp.where(inside, _rms(x1, gpf_ref[...]), 0.0).astype(BF16)
    h = hext[HALO:HALO + tm]
    for c in range(D_FF // ffc):
        sl = slice(c * ffc, (c + 1) * ffc)
        g = _dot(hext, wg_ref[:, sl])
        u = _dot(h, wu_ref[:, sl])
        g_prev = pltpu.roll(g, 1, 0)[HALO:HALO + tm]
        g_next = pltpu.roll(g, rows - 1, 0)[HALO:HALO + tm]
        gate = (cb_ref[:, sl] + g_prev * cw_ref[0:1, sl] + g[HALO:HALO + tm] * cw_ref[1:2, sl]
                + g_next * cw_ref[2:3, sl])
        inner = math.sqrt(2.0 / math.pi) * (gate + 0.044715 * (gate * gate * gate))
        act = 0.5 * gate * (1.0 + jnp.tanh(inner)) * u
        act_ref[:, sl] = act.astype(BF16)
    out = _dot(act_ref[...], wd_ref[...])
    y_ref[0] = x1[HALO:HALO + tm] + _rms(out, gpo_ref[...])


def _tail(a, f, x, woa, wof, gpm, gpf, wg, wu, cw, cb, wd, gpo, tm, ffc):
    b, s, _ = x.shape
    per = tm // HALO
    last = s // HALO - 1
    single = lambda shape: pl.BlockSpec(shape, lambda *_: (0,) * len(shape),
                                        pipeline_mode=pl.Buffered(1))

    def with_halo(width):
        return [pl.BlockSpec((1, tm, width), lambda bi, i: (bi, i, 0)),
                pl.BlockSpec((1, HALO, width), lambda bi, i: (bi, jnp.maximum(i * per - 1, 0), 0)),
                pl.BlockSpec((1, HALO, width), lambda bi, i: (bi, jnp.minimum((i + 1) * per, last), 0))]

    grouped = lambda rows, row_block: pl.BlockSpec(
        (1, FNET_GROUPS, rows, FNET_GROUP_DIM), lambda bi, i: (bi, 0, row_block(i), 0))
    f_specs = [grouped(tm, lambda i: i),
               grouped(HALO, lambda i: jnp.maximum(i * per - 1, 0)),
               grouped(HALO, lambda i: jnp.minimum((i + 1) * per, last))]
    return pl.pallas_call(
        functools.partial(_tail_kernel, ffc=ffc),
        grid=(b, s // tm),
        in_specs=with_halo(MLA_WIDTH) + f_specs + with_halo(D_MODEL) + [
            single(woa.shape), single(wof.shape), _const_spec(gpm.shape), _const_spec(gpf.shape),
            single(wg.shape), single(wu.shape), _const_spec(cw.shape), _const_spec(cb.shape),
            single(wd.shape), _const_spec(gpo.shape)],
        out_specs=pl.BlockSpec((1, tm, D_MODEL), lambda bi, i: (bi, i, 0)),
        out_shape=jax.ShapeDtypeStruct(x.shape, F32),
        scratch_shapes=[pltpu.VMEM((tm, D_FF), BF16)],
        compiler_params=_params(("parallel", "arbitrary")),
        name="tail",
    )(a, a, a, f, f, f, x, x, x, woa, wof, gpm, gpf, wg, wu, cw, cb, wd, gpo)


def _rope_tables(s):
    ang = np.arange(s, dtype=np.float64)[:, None] * (
        ROPE_THETA ** (-np.arange(0, QK_ROPE, 2, dtype=np.float64) / QK_ROPE))[None, :]
    cos, sin = np.cos(ang), np.sin(ang)
    zeros = np.zeros((s, QK_NOPE))
    zh = np.zeros((s, HALF_ROPE))
    tail = np.zeros((s, HEAD_PAD - QK_NOPE - QK_ROPE))
    cos_t = np.concatenate([np.ones((s, QK_NOPE)), cos, cos, tail], axis=1)
    sina = np.concatenate([zeros, zh, sin, tail], axis=1)
    sinb = np.concatenate([zeros, -sin, zh, tail], axis=1)
    return tuple(jnp.asarray(t, F32) for t in (cos_t, sina, sinb, cos.T, sin.T))


def _angles(num, den):
    return (2.0 * np.pi / den) * (num % den).astype(np.float64)


def _dft_tables(s):
    n2 = DFT_N2
    n1 = s // n2
    c = np.arange(FNET_GROUP_DIM, dtype=np.int64)
    ang = _angles(c[:, None] * c[None, :], FNET_GROUP_DIM)
    cs = np.concatenate([np.cos(ang), np.sin(ang)], axis=1) / math.sqrt(FNET_GROUP_DIM)
    k1 = np.arange(n1, dtype=np.int64)
    a1 = _angles(k1[:, None] * k1[None, :], n1)
    c1, s1 = np.cos(a1), np.sin(a1)
    f1 = np.concatenate([np.concatenate([c1, -s1], axis=1),
                         np.concatenate([s1, c1], axis=1)], axis=0) / math.sqrt(n1)
    k2 = np.arange(n2, dtype=np.int64)
    kk = k1[:, None, None] + n1 * k2[None, :, None]
    a2 = _angles(kk * k2[None, None, :], s)
    w = np.concatenate([np.cos(a2), -np.sin(a2)], axis=2) / math.sqrt(n2)
    return tuple(jnp.asarray(t, F32) for t in (cs, f1, w))


def _prep_weights(g_pre_mix, w_in, g_q, w_uq, g_kv, w_ukv, w_fnet, w_out, g_post_mix,
                  g_pre_ffn, w_gate, w_up, conv_w, conv_b, w_down, g_post_ffn):
    win = w_in[0]
    f_lo = Q_LORA + KV_LORA + QK_ROPE
    kr_cols = jnp.concatenate([jnp.zeros((D_MODEL, QK_NOPE), F32),
                               win[:, Q_LORA + KV_LORA:f_lo],
                               jnp.zeros((D_MODEL, HEAD_PAD - QK_NOPE - QK_ROPE), F32)], axis=1)
    win_p = jnp.concatenate([win[:, :Q_LORA + KV_LORA], win[:, f_lo:], kr_cols], axis=1)
    wuq = w_uq[0].reshape(Q_LORA, MLA_HEADS, QK_NOPE + QK_ROPE)
    wuq = jnp.pad(wuq, ((0, 0), (0, 0), (0, HEAD_PAD - QK_NOPE - QK_ROPE)))
    wuq = wuq.reshape(Q_LORA, MLA_HEADS * HEAD_PAD)
    wukv = w_ukv[0].reshape(KV_LORA, MLA_HEADS, QK_NOPE + V_HEAD)
    wuk = jnp.pad(wukv[..., :QK_NOPE], ((0, 0), (0, 0), (0, HEAD_PAD - QK_NOPE)))
    wuk = wuk.reshape(KV_LORA, MLA_HEADS * HEAD_PAD)
    wuv = wukv[..., QK_NOPE:].reshape(KV_LORA, MLA_WIDTH)
    return dict(
        gpre=g_pre_mix, win=win_p.astype(BF16), gq=g_q, wuqt=wuq.T.astype(BF16), gkv=g_kv,
        wuk=wuk.astype(BF16), wuvt=wuv.T.astype(BF16), wf=w_fnet[0].astype(BF16),
        woa=w_out[0, :MLA_WIDTH].astype(BF16), wof=w_out[0, MLA_WIDTH:].astype(BF16),
        gpm=g_post_mix, gpf=g_pre_ffn, wg=w_gate[0].astype(BF16), wu=w_up[0].astype(BF16),
        cw=conv_w[0], cb=conv_b, wd=w_down[0].astype(BF16), gpo=g_post_ffn)


def _trunk(x, w):
    b, s, _ = x.shape
    n2 = DFT_N2
    n1 = s // n2
    rope_tabs = _rope_tables(s)
    cs, f1, wtab = _dft_tables(s)
    qt, k, vt, fin = _proj(x, w["gpre"], w["win"], w["gq"], w["wuqt"], w["gkv"], w["wuk"],
                           w["wuvt"], rope_tabs, tm=PROJ_TILE)
    a = _attn(qt, k, vt, tq=MXU_TILE, tk=s // SCORE_SLOTS, qsubs=ATTN_STEP_TILES // (s // MXU_TILE))
    y = _dft1(fin.reshape(b, FNET_GROUPS, n1, n2, FNET_GROUP_DIM), cs, f1, r=DFT1_ROWS)
    f = _dft2(y, wtab, w["wf"], kb=DFT2_ROWS)
    f = f.reshape(b, FNET_GROUPS, s, FNET_GROUP_DIM)
    return _tail(a, f, x, w["woa"], w["wof"], w["gpm"], w["gpf"], w["wg"], w["wu"], w["cw"],
                 w["cb"], w["wd"], w["gpo"], tm=TOKEN_TILE, ffc=FF_CHUNK)


def kernel(x_prompt, x_sample, g_pre_mix, w_in, g_q, w_uq, g_kv, w_ukv, w_fnet, w_out,
           g_post_mix, g_pre_ffn, w_gate, w_up, conv_w, conv_b, w_down, g_post_ffn):
    w = _prep_weights(g_pre_mix, w_in, g_q, w_uq, g_kv, w_ukv, w_fnet, w_out, g_post_mix,
                      g_pre_ffn, w_gate, w_up, conv_w, conv_b, w_down, g_post_ffn)
    return _trunk(x_prompt, w), _trunk(x_sample, w)
```

```python
import functools
import math

import jax
import jax.numpy as jnp
import numpy as np
from jax.experimental import pallas as pl
from jax.experimental.pallas import tpu as pltpu

D_MODEL = 1024
MLA_HEADS = 8
QK_NOPE = 64
QK_ROPE = 32
V_HEAD = 64
MLA_WIDTH = MLA_HEADS * V_HEAD
Q_LORA = 256
KV_LORA = 256
FNET_GROUPS = 4
FNET_WIDTH = 512
FNET_GROUP_DIM = 128
D_FF = 2816
ROPE_THETA = 10000.0
RMS_EPS = 1e-6
ATTN_SCALE = 1.0 / math.sqrt(QK_NOPE + QK_ROPE)
Q_SCALE = ATTN_SCALE * math.log2(math.e)

MXUS = 2
HEAD_PAD = 128
HALF_ROPE = QK_ROPE // 2
DENOM_ROWS = 16
MXU_TILE = 256
SCORE_ACC = (0, 64, 128)
VALUE_ACC = (192, 224)
SCORE_AHEAD = 2
SCORE_SLOTS = 4
ATTN_STEP_TILES = 128
PROJ_TILE = 1024
TOKEN_TILE = 512
FF_CHUNK = 256
DFT1_TOKENS = 512
DFT2_ROWS = 8
DFT_N2 = 128
VMEM_LIMIT = 56 * 1024 * 1024

BF16 = jnp.bfloat16
F32 = jnp.float32


def _rms(x, g):
    return x * jax.lax.rsqrt(jnp.mean(x * x, axis=-1, keepdims=True) + RMS_EPS) * g


def _dot(a, b):
    return jnp.dot(a, b, preferred_element_type=F32)


def _params(sem):
    return pltpu.CompilerParams(dimension_semantics=sem, vmem_limit_bytes=VMEM_LIMIT)


def _const_spec(shape):
    zeros = (0,) * len(shape)
    return pl.BlockSpec(shape, lambda *_: zeros)


def _proj_kernel(x_ref, gpre_ref, win_ref, gq_ref, wuqt_ref, gkv_ref, wuk_ref, wuvt_ref,
                 cos_ref, sina_ref, sinb_ref, cost_ref, sint_ref, qt_ref, k_ref, vt_ref, f_ref):
    tm = x_ref.shape[1]
    h = _rms(x_ref[0], gpre_ref[...]).astype(BF16)
    p = _dot(h, win_ref[...])
    cq = _rms(p[:, :Q_LORA], gq_ref[...])
    qt = _dot(wuqt_ref[...], cq.T.astype(BF16))
    ckv = _rms(p[:, Q_LORA:Q_LORA + KV_LORA], gkv_ref[...])
    kn = _dot(ckv.astype(BF16), wuk_ref[...])
    vt_ref[0] = _dot(wuvt_ref[...], ckv.T.astype(BF16)).astype(BF16)
    cos, sina, sinb = cos_ref[...], sina_ref[...], sinb_ref[...]
    t = p[:, 2 * FNET_WIDTH:]
    kr = (t * cos + pltpu.roll(t, HALF_ROPE, 1) * sina
          + pltpu.roll(t, HEAD_PAD - HALF_ROPE, 1) * sinb)
    cos_t, sin_t = cost_ref[...], sint_ref[...]
    lo, mid, hi = QK_NOPE, QK_NOPE + HALF_ROPE, QK_NOPE + QK_ROPE
    for hd in range(MLA_HEADS):
        base = hd * HEAD_PAD
        x1, x2 = qt[base + lo:base + mid], qt[base + mid:base + hi]
        qt_ref[0, base:base + lo, :] = (qt[base:base + lo] * Q_SCALE).astype(BF16)
        qt_ref[0, base + lo:base + mid, :] = ((x1 * cos_t - x2 * sin_t) * Q_SCALE).astype(BF16)
        qt_ref[0, base + mid:base + hi, :] = ((x2 * cos_t + x1 * sin_t) * Q_SCALE).astype(BF16)
        qt_ref[0, base + hi:base + HEAD_PAD, :] = jnp.zeros((HEAD_PAD - hi, tm), BF16)
        sl = slice(base, base + HEAD_PAD)
        k_ref[0, :, sl] = (kn[:, sl] + kr).astype(BF16)
    for g in range(FNET_GROUPS):
        lo = Q_LORA + KV_LORA + g * FNET_GROUP_DIM
        f_ref[0, g] = p[:, lo:lo + FNET_GROUP_DIM]


def _proj(x, gpre, win, gq, wuqt, gkv, wuk, wuvt, rope_tabs, tm):
    b, s, _ = x.shape
    wide = MLA_HEADS * HEAD_PAD
    tok = lambda w: pl.BlockSpec((1, tm, w), lambda bi, i: (bi, i, 0))
    tab = pl.BlockSpec((tm, HEAD_PAD), lambda bi, i: (i, 0))
    tab_t = pl.BlockSpec((HALF_ROPE, tm), lambda bi, i: (0, i))
    return pl.pallas_call(
        _proj_kernel,
        grid=(b, s // tm),
        in_specs=[tok(D_MODEL), _const_spec(gpre.shape), _const_spec(win.shape),
                  _const_spec(gq.shape), _const_spec(wuqt.shape), _const_spec(gkv.shape),
                  _const_spec(wuk.shape), _const_spec(wuvt.shape),
                  tab, tab, tab, tab_t, tab_t],
        out_specs=[pl.BlockSpec((1, wide, tm), lambda bi, i: (bi, 0, i)),
                   tok(wide),
                   pl.BlockSpec((1, MLA_WIDTH, tm), lambda bi, i: (bi, 0, i)),
                   pl.BlockSpec((1, FNET_GROUPS, tm, FNET_GROUP_DIM),
                                lambda bi, i: (bi, 0, i, 0))],
        out_shape=[jax.ShapeDtypeStruct((b, wide, s), BF16),
                   jax.ShapeDtypeStruct((b, s, wide), BF16),
                   jax.ShapeDtypeStruct((b, MLA_WIDTH, s), BF16),
                   jax.ShapeDtypeStruct((b, FNET_GROUPS, s, FNET_GROUP_DIM), F32)],
        compiler_params=_params(("parallel", "parallel")),
        name="proj",
    )(x, gpre, win, gq, wuqt, gkv, wuk, wuvt, *rope_tabs)


def _attn_kernel(qt_ref, k_ref, vt_ref, o_ref, st_all, mc_all, *, tq, tk, heads, qsubs):
    st_ref = tuple(st_all.at[k] for k in range(SCORE_SLOTS))
    mc_ref = tuple(mc_all.at[k] for k in range(SCORE_SLOTS))
    s = k_ref.shape[1]
    n = s // tk
    nt = tk // MXU_TILE
    i = pl.program_id(2)
    ones = jnp.ones((DENOM_ROWS, MXU_TILE), BF16)
    pad_k = jnp.zeros((MXU_TILE, MXU_TILE - HEAD_PAD), BF16)
    pad_q = jnp.zeros((MXU_TILE - HEAD_PAD, tq), BF16)

    def q_rhs(hd, qi):
        q_off = qi * tq if isinstance(qi, int) else pl.multiple_of(qi * tq, tq)
        qt = qt_ref[0, hd * HEAD_PAD:(hd + 1) * HEAD_PAD, pl.ds(q_off, tq)]
        return jnp.concatenate([qt, pad_q], axis=0)

    def k_lhs(hd, c, j):
        k_lo = c * tk + j * MXU_TILE
        kc = k_ref[0, k_lo:k_lo + MXU_TILE, hd * HEAD_PAD:(hd + 1) * HEAD_PAD]
        return jnp.concatenate([kc, pad_k], axis=1)

    def pop_scores(hd, j, slot, mx):
        st = pltpu.matmul_pop(SCORE_ACC[j % len(SCORE_ACC)], (MXU_TILE, tq), F32, hd)
        st_ref[slot][hd, j * MXU_TILE:(j + 1) * MXU_TILE, :] = st
        return jnp.maximum(mx, jnp.max(st, axis=0, keepdims=True))

    lag = len(SCORE_ACC) - 1

    @pl.when(i == 0)
    def _():
        for c in range(SCORE_AHEAD):
            mx = [jnp.full((1, tq), -jnp.inf, F32) for _ in range(heads)]
            for hd in range(heads):
                pltpu.matmul_push_rhs(q_rhs(hd, 0), 0, hd)
            for j in range(nt + lag):
                for hd in range(heads):
                    if j < nt:
                        pltpu.matmul_acc_lhs(SCORE_ACC[j % len(SCORE_ACC)], k_lhs(hd, c, j), hd,
                                             0 if j == 0 else None)
                    if j >= lag:
                        mx[hd] = pop_scores(hd, j - lag, c, mx[hd])
            for hd in range(heads):
                mc_ref[c][hd] = mx[hd]

    pending = []
    tile = 0
    block_accs = []

    def drain(keep):
        while len(pending) > keep:
            pending.pop(0)()

    for qsub in range(qsubs):
        qb = i * qsubs + qsub
        ms = [jnp.full((1, tq), -jnp.inf, F32) for _ in range(heads)]
        accs = [jnp.zeros((V_HEAD + DENOM_ROWS, tq), F32) for _ in range(heads)]
        block_accs.append(accs)
        for slot in range(n):
            ahead = slot + SCORE_AHEAD
            if ahead < n:
                qi, c_next = qb, ahead
            elif qsub + 1 < qsubs:
                qi, c_next = qb + 1, ahead - n
            else:
                qi, c_next = jnp.minimum(qb + 1, s // tq - 1), ahead - n
            fill = ahead % SCORE_SLOTS
            value_acc = VALUE_ACC[slot % len(VALUE_ACC)]
            m_old = list(ms)
            ms = [jnp.maximum(ms[hd], mc_ref[slot][hd]) for hd in range(heads)]
            mx = [jnp.full((1, tq), -jnp.inf, F32) for _ in range(heads)]
            for j in range(nt):
                score_acc = SCORE_ACC[tile % len(SCORE_ACC)]
                tile += 1
                for hd in range(heads):
                    rows = slice(j * MXU_TILE, (j + 1) * MXU_TILE)
                    p = jnp.exp2(st_ref[slot][hd, rows, :] - ms[hd]).astype(BF16)
                    pltpu.matmul_push_rhs(p, 1, hd)
                    pltpu.matmul_push_rhs(q_rhs(hd, qi), 0, hd)
                    v_lo = slot * tk + j * MXU_TILE
                    vc = vt_ref[0, hd * V_HEAD:(hd + 1) * V_HEAD, v_lo:v_lo + MXU_TILE]
                    pltpu.matmul_acc_lhs(value_acc, jnp.concatenate([vc, ones], axis=0), hd, 1)
                    pltpu.matmul_acc_lhs(score_acc, k_lhs(hd, c_next, j), hd, 0)

                def pop_tile(j=j, fill=fill, mx=mx, score_acc=score_acc, value_acc=value_acc,
                             m_old=m_old, m_new=list(ms), accs=accs):
                    for hd in range(heads):
                        st = pltpu.matmul_pop(score_acc, (MXU_TILE, tq), F32, hd)
                        st_ref[fill][hd, j * MXU_TILE:(j + 1) * MXU_TILE, :] = st
                        mx[hd] = jnp.maximum(mx[hd], jnp.max(st, axis=0, keepdims=True))
                        if j == nt - 1:
                            mc_ref[fill][hd] = mx[hd]
                            pv = pltpu.matmul_pop(value_acc, (V_HEAD + DENOM_ROWS, tq), F32, hd)
                            accs[hd] = jnp.exp2(m_old[hd] - m_new[hd]) * accs[hd] + pv

                drain(lag - 1)
                pending.append(pop_tile)
    drain(0)
    for qsub, accs in enumerate(block_accs):
        outs = [acc[:V_HEAD] / acc[V_HEAD:V_HEAD + 1] for acc in accs]
        o_ref[0, qsub * tq:(qsub + 1) * tq, :] = jnp.concatenate(outs, axis=0).T.astype(BF16)


def _attn(qt, k, vt, tq, tk, qsubs):
    b, _, s = qt.shape
    heads = MXUS
    groups = MLA_HEADS // heads
    assert s % (qsubs * tq) == 0 and s == SCORE_SLOTS * tk and tq == MXU_TILE and tk % MXU_TILE == 0
    return pl.pallas_call(
        functools.partial(_attn_kernel, tq=tq, tk=tk, heads=heads, qsubs=qsubs),
        grid=(b, groups, s // (qsubs * tq)),
        in_specs=[pl.BlockSpec((1, heads * HEAD_PAD, s), lambda bi, j, i: (bi, j, 0)),
                  pl.BlockSpec((1, s, heads * HEAD_PAD), lambda bi, j, i: (bi, 0, j)),
                  pl.BlockSpec((1, heads * V_HEAD, s), lambda bi, j, i: (bi, j, 0))],
        out_specs=pl.BlockSpec((1, qsubs * tq, heads * V_HEAD), lambda bi, j, i: (bi, i, j)),
        out_shape=jax.ShapeDtypeStruct((b, s, MLA_WIDTH), BF16),
        scratch_shapes=[pltpu.VMEM((SCORE_SLOTS, heads, tk, tq), F32),
                        pltpu.VMEM((SCORE_SLOTS, heads, 1, tq), F32)],
        compiler_params=_params(("parallel", "parallel", "arbitrary")),
        name="attn",
    )(qt, k, vt)


def _dft1_kernel(x_ref, cs_ref, f1_ref, y_ref):
    groups, n1, r, c = x_ref.shape[1:]
    f1 = f1_ref[...].astype(BF16)
    rows = [pl.ds(j, n1, stride=r) for j in range(r)]
    flat_x = lambda g: x_ref.at[0, g].reshape(n1 * r, c)
    flat_y = lambda part, g: y_ref.at[0, part, g].reshape(n1 * r, c)
    x = jnp.concatenate([flat_x(g)[rows[j], :] for j in range(r) for g in range(groups)], axis=0)
    z = _dot(x.astype(BF16), cs_ref[...].astype(BF16)).astype(BF16)
    for j in range(r):
        blk = [z[(j * groups + g) * n1:(j * groups + g + 1) * n1] for g in range(groups)]
        zr = jnp.concatenate([t[:, :c] for t in blk], axis=1)
        zi = jnp.concatenate([t[:, c:] for t in blk], axis=1)
        y = _dot(f1[:, :n1], zr) + _dot(f1[:, n1:], zi)
        for g in range(groups):
            flat_y(0, g)[rows[j], :] = y[:n1, g * c:(g + 1) * c]
            flat_y(1, g)[rows[j], :] = y[n1:, g * c:(g + 1) * c]


def _dft1(x, cs, f1, r):
    b, groups, n1, n2, c = x.shape
    return pl.pallas_call(
        _dft1_kernel,
        grid=(b, n2 // r),
        in_specs=[pl.BlockSpec((1, groups, n1, r, c), lambda bi, j: (bi, 0, 0, j, 0)),
                  _const_spec(cs.shape), _const_spec(f1.shape)],
        out_specs=pl.BlockSpec((1, 2, groups, n1, r, c), lambda bi, j: (bi, 0, 0, 0, j, 0)),
        out_shape=jax.ShapeDtypeStruct((b, 2, groups, n1, n2, c), F32),
        compiler_params=_params(("parallel", "parallel")),
        name="dft1",
    )(x, cs, f1)


def _dft2_kernel(y_ref, w_ref, wf_ref, o_ref):
    groups, kb, n2, c = y_ref.shape[2:]
    for t in range(kb):
        w = w_ref[t].astype(BF16)
        part = lambda p: jnp.concatenate(
            [y_ref[0, p, g, t] for g in range(groups)], axis=1).astype(BF16)
        xr = (_dot(w[:, :n2], part(0)) + _dot(w[:, n2:], part(1))).astype(BF16)
        for g in range(groups):
            rows = o_ref.at[0, g].reshape(n2 * kb, c)
            rows[pl.ds(t, n2, stride=kb), :] = _dot(xr[:, g * c:(g + 1) * c], wf_ref[g])


def _dft2(y, w, wf, kb):
    b, _, groups, n1, n2, c = y.shape
    return pl.pallas_call(
        _dft2_kernel,
        grid=(n1 // kb, b),
        in_specs=[pl.BlockSpec((1, 2, groups, kb, n2, c), lambda j, bi: (bi, 0, 0, j, 0, 0)),
                  pl.BlockSpec((kb, n2, 2 * n2), lambda j, bi: (j, 0, 0)),
                  _const_spec(wf.shape)],
        out_specs=pl.BlockSpec((1, groups, n2, kb, c), lambda j, bi: (bi, 0, 0, j, 0)),
        out_shape=jax.ShapeDtypeStruct((b, groups, n2, n1, c), F32),
        compiler_params=_params(("parallel", "parallel")),
        name="dft2",
    )(y, w, wf)


HALO = 16


def _tail_kernel(a_ref, ap_ref, an_ref, f_ref, fp_ref, fn_ref, x_ref, xp_ref, xn_ref,
                 woa_ref, wof_ref, gpm_ref, gpf_ref, wg_ref, wu_ref, cw_ref, cb_ref, wd_ref,
                 gpo_ref, y_ref, act_ref, *, ffc):
    i = pl.program_id(1)
    tm = x_ref.shape[1]
    rows = tm + 2 * HALO
    ext = lambda p, c, n: jnp.concatenate([p[0], c[0], n[0]], axis=0)
    f_ext = jnp.concatenate(
        [jnp.concatenate([fp_ref[0, g], f_ref[0, g], fn_ref[0, g]], axis=0)
         for g in range(FNET_GROUPS)], axis=1).astype(BF16)
    mix = _dot(ext(ap_ref, a_ref, an_ref), woa_ref[...]) + _dot(f_ext, wof_ref[...])
    x1 = ext(xp_ref, x_ref, xn_ref) + _rms(mix, gpm_ref[...])
    row = jax.lax.broadcasted_iota(jnp.int32, (rows, 1), 0)
    inside = ((row >= HALO) | (i > 0)) & ((row < HALO + tm) | (i < pl.num_programs(1) - 1))
    hext = jnp.where(inside, _rms(x1, gpf_ref[...]), 0.0).astype(BF16)
    h = hext[HALO:HALO + tm]
    for c in range(D_FF // ffc):
        sl = slice(c * ffc, (c + 1) * ffc)
        g = _dot(hext, wg_ref[:, sl])
        u = _dot(h, wu_ref[:, sl])
        g_prev = pltpu.roll(g, 1, 0)[HALO:HALO + tm]
        g_next = pltpu.roll(g, rows - 1, 0)[HALO:HALO + tm]
        gate = (cb_ref[:, sl] + g_prev * cw_ref[0:1, sl] + g[HALO:HALO + tm] * cw_ref[1:2, sl]
                + g_next * cw_ref[2:3, sl])
        inner = math.sqrt(2.0 / math.pi) * (gate + 0.044715 * (gate * gate * gate))
        act = 0.5 * gate * (1.0 + jnp.tanh(inner)) * u
        act_ref[:, sl] = act.astype(BF16)
    out = _dot(act_ref[...], wd_ref[...])
    y_ref[0] = x1[HALO:HALO + tm] + _rms(out, gpo_ref[...])


def _tail(a, f, x, woa, wof, gpm, gpf, wg, wu, cw, cb, wd, gpo, tm, ffc):
    b, s, _ = x.shape
    per = tm // HALO
    last = s // HALO - 1
    single = lambda shape: pl.BlockSpec(shape, lambda *_: (0,) * len(shape),
                                        pipeline_mode=pl.Buffered(1))

    def with_halo(width):
        return [pl.BlockSpec((1, tm, width), lambda bi, i: (bi, i, 0)),
                pl.BlockSpec((1, HALO, width), lambda bi, i: (bi, jnp.maximum(i * per - 1, 0), 0)),
                pl.BlockSpec((1, HALO, width), lambda bi, i: (bi, jnp.minimum((i + 1) * per, last), 0))]

    grouped = lambda rows, row_block: pl.BlockSpec(
        (1, FNET_GROUPS, rows, FNET_GROUP_DIM), lambda bi, i: (bi, 0, row_block(i), 0))
    f_specs = [grouped(tm, lambda i: i),
               grouped(HALO, lambda i: jnp.maximum(i * per - 1, 0)),
               grouped(HALO, lambda i: jnp.minimum((i + 1) * per, last))]
    return pl.pallas_call(
        functools.partial(_tail_kernel, ffc=ffc),
        grid=(b, s // tm),
        in_specs=with_halo(MLA_WIDTH) + f_specs + with_halo(D_MODEL) + [
            single(woa.shape), single(wof.shape), _const_spec(gpm.shape), _const_spec(gpf.shape),
            single(wg.shape), single(wu.shape), _const_spec(cw.shape), _const_spec(cb.shape),
            single(wd.shape), _const_spec(gpo.shape)],
        out_specs=pl.BlockSpec((1, tm, D_MODEL), lambda bi, i: (bi, i, 0)),
        out_shape=jax.ShapeDtypeStruct(x.shape, F32),
        scratch_shapes=[pltpu.VMEM((tm, D_FF), BF16)],
        compiler_params=_params(("parallel", "arbitrary")),
        name="tail",
    )(a, a, a, f, f, f, x, x, x, woa, wof, gpm, gpf, wg, wu, cw, cb, wd, gpo)


def _rope_tables(s):
    ang = np.arange(s, dtype=np.float64)[:, None] * (
        ROPE_THETA ** (-np.arange(0, QK_ROPE, 2, dtype=np.float64) / QK_ROPE))[None, :]
    cos, sin = np.cos(ang), np.sin(ang)
    zeros = np.zeros((s, QK_NOPE))
    zh = np.zeros((s, HALF_ROPE))
    tail = np.zeros((s, HEAD_PAD - QK_NOPE - QK_ROPE))
    cos_t = np.concatenate([np.ones((s, QK_NOPE)), cos, cos, tail], axis=1)
    sina = np.concatenate([zeros, zh, sin, tail], axis=1)
    sinb = np.concatenate([zeros, -sin, zh, tail], axis=1)
    return tuple(jnp.asarray(t, F32) for t in (cos_t, sina, sinb, cos.T, sin.T))


def _angles(num, den):
    return (2.0 * np.pi / den) * (num % den).astype(np.float64)


def _dft_tables(s):
    n2 = DFT_N2
    n1 = s // n2
    c = np.arange(FNET_GROUP_DIM, dtype=np.int64)
    ang = _angles(c[:, None] * c[None, :], FNET_GROUP_DIM)
    cs = np.concatenate([np.cos(ang), np.sin(ang)], axis=1) / math.sqrt(FNET_GROUP_DIM)
    k1 = np.arange(n1, dtype=np.int64)
    a1 = _angles(k1[:, None] * k1[None, :], n1)
    c1, s1 = np.cos(a1), np.sin(a1)
    f1 = np.concatenate([np.concatenate([c1, -s1], axis=1),
                         np.concatenate([s1, c1], axis=1)], axis=0) / math.sqrt(n1)
    k2 = np.arange(n2, dtype=np.int64)
    kk = k1[:, None, None] + n1 * k2[None, :, None]
    a2 = _angles(kk * k2[None, None, :], s)
    w = np.concatenate([np.cos(a2), -np.sin(a2)], axis=2) / math.sqrt(n2)
    return tuple(jnp.asarray(t, F32) for t in (cs, f1, w))


def _prep_weights(g_pre_mix, w_in, g_q, w_uq, g_kv, w_ukv, w_fnet, w_out, g_post_mix,
                  g_pre_ffn, w_gate, w_up, conv_w, conv_b, w_down, g_post_ffn):
    win = w_in[0]
    f_lo = Q_LORA + KV_LORA + QK_ROPE
    kr_cols = jnp.concatenate([jnp.zeros((D_MODEL, QK_NOPE), F32),
                               win[:, Q_LORA + KV_LORA:f_lo],
                               jnp.zeros((D_MODEL, HEAD_PAD - QK_NOPE - QK_ROPE), F32)], axis=1)
    win_p = jnp.concatenate([win[:, :Q_LORA + KV_LORA], win[:, f_lo:], kr_cols], axis=1)
    wuq = w_uq[0].reshape(Q_LORA, MLA_HEADS, QK_NOPE + QK_ROPE)
    wuq = jnp.pad(wuq, ((0, 0), (0, 0), (0, HEAD_PAD - QK_NOPE - QK_ROPE)))
    wuq = wuq.reshape(Q_LORA, MLA_HEADS * HEAD_PAD)
    wukv = w_ukv[0].reshape(KV_LORA, MLA_HEADS, QK_NOPE + V_HEAD)
    wuk = jnp.pad(wukv[..., :QK_NOPE], ((0, 0), (0, 0), (0, HEAD_PAD - QK_NOPE)))
    wuk = wuk.reshape(KV_LORA, MLA_HEADS * HEAD_PAD)
    wuv = wukv[..., QK_NOPE:].reshape(KV_LORA, MLA_WIDTH)
    return dict(
        gpre=g_pre_mix, win=win_p.astype(BF16), gq=g_q, wuqt=wuq.T.astype(BF16), gkv=g_kv,
        wuk=wuk.astype(BF16), wuvt=wuv.T.astype(BF16), wf=w_fnet[0].astype(BF16),
        woa=w_out[0, :MLA_WIDTH].astype(BF16), wof=w_out[0, MLA_WIDTH:].astype(BF16),
        gpm=g_post_mix, gpf=g_pre_ffn, wg=w_gate[0].astype(BF16), wu=w_up[0].astype(BF16),
        cw=conv_w[0], cb=conv_b, wd=w_down[0].astype(BF16), gpo=g_post_ffn)


def _trunk(x, w):
    b, s, _ = x.shape
    n2 = DFT_N2
    n1 = s // n2
    rope_tabs = _rope_tables(s)
    cs, f1, wtab = _dft_tables(s)
    qt, k, vt, fin = _proj(x, w["gpre"], w["win"], w["gq"], w["wuqt"], w["gkv"], w["wuk"],
                           w["wuvt"], rope_tabs, tm=PROJ_TILE)
    a = _attn(qt, k, vt, tq=MXU_TILE, tk=s // SCORE_SLOTS, qsubs=ATTN_STEP_TILES // (s // MXU_TILE))
    y = _dft1(fin.reshape(b, FNET_GROUPS, n1, n2, FNET_GROUP_DIM), cs, f1, r=DFT1_TOKENS // n1)
    f = _dft2(y, wtab, w["wf"], kb=DFT2_ROWS)
    f = f.reshape(b, FNET_GROUPS, s, FNET_GROUP_DIM)
    return _tail(a, f, x, w["woa"], w["wof"], w["gpm"], w["gpf"], w["wg"], w["wu"], w["cw"],
                 w["cb"], w["wd"], w["gpo"], tm=TOKEN_TILE, ffc=FF_CHUNK)


def kernel(x_prompt, x_sample, g_pre_mix, w_in, g_q, w_uq, g_kv, w_ukv, w_fnet, w_out,
           g_post_mix, g_pre_ffn, w_gate, w_up, conv_w, conv_b, w_down, g_post_ffn):
    w = _prep_weights(g_pre_mix, w_in, g_q, w_uq, g_kv, w_ukv, w_fnet, w_out, g_post_mix,
                      g_pre_ffn, w_gate, w_up, conv_w, conv_b, w_down, g_post_ffn)
    return _trunk(x_prompt, w), _trunk(x_sample, w)
```

```python
import functools
import math

import jax
import jax.numpy as jnp
import numpy as np
from jax.experimental import pallas as pl
from jax.experimental.pallas import tpu as pltpu

D_MODEL = 1024
MLA_HEADS = 8
QK_NOPE = 64
QK_ROPE = 32
V_HEAD = 64
MLA_WIDTH = MLA_HEADS * V_HEAD
Q_LORA = 256
KV_LORA = 256
FNET_GROUPS = 4
FNET_WIDTH = 512
FNET_GROUP_DIM = 128
D_FF = 2816
ROPE_THETA = 10000.0
RMS_EPS = 1e-6
ATTN_SCALE = 1.0 / math.sqrt(QK_NOPE + QK_ROPE)
Q_SCALE = ATTN_SCALE * math.log2(math.e)

MXUS = 2
HEAD_PAD = 128
HALF_ROPE = QK_ROPE // 2
DENOM_ROWS = 16
MXU_TILE = 256
SCORE_ACC = (0, 64, 128)
VALUE_ACC = (192, 224)
SCORE_AHEAD = 2
SCORE_SLOTS = 4
ATTN_STEP_TILES = 128
PROJ_TILE = 1024
TOKEN_TILE = 512
FF_CHUNK = 256
DFT1_TOKENS = 512
DFT2_ROWS = 8
DFT_N2 = 128
VMEM_LIMIT = 56 * 1024 * 1024

BF16 = jnp.bfloat16
F32 = jnp.float32


def _rms(x, g):
    return x * jax.lax.rsqrt(jnp.mean(x * x, axis=-1, keepdims=True) + RMS_EPS) * g


def _dot(a, b):
    return jnp.dot(a, b, preferred_element_type=F32)


def _params(sem):
    return pltpu.CompilerParams(dimension_semantics=sem, vmem_limit_bytes=VMEM_LIMIT)


def _const_spec(shape):
    zeros = (0,) * len(shape)
    return pl.BlockSpec(shape, lambda *_: zeros)


def _proj_kernel(x_ref, gpre_ref, win_ref, gq_ref, wuqt_ref, gkv_ref, wuk_ref, wuvt_ref,
                 cos_ref, sina_ref, sinb_ref, cost_ref, sint_ref, qt_ref, k_ref, vt_ref, f_ref):
    tm = x_ref.shape[1]
    h = _rms(x_ref[0], gpre_ref[...]).astype(BF16)
    p = _dot(h, win_ref[...])
    cq = _rms(p[:, :Q_LORA], gq_ref[...])
    qt = _dot(wuqt_ref[...], cq.T.astype(BF16))
    ckv = _rms(p[:, Q_LORA:Q_LORA + KV_LORA], gkv_ref[...])
    kn = _dot(ckv.astype(BF16), wuk_ref[...])
    vt_ref[0] = _dot(wuvt_ref[...], ckv.T.astype(BF16)).astype(BF16)
    cos, sina, sinb = cos_ref[...], sina_ref[...], sinb_ref[...]
    t = p[:, 2 * FNET_WIDTH:]
    kr = (t * cos + pltpu.roll(t, HALF_ROPE, 1) * sina
          + pltpu.roll(t, HEAD_PAD - HALF_ROPE, 1) * sinb)
    cos_t, sin_t = cost_ref[...], sint_ref[...]
    lo, mid, hi = QK_NOPE, QK_NOPE + HALF_ROPE, QK_NOPE + QK_ROPE
    for hd in range(MLA_HEADS):
        base = hd * HEAD_PAD
        x1, x2 = qt[base + lo:base + mid], qt[base + mid:base + hi]
        qt_ref[0, base:base + lo, :] = (qt[base:base + lo] * Q_SCALE).astype(BF16)
        qt_ref[0, base + lo:base + mid, :] = ((x1 * cos_t - x2 * sin_t) * Q_SCALE).astype(BF16)
        qt_ref[0, base + mid:base + hi, :] = ((x2 * cos_t + x1 * sin_t) * Q_SCALE).astype(BF16)
        qt_ref[0, base + hi:base + HEAD_PAD, :] = jnp.zeros((HEAD_PAD - hi, tm), BF16)
        sl = slice(base, base + HEAD_PAD)
        k_ref[0, :, sl] = (kn[:, sl] + kr).astype(BF16)
    for g in range(FNET_GROUPS):
        lo = Q_LORA + KV_LORA + g * FNET_GROUP_DIM
        f_ref[0, g] = p[:, lo:lo + FNET_GROUP_DIM]


def _proj(x, gpre, win, gq, wuqt, gkv, wuk, wuvt, rope_tabs, tm):
    b, s, _ = x.shape
    wide = MLA_HEADS * HEAD_PAD
    tok = lambda w: pl.BlockSpec((1, tm, w), lambda bi, i: (bi, i, 0))
    tab = pl.BlockSpec((tm, HEAD_PAD), lambda bi, i: (i, 0))
    tab_t = pl.BlockSpec((HALF_ROPE, tm), lambda bi, i: (0, i))
    return pl.pallas_call(
        _proj_kernel,
        grid=(b, s // tm),
        in_specs=[tok(D_MODEL), _const_spec(gpre.shape), _const_spec(win.shape),
                  _const_spec(gq.shape), _const_spec(wuqt.shape), _const_spec(gkv.shape),
                  _const_spec(wuk.shape), _const_spec(wuvt.shape),
                  tab, tab, tab, tab_t, tab_t],
        out_specs=[pl.BlockSpec((1, wide, tm), lambda bi, i: (bi, 0, i)),
                   tok(wide),
                   pl.BlockSpec((1, MLA_WIDTH, tm), lambda bi, i: (bi, 0, i)),
                   pl.BlockSpec((1, FNET_GROUPS, tm, FNET_GROUP_DIM),
                                lambda bi, i: (bi, 0, i, 0))],
        out_shape=[jax.ShapeDtypeStruct((b, wide, s), BF16),
                   jax.ShapeDtypeStruct((b, s, wide), BF16),
                   jax.ShapeDtypeStruct((b, MLA_WIDTH, s), BF16),
                   jax.ShapeDtypeStruct((b, FNET_GROUPS, s, FNET_GROUP_DIM), F32)],
        compiler_params=_params(("parallel", "parallel")),
        name="proj",
    )(x, gpre, win, gq, wuqt, gkv, wuk, wuvt, *rope_tabs)


def _attn_kernel(qt_ref, k_ref, vt_ref, o_ref, st_all, mc_all, *, tq, tk, heads, qsubs):
    st_ref = tuple(st_all.at[k] for k in range(SCORE_SLOTS))
    mc_ref = tuple(mc_all.at[k] for k in range(SCORE_SLOTS))
    s = k_ref.shape[1]
    n = s // tk
    nt = tk // MXU_TILE
    i = pl.program_id(2)
    ones = jnp.ones((DENOM_ROWS, MXU_TILE), BF16)
    pad_k = jnp.zeros((MXU_TILE, MXU_TILE - HEAD_PAD), BF16)
    pad_q = jnp.zeros((MXU_TILE - HEAD_PAD, tq), BF16)

    def q_rhs(hd, qi):
        q_off = qi * tq if isinstance(qi, int) else pl.multiple_of(qi * tq, tq)
        qt = qt_ref[0, hd * HEAD_PAD:(hd + 1) * HEAD_PAD, pl.ds(q_off, tq)]
        return jnp.concatenate([qt, pad_q], axis=0)

    def k_lhs(hd, c, j):
        k_lo = c * tk + j * MXU_TILE
        kc = k_ref[0, k_lo:k_lo + MXU_TILE, hd * HEAD_PAD:(hd + 1) * HEAD_PAD]
        return jnp.concatenate([kc, pad_k], axis=1)

    def pop_scores(hd, j, slot, mx):
        st = pltpu.matmul_pop(SCORE_ACC[j % len(SCORE_ACC)], (MXU_TILE, tq), F32, hd)
        st_ref[slot][hd, j * MXU_TILE:(j + 1) * MXU_TILE, :] = st
        return jnp.maximum(mx, jnp.max(st, axis=0, keepdims=True))

    lag = len(SCORE_ACC) - 1

    @pl.when(i == 0)
    def _():
        for c in range(SCORE_AHEAD):
            mx = [jnp.full((1, tq), -jnp.inf, F32) for _ in range(heads)]
            for hd in range(heads):
                pltpu.matmul_push_rhs(q_rhs(hd, 0), 0, hd)
            for j in range(nt + lag):
                for hd in range(heads):
                    if j < nt:
                        pltpu.matmul_acc_lhs(SCORE_ACC[j % len(SCORE_ACC)], k_lhs(hd, c, j), hd,
                                             0 if j == 0 else None)
                    if j >= lag:
                        mx[hd] = pop_scores(hd, j - lag, c, mx[hd])
            for hd in range(heads):
                mc_ref[c][hd] = mx[hd]

    pending = []
    tile = 0
    block_accs = []

    def drain(keep):
        while len(pending) > keep:
            pending.pop(0)()

    for qsub in range(qsubs):
        qb = i * qsubs + qsub
        ms = [jnp.full((1, tq), -jnp.inf, F32) for _ in range(heads)]
        accs = [jnp.zeros((V_HEAD + DENOM_ROWS, tq), F32) for _ in range(heads)]
        block_accs.append(accs)
        for slot in range(n):
            ahead = slot + SCORE_AHEAD
            if ahead < n:
                qi, c_next = qb, ahead
            elif qsub + 1 < qsubs:
                qi, c_next = qb + 1, ahead - n
            else:
                qi, c_next = jnp.minimum(qb + 1, s // tq - 1), ahead - n
            fill = ahead % SCORE_SLOTS
            value_acc = VALUE_ACC[slot % len(VALUE_ACC)]
            m_old = list(ms)
            ms = [jnp.maximum(ms[hd], mc_ref[slot][hd]) for hd in range(heads)]
            mx = [jnp.full((1, tq), -jnp.inf, F32) for _ in range(heads)]
            for j in range(nt):
                score_acc = SCORE_ACC[tile % len(SCORE_ACC)]
                tile += 1
                for hd in range(heads):
                    rows = slice(j * MXU_TILE, (j + 1) * MXU_TILE)
                    p = jnp.exp2(st_ref[slot][hd, rows, :] - ms[hd]).astype(BF16)
                    pltpu.matmul_push_rhs(p, 1, hd)
                    pltpu.matmul_push_rhs(q_rhs(hd, qi), 0, hd)
                    v_lo = slot * tk + j * MXU_TILE
                    vc = vt_ref[0, hd * V_HEAD:(hd + 1) * V_HEAD, v_lo:v_lo + MXU_TILE]
                    pltpu.matmul_acc_lhs(value_acc, jnp.concatenate([vc, ones], axis=0), hd, 1)
                    pltpu.matmul_acc_lhs(score_acc, k_lhs(hd, c_next, j), hd, 0)

                def pop_tile(j=j, fill=fill, mx=mx, score_acc=score_acc, value_acc=value_acc,
                             m_old=m_old, m_new=list(ms), accs=accs):
                    for hd in range(heads):
                        st = pltpu.matmul_pop(score_acc, (MXU_TILE, tq), F32, hd)
                        st_ref[fill][hd, j * MXU_TILE:(j + 1) * MXU_TILE, :] = st
                        mx[hd] = jnp.maximum(mx[hd], jnp.max(st, axis=0, keepdims=True))
                        if j == nt - 1:
                            mc_ref[fill][hd] = mx[hd]
                            pv = pltpu.matmul_pop(value_acc, (V_HEAD + DENOM_ROWS, tq), F32, hd)
                            accs[hd] = jnp.exp2(m_old[hd] - m_new[hd]) * accs[hd] + pv

                drain(lag - 1)
                pending.append(pop_tile)
    drain(0)
    for qsub, accs in enumerate(block_accs):
        outs = [acc[:V_HEAD] / acc[V_HEAD:V_HEAD + 1] for acc in accs]
        o_ref[0, qsub * tq:(qsub + 1) * tq, :] = jnp.concatenate(outs, axis=0).T.astype(BF16)


def _attn(qt, k, vt, tq, tk, qsubs):
    b, _, s = qt.shape
    heads = MXUS
    groups = MLA_HEADS // heads
    assert s % (qsubs * tq) == 0 and s == SCORE_SLOTS * tk and tq == MXU_TILE and tk % MXU_TILE == 0
    return pl.pallas_call(
        functools.partial(_attn_kernel, tq=tq, tk=tk, heads=heads, qsubs=qsubs),
        grid=(b, groups, s // (qsubs * tq)),
        in_specs=[pl.BlockSpec((1, heads * HEAD_PAD, s), lambda bi, j, i: (bi, j, 0)),
                  pl.BlockSpec((1, s, heads * HEAD_PAD), lambda bi, j, i: (bi, 0, j)),
                  pl.BlockSpec((1, heads * V_HEAD, s), lambda bi, j, i: (bi, j, 0))],
        out_specs=pl.BlockSpec((1, qsubs * tq, heads * V_HEAD), lambda bi, j, i: (bi, i, j)),
        out_shape=jax.ShapeDtypeStruct((b, s, MLA_WIDTH), BF16),
        scratch_shapes=[pltpu.VMEM((SCORE_SLOTS, heads, tk, tq), F32),
                        pltpu.VMEM((SCORE_SLOTS, heads, 1, tq), F32)],
        compiler_params=_params(("parallel", "parallel", "arbitrary")),
        name="attn",
    )(qt, k, vt)


def _dft1_kernel(x_ref, cs_ref, f1_ref, y_ref):
    groups, n1, r, c = x_ref.shape[1:]
    f1 = f1_ref[...].astype(BF16)
    rows = [pl.ds(j, n1, stride=r) for j in range(r)]
    flat_x = lambda g: x_ref.at[0, g].reshape(n1 * r, c)
    flat_y = lambda part, g: y_ref.at[0, part, g].reshape(n1 * r, c)
    x = jnp.concatenate([flat_x(g)[rows[j], :] for j in range(r) for g in range(groups)], axis=0)
    z = _dot(x.astype(BF16), cs_ref[...].astype(BF16)).astype(BF16)
    for j in range(r):
        blk = [z[(j * groups + g) * n1:(j * groups + g + 1) * n1] for g in range(groups)]
        zr = jnp.concatenate([t[:, :c] for t in blk], axis=1)
        zi = jnp.concatenate([t[:, c:] for t in blk], axis=1)
        y = _dot(f1[:, :n1], zr) + _dot(f1[:, n1:], zi)
        for g in range(groups):
            flat_y(0, g)[rows[j], :] = y[:n1, g * c:(g + 1) * c]
            flat_y(1, g)[rows[j], :] = y[n1:, g * c:(g + 1) * c]


def _dft1(x, cs, f1, r):
    b, groups, n1, n2, c = x.shape
    return pl.pallas_call(
        _dft1_kernel,
        grid=(b, n2 // r),
        in_specs=[pl.BlockSpec((1, groups, n1, r, c), lambda bi, j: (bi, 0, 0, j, 0)),
                  _const_spec(cs.shape), _const_spec(f1.shape)],
        out_specs=pl.BlockSpec((1, 2, groups, n1, r, c), lambda bi, j: (bi, 0, 0, 0, j, 0)),
        out_shape=jax.ShapeDtypeStruct((b, 2, groups, n1, n2, c), F32),
        compiler_params=_params(("parallel", "parallel")),
        name="dft1",
    )(x, cs, f1)


def _dft2_kernel(y_ref, w_ref, wf_ref, o_ref):
    groups, kb, n2, c = y_ref.shape[2:]
    for t in range(kb):
        w = w_ref[t].astype(BF16)
        part = lambda p: jnp.concatenate(
            [y_ref[0, p, g, t] for g in range(groups)], axis=1).astype(BF16)
        xr = (_dot(w[:, :n2], part(0)) + _dot(w[:, n2:], part(1))).astype(BF16)
        for g in range(groups):
            rows = o_ref.at[0, g].reshape(n2 * kb, c)
            rows[pl.ds(t, n2, stride=kb), :] = _dot(xr[:, g * c:(g + 1) * c], wf_ref[g])


def _dft2(y, w, wf, kb):
    b, _, groups, n1, n2, c = y.shape
    pipeline = pltpu.emit_pipeline(
        _dft2_kernel,
        grid=(n1 // kb, b),
        in_specs=[pl.BlockSpec((1, 2, groups, kb, n2, c), lambda j, bi: (bi, 0, 0, j, 0, 0),
                               pipeline_mode=pl.Buffered(3)),
                  pl.BlockSpec((kb, n2, 2 * n2), lambda j, bi: (j, 0, 0)),
                  pl.BlockSpec(wf.shape, lambda j, bi: (0, 0, 0))],
        out_specs=[pl.BlockSpec((1, groups, n2, kb, c), lambda j, bi: (bi, 0, 0, j, 0))])
    hbm = pl.BlockSpec(memory_space=pl.ANY)
    return pl.pallas_call(
        lambda y_hbm, w_hbm, wf_hbm, o_hbm: pipeline(y_hbm, w_hbm, wf_hbm, o_hbm),
        in_specs=[hbm, hbm, hbm],
        out_specs=hbm,
        out_shape=jax.ShapeDtypeStruct((b, groups, n2, n1, c), F32),
        compiler_params=pltpu.CompilerParams(vmem_limit_bytes=VMEM_LIMIT),
        name="dft2",
    )(y, w, wf)


HALO = 16


def _tail_kernel(a_ref, ap_ref, an_ref, f_ref, fp_ref, fn_ref, x_ref, xp_ref, xn_ref,
                 woa_ref, wof_ref, gpm_ref, gpf_ref, wg_ref, wu_ref, cw_ref, cb_ref, wd_ref,
                 gpo_ref, y_ref, act_ref, *, ffc):
    i = pl.program_id(1)
    tm = x_ref.shape[1]
    rows = tm + 2 * HALO
    ext = lambda p, c, n: jnp.concatenate([p[0], c[0], n[0]], axis=0)
    f_ext = jnp.concatenate(
        [jnp.concatenate([fp_ref[0, g], f_ref[0, g], fn_ref[0, g]], axis=0)
         for g in range(FNET_GROUPS)], axis=1).astype(BF16)
    mix = _dot(ext(ap_ref, a_ref, an_ref), woa_ref[...]) + _dot(f_ext, wof_ref[...])
    x1 = ext(xp_ref, x_ref, xn_ref) + _rms(mix, gpm_ref[...])
    row = jax.lax.broadcasted_iota(jnp.int32, (rows, 1), 0)
    inside = ((row >= HALO) | (i > 0)) & ((row < HALO + tm) | (i < pl.num_programs(1) - 1))
    hext = jnp.where(inside, _rms(x1, gpf_ref[...]), 0.0).astype(BF16)
    h = hext[HALO:HALO + tm]
    for c in range(D_FF // ffc):
        sl = slice(c * ffc, (c + 1) * ffc)
        g = _dot(hext, wg_ref[:, sl])
        u = _dot(h, wu_ref[:, sl])
        g_prev = pltpu.roll(g, 1, 0)[HALO:HALO + tm]
        g_next = pltpu.roll(g, rows - 1, 0)[HALO:HALO + tm]
        gate = (cb_ref[:, sl] + g_prev * cw_ref[0:1, sl] + g[HALO:HALO + tm] * cw_ref[1:2, sl]
                + g_next * cw_ref[2:3, sl])
        inner = math.sqrt(2.0 / math.pi) * (gate + 0.044715 * (gate * gate * gate))
        act = 0.5 * gate * (1.0 + jnp.tanh(inner)) * u
        act_ref[:, sl] = act.astype(BF16)
    out = _dot(act_ref[...], wd_ref[...])
    y_ref[0] = x1[HALO:HALO + tm] + _rms(out, gpo_ref[...])


def _tail(a, f, x, woa, wof, gpm, gpf, wg, wu, cw, cb, wd, gpo, tm, ffc):
    b, s, _ = x.shape
    per = tm // HALO
    last = s // HALO - 1
    single = lambda shape: pl.BlockSpec(shape, lambda *_: (0,) * len(shape),
                                        pipeline_mode=pl.Buffered(1))

    def with_halo(width):
        return [pl.BlockSpec((1, tm, width), lambda bi, i: (bi, i, 0)),
                pl.BlockSpec((1, HALO, width), lambda bi, i: (bi, jnp.maximum(i * per - 1, 0), 0)),
                pl.BlockSpec((1, HALO, width), lambda bi, i: (bi, jnp.minimum((i + 1) * per, last), 0))]

    grouped = lambda rows, row_block: pl.BlockSpec(
        (1, FNET_GROUPS, rows, FNET_GROUP_DIM), lambda bi, i: (bi, 0, row_block(i), 0))
    f_specs = [grouped(tm, lambda i: i),
               grouped(HALO, lambda i: jnp.maximum(i * per - 1, 0)),
               grouped(HALO, lambda i: jnp.minimum((i + 1) * per, last))]
    return pl.pallas_call(
        functools.partial(_tail_kernel, ffc=ffc),
        grid=(b, s // tm),
        in_specs=with_halo(MLA_WIDTH) + f_specs + with_halo(D_MODEL) + [
            single(woa.shape), single(wof.shape), _const_spec(gpm.shape), _const_spec(gpf.shape),
            single(wg.shape), single(wu.shape), _const_spec(cw.shape), _const_spec(cb.shape),
            single(wd.shape), _const_spec(gpo.shape)],
        out_specs=pl.BlockSpec((1, tm, D_MODEL), lambda bi, i: (bi, i, 0)),
        out_shape=jax.ShapeDtypeStruct(x.shape, F32),
        scratch_shapes=[pltpu.VMEM((tm, D_FF), BF16)],
        compiler_params=_params(("parallel", "arbitrary")),
        name="tail",
    )(a, a, a, f, f, f, x, x, x, woa, wof, gpm, gpf, wg, wu, cw, cb, wd, gpo)


def _rope_tables(s):
    ang = np.arange(s, dtype=np.float64)[:, None] * (
        ROPE_THETA ** (-np.arange(0, QK_ROPE, 2, dtype=np.float64) / QK_ROPE))[None, :]
    cos, sin = np.cos(ang), np.sin(ang)
    zeros = np.zeros((s, QK_NOPE))
    zh = np.zeros((s, HALF_ROPE))
    tail = np.zeros((s, HEAD_PAD - QK_NOPE - QK_ROPE))
    cos_t = np.concatenate([np.ones((s, QK_NOPE)), cos, cos, tail], axis=1)
    sina = np.concatenate([zeros, zh, sin, tail], axis=1)
    sinb = np.concatenate([zeros, -sin, zh, tail], axis=1)
    return tuple(jnp.asarray(t, F32) for t in (cos_t, sina, sinb, cos.T, sin.T))


def _angles(num, den):
    return (2.0 * np.pi / den) * (num % den).astype(np.float64)


def _dft_tables(s):
    n2 = DFT_N2
    n1 = s // n2
    c = np.arange(FNET_GROUP_DIM, dtype=np.int64)
    ang = _angles(c[:, None] * c[None, :], FNET_GROUP_DIM)
    cs = np.concatenate([np.cos(ang), np.sin(ang)], axis=1) / math.sqrt(FNET_GROUP_DIM)
    k1 = np.arange(n1, dtype=np.int64)
    a1 = _angles(k1[:, None] * k1[None, :], n1)
    c1, s1 = np.cos(a1), np.sin(a1)
    f1 = np.concatenate([np.concatenate([c1, -s1], axis=1),
                         np.concatenate([s1, c1], axis=1)], axis=0) / math.sqrt(n1)
    k2 = np.arange(n2, dtype=np.int64)
    kk = k1[:, None, None] + n1 * k2[None, :, None]
    a2 = _angles(kk * k2[None, None, :], s)
    w = np.concatenate([np.cos(a2), -np.sin(a2)], axis=2) / math.sqrt(n2)
    return tuple(jnp.asarray(t, F32) for t in (cs, f1, w))


def _prep_weights(g_pre_mix, w_in, g_q, w_uq, g_kv, w_ukv, w_fnet, w_out, g_post_mix,
                  g_pre_ffn, w_gate, w_up, conv_w, conv_b, w_down, g_post_ffn):
    win = w_in[0]
    f_lo = Q_LORA + KV_LORA + QK_ROPE
    kr_cols = jnp.concatenate([jnp.zeros((D_MODEL, QK_NOPE), F32),
                               win[:, Q_LORA + KV_LORA:f_lo],
                               jnp.zeros((D_MODEL, HEAD_PAD - QK_NOPE - QK_ROPE), F32)], axis=1)
    win_p = jnp.concatenate([win[:, :Q_LORA + KV_LORA], win[:, f_lo:], kr_cols], axis=1)
    wuq = w_uq[0].reshape(Q_LORA, MLA_HEADS, QK_NOPE + QK_ROPE)
    wuq = jnp.pad(wuq, ((0, 0), (0, 0), (0, HEAD_PAD - QK_NOPE - QK_ROPE)))
    wuq = wuq.reshape(Q_LORA, MLA_HEADS * HEAD_PAD)
    wukv = w_ukv[0].reshape(KV_LORA, MLA_HEADS, QK_NOPE + V_HEAD)
    wuk = jnp.pad(wukv[..., :QK_NOPE], ((0, 0), (0, 0), (0, HEAD_PAD - QK_NOPE)))
    wuk = wuk.reshape(KV_LORA, MLA_HEADS * HEAD_PAD)
    wuv = wukv[..., QK_NOPE:].reshape(KV_LORA, MLA_WIDTH)
    return dict(
        gpre=g_pre_mix, win=win_p.astype(BF16), gq=g_q, wuqt=wuq.T.astype(BF16), gkv=g_kv,
        wuk=wuk.astype(BF16), wuvt=wuv.T.astype(BF16), wf=w_fnet[0].astype(BF16),
        woa=w_out[0, :MLA_WIDTH].astype(BF16), wof=w_out[0, MLA_WIDTH:].astype(BF16),
        gpm=g_post_mix, gpf=g_pre_ffn, wg=w_gate[0].astype(BF16), wu=w_up[0].astype(BF16),
        cw=conv_w[0], cb=conv_b, wd=w_down[0].astype(BF16), gpo=g_post_ffn)


def _trunk(x, w):
    b, s, _ = x.shape
    n2 = DFT_N2
    n1 = s // n2
    rope_tabs = _rope_tables(s)
    cs, f1, wtab = _dft_tables(s)
    qt, k, vt, fin = _proj(x, w["gpre"], w["win"], w["gq"], w["wuqt"], w["gkv"], w["wuk"],
                           w["wuvt"], rope_tabs, tm=PROJ_TILE)
    a = _attn(qt, k, vt, tq=MXU_TILE, tk=s // SCORE_SLOTS, qsubs=ATTN_STEP_TILES // (s // MXU_TILE))
    y = _dft1(fin.reshape(b, FNET_GROUPS, n1, n2, FNET_GROUP_DIM), cs, f1, r=DFT1_TOKENS // n1)
    f = _dft2(y, wtab, w["wf"], kb=DFT2_ROWS)
    f = f.reshape(b, FNET_GROUPS, s, FNET_GROUP_DIM)
    return _tail(a, f, x, w["woa"], w["wof"], w["gpm"], w["gpf"], w["wg"], w["wu"], w["cw"],
                 w["cb"], w["wd"], w["gpo"], tm=TOKEN_TILE, ffc=FF_CHUNK)


def kernel(x_prompt, x_sample, g_pre_mix, w_in, g_q, w_uq, g_kv, w_ukv, w_fnet, w_out,
           g_post_mix, g_pre_ffn, w_gate, w_up, conv_w, conv_b, w_down, g_post_ffn):
    w = _prep_weights(g_pre_mix, w_in, g_q, w_uq, g_kv, w_ukv, w_fnet, w_out, g_post_mix,
                      g_pre_ffn, w_gate, w_up, conv_w, conv_b, w_down, g_post_ffn)
    return _trunk(x_prompt, w), _trunk(x_sample, w)
```
